```python
import jax, jax.numpy as jnp
from jax import lax
import numpy as np

D_MODEL = 2048
BATCH = 2
SEQ = 4096
DEPTH = 4
DEC_BATCH = 8
DEC_SEQ = 1
PAST_LEN = 16384
PAGE_SIZE = 128

RWKV_HEADS = 8
RWKV_HEAD_DIM = 64
RWKV_WIDTH = RWKV_HEADS * RWKV_HEAD_DIM
RWKV_DECAY_RANK = 64
RWKV_ICLR_RANK = 64
RWKV_GATE_RANK = 128
RWKV_PROJ = 3 * RWKV_WIDTH + RWKV_DECAY_RANK + RWKV_ICLR_RANK + RWKV_GATE_RANK
RWKV_GN_EPS = 64e-5
MOBA_HEADS = 8
MOBA_HEAD_DIM = 128
MOBA_WIDTH = MOBA_HEADS * MOBA_HEAD_DIM
MOBA_BLOCK = 256
MOBA_TOPK = 3
MOBA_Q_CHUNK = 32
ROPE_THETA = 10000.0
GLA_HEADS = 4
GLA_KEY_DIM = 64
GLA_VALUE_DIM = 128
GLA_K_WIDTH = GLA_HEADS * GLA_KEY_DIM
GLA_V_WIDTH = GLA_HEADS * GLA_VALUE_DIM
GLA_GATE_RANK = 16
GLA_TAU = 16.0
GLA_CHUNK = 64
GLA_PROJ = 2 * GLA_K_WIDTH + 2 * GLA_V_WIDTH + GLA_GATE_RANK
N_BRANCHES = 3
IN_WIDTH = RWKV_PROJ + 3 * MOBA_WIDTH + GLA_PROJ + N_BRANCHES * D_MODEL
FFN_HIDDEN = ((8 * D_MODEL + 3 * 256 - 1) // (3 * 256)) * 256
NORM_EPS = 1e-6
NEG_INF = -1e30

kernel_name = 'hybrid_rwkv7_moba_gla_decoder_step'


def rms_norm(x, g):
    xf = x.astype(jnp.float32)
    y = xf * lax.rsqrt(jnp.mean(xf * xf, axis=-1, keepdims=True) + NORM_EPS)
    return (y * g.astype(jnp.float32)).astype(x.dtype)


def rope(x, pos):
    half = x.shape[-1] // 2
    inv = ROPE_THETA ** (-jnp.arange(half, dtype=jnp.float32) / half)
    ang = pos.astype(jnp.float32)[:, None] * inv[None, :]
    cos = jnp.cos(ang)[None, :, None, :]
    sin = jnp.sin(ang)[None, :, None, :]
    xf = x.astype(jnp.float32)
    x1, x2 = xf[..., :half], xf[..., half:]
    return jnp.concatenate([x1 * cos - x2 * sin, x2 * cos + x1 * sin], axis=-1).astype(x.dtype)


def rwkv7_scan(r, w, k, v, kk, b, S0):
    def step(S, inp):
        r_t, w_t, k_t, v_t, kk_t, b_t = inp
        sa = jnp.einsum('bhij,bhj->bhi', S, -kk_t)
        S = S * w_t[:, :, None, :] + sa[..., None] * b_t[:, :, None, :] + v_t[..., None] * k_t[:, :, None, :]
        return S, jnp.einsum('bhij,bhj->bhi', S, r_t)
    xs = tuple(jnp.swapaxes(t, 0, 1) for t in (r, w, k, v, kk, b))
    S, ys = lax.scan(step, S0, xs)
    return jnp.swapaxes(ys, 0, 1), S


def rwkv7_branch(P, shift0, S0, lw):
    B, T, _ = P.shape
    prev = jnp.concatenate([shift0[:, None, :].astype(P.dtype), P[:, :-1]], axis=1)
    Pm = (P + (prev - P) * lw['rwkv_mu']).astype(jnp.float32)
    W = RWKV_WIDTH
    r, k, v, wd, ad, gd = jnp.split(Pm, [W, 2 * W, 3 * W, 3 * W + RWKV_DECAY_RANK,
                                        3 * W + RWKV_DECAY_RANK + RWKV_ICLR_RANK], axis=-1)
    w_val = -jax.nn.softplus(-(lw['rwkv_w0'] + jnp.tanh(wd) @ lw['rwkv_w_up'])) - 0.5
    decay = jnp.exp(-jnp.exp(w_val))
    a = jax.nn.sigmoid(lw['rwkv_a0'] + ad @ lw['rwkv_a_up'])
    g = jax.nn.sigmoid(gd) @ lw['rwkv_g_up']
    heads = lambda t: t.reshape(B, T, RWKV_HEADS, RWKV_HEAD_DIM).astype(jnp.float32)
    kk = heads(k * lw['rwkv_k_k'])
    kk = kk / jnp.maximum(jnp.sqrt(jnp.sum(kk * kk, axis=-1, keepdims=True)), 1e-12)
    k = k * (1.0 + (a - 1.0) * lw['rwkv_k_a'])
    rh, wh, kh, vh, ah = heads(r), heads(decay), heads(k), heads(v), heads(a)
    y, S = rwkv7_scan(rh, wh, kh, vh, kk, kk * ah, S0.astype(jnp.float32))
    mu = jnp.mean(y, axis=-1, keepdims=True)
    var = jnp.mean(jnp.square(y - mu), axis=-1, keepdims=True)
    y = ((y - mu) * lax.rsqrt(var + RWKV_GN_EPS)).reshape(B, T, W) * lw['rwkv_ln_w'] + lw['rwkv_ln_b']
    r_k = lw['rwkv_r_k'].reshape(RWKV_HEADS, RWKV_HEAD_DIM).astype(jnp.float32)
    bonus = jnp.sum(rh * kh * r_k, axis=-1, keepdims=True) * vh
    y = (y + bonus.reshape(B, T, W)) * g
    return y.astype(P.dtype), S.astype(P.dtype), P[:, -1]


def gla_chunked(q, k, v, log_a, S0):
    B, T, H, dk = q.shape
    dv = v.shape[-1]
    C = min(GLA_CHUNK, T)
    nc = -(-T // C)
    pad = nc * C - T
    def blocks(t):
        t = jnp.pad(t, ((0, 0), (0, pad), (0, 0), (0, 0)))
        return t.reshape(B, nc, C, H, t.shape[-1]).transpose(1, 0, 3, 2, 4)
    causal = jnp.tril(jnp.ones((C, C), dtype=bool))[:, :, None]
    def step(S, inp):
        qc, kc, vc, gc = inp
        cum = jnp.cumsum(gc, axis=2)
        dec = jnp.exp(jnp.where(causal, cum[:, :, :, None, :] - cum[:, :, None, :, :], NEG_INF))
        att = jnp.einsum('bhijd,bhjd->bhij', qc[:, :, :, None, :] * dec, kc)
        o = jnp.einsum('bhij,bhjv->bhiv', att, vc) + jnp.einsum('bhid,bhdv->bhiv', qc * jnp.exp(cum), S)
        last = cum[:, :, -1:, :]
        S = S * jnp.exp(last[:, :, 0, :, None]) + jnp.einsum('bhjd,bhjv->bhdv', kc * jnp.exp(last - cum), vc)
        return S, o
    S, o = lax.scan(step, S0, (blocks(q), blocks(k), blocks(v), blocks(log_a)))
    o = o.transpose(1, 0, 3, 2, 4).reshape(B, nc * C, H, dv)[:, :T]
    return o, S


def gla_branch(G, S0, lw):
    B, T, _ = G.shape
    KW, VW = GLA_K_WIDTH, GLA_V_WIDTH
    q, k, v, gl, r = jnp.split(G.astype(jnp.float32), [KW, 2 * KW, 2 * KW + VW, 2 * KW + VW + GLA_GATE_RANK], axis=-1)
    log_a = jax.nn.log_sigmoid(gl @ lw['gla_a_up'] + lw['gla_a_bias']) / GLA_TAU
    hk = lambda t: t.reshape(B, T, GLA_HEADS, GLA_KEY_DIM)
    q = hk(q) * (GLA_KEY_DIM ** -0.5)
    o, S = gla_chunked(q, hk(k), v.reshape(B, T, GLA_HEADS, GLA_VALUE_DIM), hk(log_a), S0.astype(jnp.float32))
    o = rms_norm(o, lw['gla_o_norm']).reshape(B, T, VW) * jax.nn.silu(r)
    return o.astype(G.dtype), S.astype(G.dtype)


def moba_attention(q, k, v, pos0):
    B, Q, H, Dh = q.shape
    L = k.shape[1]
    nb = -(-L // MOBA_BLOCK)
    kpad = ((0, 0), (0, nb * MOBA_BLOCK - L), (0, 0), (0, 0))
    kb = jnp.pad(k, kpad).reshape(B, nb, MOBA_BLOCK, H, Dh).transpose(0, 3, 1, 2, 4)
    vb = jnp.pad(v, kpad).reshape(B, nb, MOBA_BLOCK, H, Dh).transpose(0, 3, 1, 2, 4)
    kmean = jnp.mean(kb, axis=3, dtype=jnp.float32)
    n_sel = min(MOBA_TOPK, nb)
    qc = min(MOBA_Q_CHUNK, Q)
    nq = -(-Q // qc)
    qp = jnp.pad(q, ((0, 0), (0, nq * qc - Q), (0, 0), (0, 0))).astype(jnp.float32)
    pos = jnp.minimum(pos0 + jnp.arange(nq * qc), L - 1)
    scale = MOBA_HEAD_DIM ** -0.5
    slot = jnp.arange(n_sel + 1)
    gather = jax.vmap(jax.vmap(lambda blk, i: blk[i]))

    def one_chunk(args):
        qch, pch = args
        cur = pch // MOBA_BLOCK
        bs = jnp.einsum('bqhd,bhnd->bhqn', qch, kmean)
        past = jnp.arange(nb)[None, :] < cur[:, None]
        bs = jnp.where(past[None, None], bs, NEG_INF)
        _, sel = lax.top_k(bs, n_sel)
        own = jnp.broadcast_to(cur[None, None, :, None], (B, H, qc, 1)).astype(sel.dtype)
        idx = jnp.concatenate([sel, own], axis=-1)
        kg = gather(kb, idx).astype(jnp.float32)
        vg = gather(vb, idx).astype(jnp.float32)
        s = jnp.einsum('bqhd,bhqsnd->bhqsn', qch, kg) * scale
        sel_ok = (slot[None, :] < cur[:, None]) | (slot[None, :] == n_sel)
        keypos = cur[:, None] * MOBA_BLOCK + jnp.arange(MOBA_BLOCK)[None, :]
        own_ok = keypos <= pch[:, None]
        mask = sel_ok[:, :, None] & jnp.where((slot == n_sel)[None, :, None], own_ok[:, None, :], True)
        s = jnp.where(mask[None, None], s, NEG_INF)
        p = jax.nn.softmax(s.reshape(B, H, qc, -1), axis=-1).reshape(s.shape)
        return jnp.einsum('bhqsn,bhqsnd->bqhd', p, vg)

    qs = qp.reshape(B, nq, qc, H, Dh).transpose(1, 0, 2, 3, 4)
    out = lax.map(one_chunk, (qs, pos.reshape(nq, qc)))
    return out.transpose(1, 0, 2, 3, 4).reshape(B, nq * qc, H, Dh)[:, :Q].astype(q.dtype)


def moba_branch(M, pos0, k_past, v_past, lw):
    B, T, _ = M.shape
    q, k, v = (t.reshape(B, T, MOBA_HEADS, MOBA_HEAD_DIM) for t in jnp.split(M, 3, axis=-1))
    pos = pos0 + jnp.arange(T)
    q = rope(rms_norm(q, lw['moba_q_norm']), pos)
    k = rope(rms_norm(k, lw['moba_k_norm']), pos)
    if k_past is None:
        k_all, v_all = k, v
    else:
        k_all = jnp.concatenate([k_past, k.astype(k_past.dtype)], axis=1)
        v_all = jnp.concatenate([v_past, v.astype(v_past.dtype)], axis=1)
    o = moba_attention(q, k_all, v_all, pos0)
    return o.reshape(B, T, MOBA_WIDTH), k, v


def trunk_layer(x, pos0, shift0, rwkv_S0, gla_S0, k_past, v_past, lw):
    B, T, D = x.shape
    h = rms_norm(x, lw['norm_mix'])
    proj = h @ lw['w_in']
    o1 = RWKV_PROJ
    o2 = o1 + 3 * MOBA_WIDTH
    o3 = o2 + GLA_PROJ
    P, M, G, Z = jnp.split(proj, [o1, o2, o3], axis=-1)
    yA, rwkv_S, shift = rwkv7_branch(P, shift0, rwkv_S0, lw)
    yB, k_new, v_new = moba_branch(M, pos0, k_past, v_past, lw)
    yC, gla_S = gla_branch(G, gla_S0, lw)
    gates = jax.nn.sigmoid(Z.astype(jnp.float32)).reshape(B, T, N_BRANCHES, D)
    merged = (gates[:, :, 0] * (yA @ lw['w_up_rwkv']) + gates[:, :, 1] * (yB @ lw['w_up_moba'])
              + gates[:, :, 2] * (yC @ lw['w_up_gla']))
    x = x + (merged.astype(x.dtype) @ lw['w_out']).astype(x.dtype)
    h2 = rms_norm(x, lw['norm_ffn'])
    u_gate, u_val = jnp.split(h2 @ lw['w_ffn_in'], 2, axis=-1)
    x = x + ((jax.nn.silu(u_gate) * u_val) @ lw['w_ffn_out']).astype(x.dtype)
    return x, k_new, v_new, rwkv_S, shift, gla_S


def setup_inputs(seed: int = 0) -> dict:
    key = jax.random.key(seed)
    ks = iter(jax.random.split(key, 48))
    nrm = lambda shape, scale: jax.random.normal(next(ks), shape, jnp.float32) * scale
    n_pages = PAST_LEN // PAGE_SIZE
    n_used = DEC_BATCH * n_pages
    n_pool = n_used + max(1, n_used // 4)
    perm = jax.random.permutation(next(ks), n_pool)
    page_table = perm[:n_used].reshape(DEC_BATCH, n_pages).astype(jnp.int32)
    L = DEPTH
    return {
        'x_prompt': nrm((BATCH, SEQ, D_MODEL), 1.0),
        'x_sample': nrm((DEC_BATCH, DEC_SEQ, D_MODEL), 1.0),
        'cache_k': nrm((L, n_pool, PAGE_SIZE, MOBA_HEADS, MOBA_HEAD_DIM), 1.0),
        'cache_v': nrm((L, n_pool, PAGE_SIZE, MOBA_HEADS, MOBA_HEAD_DIM), 1.0),
        'page_table': page_table,
        'state_rwkv': nrm((L, DEC_BATCH, RWKV_HEADS, RWKV_HEAD_DIM, RWKV_HEAD_DIM), 0.1),
        'state_rwkv_shift': nrm((L, DEC_BATCH, RWKV_PROJ), 1.0),
        'state_gla': nrm((L, DEC_BATCH, GLA_HEADS, GLA_KEY_DIM, GLA_VALUE_DIM), 0.1),
        'norm_mix': 1.0 + nrm((L, D_MODEL), 0.05),
        'w_in': nrm((L, D_MODEL, IN_WIDTH), D_MODEL ** -0.5),
        'rwkv_mu': jax.random.uniform(next(ks), (L, RWKV_PROJ), jnp.float32),
        'rwkv_w0': nrm((L, RWKV_WIDTH), 1.0),
        'rwkv_w_up': nrm((L, RWKV_DECAY_RANK, RWKV_WIDTH), 0.1 * RWKV_DECAY_RANK ** -0.5),
        'rwkv_a0': nrm((L, RWKV_WIDTH), 0.5),
        'rwkv_a_up': nrm((L, RWKV_ICLR_RANK, RWKV_WIDTH), 0.1 * RWKV_ICLR_RANK ** -0.5),
        'rwkv_g_up': nrm((L, RWKV_GATE_RANK, RWKV_WIDTH), RWKV_GATE_RANK ** -0.5),
        'rwkv_k_k': 0.85 + nrm((L, RWKV_WIDTH), 0.05),
        'rwkv_k_a': 1.0 + nrm((L, RWKV_WIDTH), 0.05),
        'rwkv_r_k': nrm((L, RWKV_WIDTH), 0.1),
        'rwkv_ln_w': 1.0 + nrm((L, RWKV_WIDTH), 0.05),
        'rwkv_ln_b': nrm((L, RWKV_WIDTH), 0.01),
        'moba_q_norm': 1.0 + nrm((L, MOBA_HEAD_DIM), 0.05),
        'moba_k_norm': 1.0 + nrm((L, MOBA_HEAD_DIM), 0.05),
        'gla_a_up': nrm((L, GLA_GATE_RANK, GLA_K_WIDTH), GLA_GATE_RANK ** -0.5),
        'gla_a_bias': nrm((L, GLA_K_WIDTH), 0.1),
        'gla_o_norm': 1.0 + nrm((L, GLA_VALUE_DIM), 0.05),
        'w_up_rwkv': nrm((L, RWKV_WIDTH, D_MODEL), RWKV_WIDTH ** -0.5),
        'w_up_moba': nrm((L, MOBA_WIDTH, D_MODEL), MOBA_WIDTH ** -0.5),
        'w_up_gla': nrm((L, GLA_V_WIDTH, D_MODEL), GLA_V_WIDTH ** -0.5),
        'w_out': nrm((L, D_MODEL, D_MODEL), D_MODEL ** -0.5),
        'norm_ffn': 1.0 + nrm((L, D_MODEL), 0.05),
        'w_ffn_in': nrm((L, D_MODEL, 2 * FFN_HIDDEN), D_MODEL ** -0.5),
        'w_ffn_out': nrm((L, FFN_HIDDEN, D_MODEL), FFN_HIDDEN ** -0.5),
    }


def reference(x_prompt, x_sample, cache_k, cache_v, page_table, state_rwkv, state_rwkv_shift, state_gla,
              norm_mix, w_in, rwkv_mu, rwkv_w0, rwkv_w_up, rwkv_a0, rwkv_a_up, rwkv_g_up, rwkv_k_k, rwkv_k_a,
              rwkv_r_k, rwkv_ln_w, rwkv_ln_b, moba_q_norm, moba_k_norm, gla_a_up, gla_a_bias, gla_o_norm,
              w_up_rwkv, w_up_moba, w_up_gla, w_out, norm_ffn, w_ffn_in, w_ffn_out):
    Bp = x_prompt.shape[0]
    Bs = x_sample.shape[0]
    past_len = page_table.shape[1] * cache_k.shape[2]
    dt = x_prompt.dtype
    yp, ys = x_prompt, x_sample
    kp_l, vp_l, rp_l, sp_l, gp_l = [], [], [], [], []
    ks_l, vs_l, rs_l, ss_l, gs_l = [], [], [], [], []
    for l in range(DEPTH):
        lw = {
            'norm_mix': norm_mix[l], 'w_in': w_in[l], 'rwkv_mu': rwkv_mu[l], 'rwkv_w0': rwkv_w0[l],
            'rwkv_w_up': rwkv_w_up[l], 'rwkv_a0': rwkv_a0[l], 'rwkv_a_up': rwkv_a_up[l],
            'rwkv_g_up': rwkv_g_up[l], 'rwkv_k_k': rwkv_k_k[l], 'rwkv_k_a': rwkv_k_a[l],
            'rwkv_r_k': rwkv_r_k[l], 'rwkv_ln_w': rwkv_ln_w[l], 'rwkv_ln_b': rwkv_ln_b[l],
            'moba_q_norm': moba_q_norm[l], 'moba_k_norm': moba_k_norm[l], 'gla_a_up': gla_a_up[l],
            'gla_a_bias': gla_a_bias[l], 'gla_o_norm': gla_o_norm[l], 'w_up_rwkv': w_up_rwkv[l],
            'w_up_moba': w_up_moba[l], 'w_up_gla': w_up_gla[l], 'w_out': w_out[l],
            'norm_ffn': norm_ffn[l], 'w_ffn_in': w_ffn_in[l], 'w_ffn_out': w_ffn_out[l],
        }
        yp, k_new, v_new, r_S, shift, g_S = trunk_layer(
            yp, 0,
            jnp.zeros((Bp, RWKV_PROJ), dt),
            jnp.zeros((Bp, RWKV_HEADS, RWKV_HEAD_DIM, RWKV_HEAD_DIM), dt),
            jnp.zeros((Bp, GLA_HEADS, GLA_KEY_DIM, GLA_VALUE_DIM), dt),
            None, None, lw)
        kp_l.append(k_new); vp_l.append(v_new); rp_l.append(r_S); sp_l.append(shift); gp_l.append(g_S)
        k_past = cache_k[l, page_table].reshape(Bs, past_len, MOBA_HEADS, MOBA_HEAD_DIM)
        v_past = cache_v[l, page_table].reshape(Bs, past_len, MOBA_HEADS, MOBA_HEAD_DIM)
        ys, k_new, v_new, r_S, shift, g_S = trunk_layer(
            ys, past_len, state_rwkv_shift[l], state_rwkv[l], state_gla[l], k_past, v_past, lw)
        ks_l.append(k_new); vs_l.append(v_new); rs_l.append(r_S); ss_l.append(shift); gs_l.append(g_S)
    return (yp, ys,
            jnp.stack(kp_l), jnp.stack(vp_l), jnp.stack(rp_l), jnp.stack(sp_l), jnp.stack(gp_l),
            jnp.stack(ks_l), jnp.stack(vs_l), jnp.stack(rs_l), jnp.stack(ss_l), jnp.stack(gs_l))
```

```python
import functools

import jax
import jax.numpy as jnp
from jax import lax
from jax.experimental import pallas as pl
from jax.experimental.pallas import tpu as pltpu

F32 = jnp.float32
BF16 = jnp.bfloat16
HIGHEST = lax.Precision.HIGHEST

PAGE_SIZE = 128
RWKV_HEADS = 8
RWKV_HEAD_DIM = 64
RWKV_WIDTH = RWKV_HEADS * RWKV_HEAD_DIM
RWKV_DECAY_RANK = 64
RWKV_ICLR_RANK = 64
RWKV_GATE_RANK = 128
RWKV_PROJ = 3 * RWKV_WIDTH + RWKV_DECAY_RANK + RWKV_ICLR_RANK + RWKV_GATE_RANK
RWKV_GN_EPS = 64e-5
MOBA_HEADS = 8
MOBA_HEAD_DIM = 128
MOBA_WIDTH = MOBA_HEADS * MOBA_HEAD_DIM
MOBA_BLOCK = 256
MOBA_TOPK = 3
ROPE_THETA = 10000.0
GLA_HEADS = 4
GLA_KEY_DIM = 64
GLA_VALUE_DIM = 128
GLA_K_WIDTH = GLA_HEADS * GLA_KEY_DIM
GLA_V_WIDTH = GLA_HEADS * GLA_VALUE_DIM
GLA_GATE_RANK = 16
GLA_GATE_PAD = 256
GLA_SECT = 2 * GLA_K_WIDTH + 2 * GLA_V_WIDTH + GLA_GATE_PAD
GLA_TAU = 16.0
N_BRANCHES = 3
NORM_EPS = 1e-6
NEG_INF = -1e30

VMEM_LIMIT_BYTES = 56 * 1024 * 1024
SUBLANES = 8

REC_PRECISION = HIGHEST


def _cparams(*sem):
    return pltpu.CompilerParams(dimension_semantics=sem, vmem_limit_bytes=VMEM_LIMIT_BYTES)


def _dot(a, b, precision=None):
    return jnp.dot(a, b, preferred_element_type=F32, precision=precision)


def _dot_nt(a, b, precision=None):
    return lax.dot_general(a, b, (((1,), (1,)), ((), ())), preferred_element_type=F32, precision=precision)


def _dot_tn(a, b, precision=None):
    return lax.dot_general(a, b, (((0,), (0,)), ((), ())), preferred_element_type=F32, precision=precision)


def _split3(x):
    hi = x.astype(BF16)
    r1 = x - hi.astype(F32)
    mid = r1.astype(BF16)
    lo = (r1 - mid.astype(F32)).astype(BF16)
    return hi, mid, lo


def _tril_ones(n, strict=False):
    r = lax.broadcasted_iota(jnp.int32, (n, n), 0)
    c = lax.broadcasted_iota(jnp.int32, (n, n), 1)
    return (r > c) if strict else (r >= c)


def _cumsum_rows(x, tri_bf16):
    hi, mid, lo = _split3(x)
    return _dot(tri_bf16, hi) + _dot(tri_bf16, mid) + _dot(tri_bf16, lo)


def _sigmoid(x):
    return 1.0 / (1.0 + jnp.exp(-x))


def _softplus(x):
    return jnp.maximum(x, 0.0) + jnp.log(1.0 + jnp.exp(-jnp.abs(x)))


def _row_tile(m, pref):
    return pref if m % pref == 0 else m


def _act_dtype(tile_rows):
    return BF16 if tile_rows % (2 * SUBLANES) == 0 else F32


def _norm_matmul_kernel(x_ref, g_ref, w_ref, o_ref, h_ref):
    @pl.when(pl.program_id(1) == 0)
    def _():
        x = x_ref[...]
        ms = jnp.mean(x * x, axis=-1, keepdims=True)
        h_ref[...] = (x * lax.rsqrt(ms + NORM_EPS) * g_ref[...]).astype(BF16)

    o_ref[...] = _dot(h_ref[...], w_ref[...]).astype(o_ref.dtype)


def norm_matmul(x, g, w, tn, out_dtype=F32, tm_pref=512):
    m, d = x.shape
    n = w.shape[1]
    tm = _row_tile(m, tm_pref)
    return pl.pallas_call(
        _norm_matmul_kernel,
        out_shape=jax.ShapeDtypeStruct((m, n), out_dtype),
        grid=(m // tm, n // tn),
        in_specs=[
            pl.BlockSpec((tm, d), lambda i, j: (i, 0)),
            pl.BlockSpec((1, d), lambda i, j: (0, 0)),
            pl.BlockSpec((d, tn), lambda i, j: (0, j)),
        ],
        out_specs=pl.BlockSpec((tm, tn), lambda i, j: (i, j)),
        scratch_shapes=[pltpu.VMEM((tm, d), BF16)],
        compiler_params=_cparams("parallel", "arbitrary"),
        name="norm_matmul",
    )(x, g.reshape(1, d), w)


def _rwkv_kernel(p_ref, shift0_ref, s0_ref, mu_ref, w0_ref, wup_ref, a0_ref, aup_ref, gup_ref,
                 kk_ref, ka_ref, rk_ref, lnw_ref, lnb_ref,
                 y_ref, sout_ref, shout_ref,
                 carry_ref, s_ref, r_s, k_s, v_s, kk_s, a_s, lw_s, y_s,
                 *, t_real, tt, chunk):
    t = pl.program_id(1)
    nt = pl.num_programs(1)
    W = RWKV_WIDTH
    N = RWKV_HEAD_DIM
    C = chunk
    prec = REC_PRECISION

    @pl.when(t == 0)
    def _():
        carry_ref[...] = shift0_ref[0]
        s_ref[...] = s0_ref[0]

    P = p_ref[...]
    rowid = lax.broadcasted_iota(jnp.int32, (tt, 1), 0)
    prev = jnp.where(rowid == 0, carry_ref[...], pltpu.roll(P, 1, axis=0))
    last_row = (t_real - 1) % tt
    carry_ref[...] = P[last_row:last_row + 1, :]

    @pl.when(t == nt - 1)
    def _():
        shout_ref[0] = P[last_row:last_row + 1, :]

    Pm = P + (prev - P) * mu_ref[...]
    r = Pm[:, 0:W]
    k = Pm[:, W:2 * W]
    v = Pm[:, 2 * W:3 * W]
    o = 3 * W
    wd = Pm[:, o:o + RWKV_DECAY_RANK]
    ad = Pm[:, o + RWKV_DECAY_RANK:o + RWKV_DECAY_RANK + RWKV_ICLR_RANK]
    gd = Pm[:, o + RWKV_DECAY_RANK + RWKV_ICLR_RANK:]
    w_val = -_softplus(-(w0_ref[...] + _dot(jnp.tanh(wd), wup_ref[...], prec))) - 0.5
    lw = -jnp.exp(w_val)
    a = _sigmoid(a0_ref[...] + _dot(ad, aup_ref[...], prec))
    g = _dot(_sigmoid(gd), gup_ref[...], prec)
    kk = k * kk_ref[...]
    kmod = k * (1.0 + (a - 1.0) * ka_ref[...])
    if t_real % tt != 0:
        valid = (t * tt + rowid) < t_real
        lw = jnp.where(valid, lw, 0.0)
        kk = jnp.where(valid, kk, 0.0)
        kmod = jnp.where(valid, kmod, 0.0)
        v = jnp.where(valid, v, 0.0)
    r_s[...] = r
    k_s[...] = kmod
    v_s[...] = v
    kk_s[...] = kk
    a_s[...] = a
    lw_s[...] = lw

    tri = jnp.where(_tril_ones(C), 1.0, 0.0).astype(BF16)
    low_incl = _tril_ones(C)
    low_strict = _tril_ones(C, strict=True)
    mid = max(C // 2 - 1, 0)

    def chunk_body(c, carry):
        rows = pl.ds(pl.multiple_of(c * C, C), C)
        cum_all = _cumsum_rows(lw_s[rows, :], tri)
        for h in range(RWKV_HEADS):
            sl = slice(h * N, (h + 1) * N)
            lw_h = lw_s[rows, sl]
            cum = cum_all[:, sl]
            cume = cum - lw_h
            rho = cum[mid:mid + 1, :]
            cum_last = cum[C - 1:C, :]
            r_h = r_s[rows, sl]
            k_h = k_s[rows, sl]
            v_h = v_s[rows, sl]
            kk_h = kk_s[rows, sl]
            nrm = jnp.sqrt(jnp.sum(kk_h * kk_h, axis=-1, keepdims=True))
            kkn = kk_h / jnp.maximum(nrm, 1e-12)
            b_h = kkn * a_s[rows, sl]
            e_out = jnp.exp(rho - cum)
            lhs = jnp.concatenate([kkn * jnp.exp(cume - rho), r_h * jnp.exp(cum - rho)], axis=0)
            rhs = jnp.concatenate([k_h * e_out, b_h * e_out], axis=0)
            A = _dot_nt(lhs, rhs, prec)
            Lk = jnp.where(low_strict, A[0:C, 0:C], 0.0)
            Lb = jnp.where(low_strict, A[0:C, C:2 * C], 0.0)
            Ark = jnp.where(low_incl, A[C:2 * C, 0:C], 0.0)
            Arb = jnp.where(low_incl, A[C:2 * C, C:2 * C], 0.0)
            S = s_ref[h]
            lhs_abs = jnp.concatenate([kkn * jnp.exp(cume), r_h * jnp.exp(cum)], axis=0)
            SS = _dot_nt(lhs_abs, S, prec)
            X = SS[0:C] + _dot(Lk, v_h, prec)
            X = X - _dot(Lb, X, prec)
            Lp = Lb
            p = 2
            while p < C:
                Lp = _dot(Lp, Lp, prec)
                X = X + _dot(Lp, X, prec)
                p *= 2
            U = X
            Y = SS[C:2 * C] + _dot(jnp.concatenate([Ark, -Arb], axis=1),
                                   jnp.concatenate([v_h, U], axis=0), prec)
            e_end = jnp.exp(cum_last - cum)
            s_ref[h] = S * jnp.exp(cum_last) + _dot_tn(
                jnp.concatenate([v_h, -U], axis=0),
                jnp.concatenate([k_h * e_end, b_h * e_end], axis=0), prec)
            mu = jnp.mean(Y, axis=-1, keepdims=True)
            var = jnp.mean(jnp.square(Y - mu), axis=-1, keepdims=True)
            yn = (Y - mu) * lax.rsqrt(var + RWKV_GN_EPS) * lnw_ref[:, sl] + lnb_ref[:, sl]
            bonus = jnp.sum(r_h * k_h * rk_ref[:, sl], axis=-1, keepdims=True) * v_h
            y_s[rows, sl] = yn + bonus
        return carry

    lax.fori_loop(0, tt // C, chunk_body, 0)
    y_ref[...] = (y_s[...] * g).astype(y_ref.dtype)

    @pl.when(t == nt - 1)
    def _():
        sout_ref[0] = s_ref[...]


def rwkv_branch(P, shift0, S0, lw, *, batch, t_pad, t_real, tt, chunk):
    nt = t_pad // tt
    W = RWKV_WIDTH
    vec = lambda a: a.reshape(1, -1)
    full = lambda shape: pl.BlockSpec(shape, lambda b, t: (0,) * len(shape))
    kern = functools.partial(_rwkv_kernel, t_real=t_real, tt=tt, chunk=chunk)
    y, s_out, sh_out = pl.pallas_call(
        kern,
        out_shape=(
            jax.ShapeDtypeStruct((batch * t_pad, W), _act_dtype(tt)),
            jax.ShapeDtypeStruct((batch, RWKV_HEADS, RWKV_HEAD_DIM, RWKV_HEAD_DIM), F32),
            jax.ShapeDtypeStruct((batch, 1, RWKV_PROJ), F32),
        ),
        grid=(batch, nt),
        in_specs=[
            pl.BlockSpec((tt, RWKV_PROJ), lambda b, t: (b * nt + t, 0)),
            pl.BlockSpec((1, 1, RWKV_PROJ), lambda b, t: (b, 0, 0)),
            pl.BlockSpec((1, RWKV_HEADS, RWKV_HEAD_DIM, RWKV_HEAD_DIM), lambda b, t: (b, 0, 0, 0)),
            full((1, RWKV_PROJ)),
            full((1, W)),
            full((RWKV_DECAY_RANK, W)),
            full((1, W)),
            full((RWKV_ICLR_RANK, W)),
            full((RWKV_GATE_RANK, W)),
            full((1, W)), full((1, W)), full((1, W)), full((1, W)), full((1, W)),
        ],
        out_specs=(
            pl.BlockSpec((tt, W), lambda b, t: (b * nt + t, 0)),
            pl.BlockSpec((1, RWKV_HEADS, RWKV_HEAD_DIM, RWKV_HEAD_DIM), lambda b, t: (b, 0, 0, 0)),
            pl.BlockSpec((1, 1, RWKV_PROJ), lambda b, t: (b, 0, 0)),
        ),
        scratch_shapes=[
            pltpu.VMEM((1, RWKV_PROJ), F32),
            pltpu.VMEM((RWKV_HEADS, RWKV_HEAD_DIM, RWKV_HEAD_DIM), F32),
        ] + [pltpu.VMEM((tt, W), F32) for _ in range(7)],
        compiler_params=_cparams("parallel", "arbitrary"),
        name="rwkv7_branch",
    )(P, shift0.reshape(batch, 1, RWKV_PROJ), S0,
      vec(lw['rwkv_mu']), vec(lw['rwkv_w0']), lw['rwkv_w_up'], vec(lw['rwkv_a0']), lw['rwkv_a_up'],
      lw['rwkv_g_up'], vec(lw['rwkv_k_k']), vec(lw['rwkv_k_a']), vec(lw['rwkv_r_k']),
      vec(lw['rwkv_ln_w']), vec(lw['rwkv_ln_b']))
    return y, s_out, sh_out.reshape(batch, RWKV_PROJ)


def _gla_kernel(g_ref, s0_ref, aup_ref, abias_ref, onorm_ref,
                y_ref, sout_ref,
                s_ref, la_s,
                *, t_real, tt, chunk, sub):
    t = pl.program_id(1)
    nt = pl.num_programs(1)
    KW, VW = GLA_K_WIDTH, GLA_V_WIDTH
    dk, dv = GLA_KEY_DIM, GLA_VALUE_DIM
    C = chunk
    prec = REC_PRECISION

    @pl.when(t == 0)
    def _():
        s_ref[...] = s0_ref[0]

    gl = g_ref[:, 2 * KW + 2 * VW:]
    x = _dot(gl, aup_ref[...], prec) + abias_ref[...]
    la = -_softplus(-x) * (1.0 / GLA_TAU)
    padded = t_real % tt != 0
    if padded:
        rowid = lax.broadcasted_iota(jnp.int32, (tt, 1), 0)
        valid = (t * tt + rowid) < t_real
        la = jnp.where(valid, la, 0.0)
    la_s[...] = la

    tri = jnp.where(_tril_ones(C), 1.0, 0.0).astype(BF16)
    ones_cv = jnp.ones((C, dv), BF16)
    low_incl = _tril_ones(C)
    crow = lax.broadcasted_iota(jnp.int32, (C, 1), 0)
    nsub = C // sub

    def chunk_body(c, carry):
        rows = pl.ds(pl.multiple_of(c * C, C), C)
        la_c = la_s[rows, :]
        cum_all = _cumsum_rows(la_c, tri)
        if padded:
            vmask = (t * tt + c * C + crow) < t_real
        for h in range(GLA_HEADS):
            ks = slice(h * dk, (h + 1) * dk)
            q_h = g_ref[rows, h * dk:(h + 1) * dk] * (dk ** -0.5)
            k_h = g_ref[rows, KW + h * dk:KW + (h + 1) * dk]
            v_h = g_ref[rows, 2 * KW + h * dv:2 * KW + (h + 1) * dv]
            r_h = g_ref[rows, 2 * KW + VW + h * dv:2 * KW + VW + (h + 1) * dv]
            if padded:
                k_h = jnp.where(vmask, k_h, 0.0)
                v_h = jnp.where(vmask, v_h, 0.0)
            cum = cum_all[:, ks]
            cum_last = cum[C - 1:C, :]
            bounds = [jnp.zeros((1, dk), F32)] + [cum[i * sub - 1:i * sub, :] for i in range(1, nsub)]
            bref = jnp.concatenate([jnp.broadcast_to(b, (sub, dk)) for b in bounds], axis=0)
            qt = q_h * jnp.exp(cum - bref)
            att_rows = []
            for i in range(nsub):
                kt = k_h * jnp.exp(jnp.where(crow < (i + 1) * sub, bounds[i] - cum, NEG_INF))
                att_rows.append(_dot_nt(qt[i * sub:(i + 1) * sub], kt, prec))
            att = jnp.where(low_incl, jnp.concatenate(att_rows, axis=0), 0.0)
            S = s_ref[h]
            o = _dot(att, v_h, prec) + _dot(q_h * jnp.exp(cum), S, prec)
            hi, mid, lo = _split3(la_c[:, ks])
            tot = _dot_tn(hi, ones_cv) + _dot_tn(mid, ones_cv) + _dot_tn(lo, ones_cv)
            s_ref[h] = S * jnp.exp(tot) + _dot_tn(k_h * jnp.exp(cum_last - cum), v_h, prec)
            ms = jnp.mean(o * o, axis=-1, keepdims=True)
            on = o * lax.rsqrt(ms + NORM_EPS) * onorm_ref[...]
            y_ref[rows, h * dv:(h + 1) * dv] = (on * (r_h * _sigmoid(r_h))).astype(y_ref.dtype)
        return carry

    lax.fori_loop(0, tt // C, chunk_body, 0)

    @pl.when(t == nt - 1)
    def _():
        sout_ref[0] = s_ref[...]


def gla_branch(G, S0, lw, *, batch, t_pad, t_real, tt, chunk, sub):
    nt = t_pad // tt
    full = lambda shape: pl.BlockSpec(shape, lambda b, t: (0,) * len(shape))
    aup = jnp.zeros((GLA_GATE_PAD, GLA_K_WIDTH), F32).at[:GLA_GATE_RANK].set(lw['gla_a_up'])
    kern = functools.partial(_gla_kernel, t_real=t_real, tt=tt, chunk=chunk, sub=sub)
    y, s_out = pl.pallas_call(
        kern,
        out_shape=(
            jax.ShapeDtypeStruct((batch * t_pad, GLA_V_WIDTH), _act_dtype(tt)),
            jax.ShapeDtypeStruct((batch, GLA_HEADS, GLA_KEY_DIM, GLA_VALUE_DIM), F32),
        ),
        grid=(batch, nt),
        in_specs=[
            pl.BlockSpec((tt, GLA_SECT), lambda b, t: (b * nt + t, 0)),
            pl.BlockSpec((1, GLA_HEADS, GLA_KEY_DIM, GLA_VALUE_DIM), lambda b, t: (b, 0, 0, 0)),
            full((GLA_GATE_PAD, GLA_K_WIDTH)),
            full((1, GLA_K_WIDTH)),
            full((1, GLA_VALUE_DIM)),
        ],
        out_specs=(
            pl.BlockSpec((tt, GLA_V_WIDTH), lambda b, t: (b * nt + t, 0)),
            pl.BlockSpec((1, GLA_HEADS, GLA_KEY_DIM, GLA_VALUE_DIM), lambda b, t: (b, 0, 0, 0)),
        ),
        scratch_shapes=[
            pltpu.VMEM((GLA_HEADS, GLA_KEY_DIM, GLA_VALUE_DIM), F32),
            pltpu.VMEM((tt, GLA_K_WIDTH), F32),
        ],
        compiler_params=_cparams("parallel", "arbitrary"),
        name="gla_branch",
    )(G, S0, aup, lw['gla_a_bias'].reshape(1, -1), lw['gla_o_norm'].reshape(1, -1))
    return y, s_out


def _moba_prep_kernel(q_ref, k_ref, cos_ref, sin_ref, gq_ref, gk_ref, qo_ref, ko_ref, km_ref):
    Dh = MOBA_HEAD_DIM
    cos = cos_ref[...]
    sin = sin_ref[...]
    rows = q_ref.shape[0]

    def norm_rope(x, g):
        ms = jnp.mean(x * x, axis=-1, keepdims=True)
        y = x * lax.rsqrt(ms + NORM_EPS) * g
        return y * cos + pltpu.roll(y, Dh // 2, axis=1) * sin

    for h in range(MOBA_HEADS):
        sl = slice(h * Dh, (h + 1) * Dh)
        qo_ref[:, sl] = norm_rope(q_ref[:, sl], gq_ref[...])
        kr = norm_rope(k_ref[:, sl], gk_ref[...])
        ko_ref[:, sl] = kr
        km_ref[0, :, sl] = jnp.sum(kr, axis=0, keepdims=True) * (1.0 / rows)


def moba_prep(M, cos, sin, gq, gk, *, tile, n_pos_tiles):
    m = M.shape[0]
    W = MOBA_WIDTH
    nblk = m // tile
    return pl.pallas_call(
        _moba_prep_kernel,
        out_shape=(
            jax.ShapeDtypeStruct((m, W), F32),
            jax.ShapeDtypeStruct((m, W), F32),
            jax.ShapeDtypeStruct((nblk, 1, W), F32),
        ),
        grid=(nblk,),
        in_specs=[
            pl.BlockSpec((tile, W), lambda i: (i, 0)),
            pl.BlockSpec((tile, W), lambda i: (i, 1)),
            pl.BlockSpec((tile, MOBA_HEAD_DIM), lambda i: (i % n_pos_tiles, 0)),
            pl.BlockSpec((tile, MOBA_HEAD_DIM), lambda i: (i % n_pos_tiles, 0)),
            pl.BlockSpec((1, MOBA_HEAD_DIM), lambda i: (0, 0)),
            pl.BlockSpec((1, MOBA_HEAD_DIM), lambda i: (0, 0)),
        ],
        out_specs=(
            pl.BlockSpec((tile, W), lambda i: (i, 0)),
            pl.BlockSpec((tile, W), lambda i: (i, 0)),
            pl.BlockSpec((1, 1, W), lambda i: (i, 0, 0)),
        ),
        compiler_params=_cparams("parallel"),
        name="moba_qk_prep",
    )(M, M, cos, sin, gq.reshape(1, -1), gk.reshape(1, -1))


def _block_rank(bs, n_valid):
    nb = bs.shape[1]
    lane = lax.broadcasted_iota(jnp.int32, bs.shape, 1)
    rank = jnp.zeros(bs.shape, jnp.int32)
    for mm in range(nb):
        col = bs[:, mm:mm + 1]
        beats = (col > bs) | ((col == bs) & (lane > mm))
        if n_valid is not None:
            beats = beats & (mm < n_valid)
        rank = rank + jnp.where(beats, 1, 0)
    return rank


def _moba_attn_kernel(q_ref, k_ref, v_ref, km_ref, o_ref):
    i = pl.program_id(2)
    blk = MOBA_BLOCK
    scale = MOBA_HEAD_DIM ** -0.5
    q = q_ref[...]
    km = km_ref[0]
    nb = km.shape[0]
    bs = _dot_nt(q, km, HIGHEST)
    lane = lax.broadcasted_iota(jnp.int32, (blk, nb), 1)
    rank = _block_rank(bs, i)
    sel = jnp.where((lane < i) & (rank < MOBA_TOPK), 1.0, 0.0)
    qb = q.astype(BF16)

    own = pl.ds(pl.multiple_of(i * blk, blk), blk)
    s = _dot_nt(qb, k_ref[0, own, :].astype(BF16)) * scale
    s = jnp.where(_tril_ones(blk), s, NEG_INF)
    m0 = jnp.max(s, axis=-1, keepdims=True)
    p = jnp.exp(s - m0)
    l0 = jnp.sum(p, axis=-1, keepdims=True)
    acc0 = _dot(p.astype(BF16), v_ref[0, own, :].astype(BF16))

    def body(j, carry):
        m, l, acc = carry
        rows = pl.ds(pl.multiple_of(j * blk, blk), blk)
        picked = jnp.sum(jnp.where(lane == j, sel, 0.0), axis=-1, keepdims=True) > 0.0
        s = _dot_nt(qb, k_ref[0, rows, :].astype(BF16)) * scale
        s = jnp.where(picked, s, NEG_INF)
        m_new = jnp.maximum(m, jnp.max(s, axis=-1, keepdims=True))
        alpha = jnp.exp(m - m_new)
        p = jnp.where(picked, jnp.exp(s - m_new), 0.0)
        l = l * alpha + jnp.sum(p, axis=-1, keepdims=True)
        acc = acc * alpha + _dot(p.astype(BF16), v_ref[0, rows, :].astype(BF16))
        return m_new, l, acc

    m, l, acc = lax.fori_loop(0, i, body, (m0, l0, acc0))
    o_ref[...] = (acc / l).astype(o_ref.dtype)


def moba_attention_prompt(q_rot, k_rot, M, kmean, *, batch, seq):
    W = MOBA_WIDTH
    nq = seq // MOBA_BLOCK
    nb = kmean.shape[1]
    return pl.pallas_call(
        _moba_attn_kernel,
        out_shape=jax.ShapeDtypeStruct((batch * seq, W), BF16),
        grid=(batch, MOBA_HEADS, nq),
        in_specs=[
            pl.BlockSpec((MOBA_BLOCK, MOBA_HEAD_DIM), lambda b, h, i: (b * nq + i, h)),
            pl.BlockSpec((1, seq, MOBA_HEAD_DIM), lambda b, h, i: (b, 0, h)),
            pl.BlockSpec((1, seq, MOBA_HEAD_DIM), lambda b, h, i: (b, 0, 2 * MOBA_HEADS + h)),
            pl.BlockSpec((1, nb, MOBA_HEAD_DIM), lambda b, h, i: (b, 0, h)),
        ],
        out_specs=pl.BlockSpec((MOBA_BLOCK, MOBA_HEAD_DIM), lambda b, h, i: (b * nq + i, h)),
        compiler_params=_cparams("parallel", "parallel", "arbitrary"),
        name="moba_attention",
    )(q_rot, k_rot.reshape(batch, seq, W), M.reshape(batch, seq, 3 * W), kmean)


PAGES_PER_STEP = 16


def _kmean_pages_kernel(pt_ref, *refs):
    del pt_ref
    page_refs, o_ref = refs[:-1], refs[-1]
    per_blk = MOBA_BLOCK // PAGE_SIZE
    rows = []
    for n in range(len(page_refs) // per_blk):
        tot = page_refs[per_blk * n][0, 0].sum(axis=0, keepdims=True)
        for e in range(1, per_blk):
            tot = tot + page_refs[per_blk * n + e][0, 0].sum(axis=0, keepdims=True)
        rows.append(tot * (1.0 / MOBA_BLOCK))
    o_ref[0, 0] = jnp.concatenate(rows, axis=0)


def kmean_from_pages(cache_k4, page_table):
    L, _, _, W = cache_k4.shape
    bs, n_pages = page_table.shape
    per_blk = MOBA_BLOCK // PAGE_SIZE
    pps = min(PAGES_PER_STEP, n_pages)
    steps = n_pages // pps

    def page_spec(e):
        return pl.BlockSpec((1, 1, PAGE_SIZE, W),
                            lambda l, b, s, pt: (l, pt[b * n_pages + s * pps + e], 0, 0))

    return pl.pallas_call(
        _kmean_pages_kernel,
        out_shape=jax.ShapeDtypeStruct((L, bs, n_pages // per_blk, W), F32),
        grid_spec=pltpu.PrefetchScalarGridSpec(
            num_scalar_prefetch=1,
            grid=(L, bs, steps),
            in_specs=[page_spec(e) for e in range(pps)],
            out_specs=pl.BlockSpec((1, 1, pps // per_blk, W), lambda l, b, s, pt: (l, b, s, 0)),
        ),
        compiler_params=_cparams("parallel", "parallel", "arbitrary"),
        name="moba_paged_kmean",
    )(page_table.reshape(-1), *([cache_k4] * pps))


def _moba_select_kernel(q_ref, km_ref, o_ref):
    Dh = MOBA_HEAD_DIM
    nb = km_ref.shape[2]
    rows = []
    for h in range(MOBA_HEADS):
        sl = slice(h * Dh, (h + 1) * Dh)
        rows.append(_dot_nt(q_ref[0, :, sl], km_ref[0, 0, :, sl], HIGHEST))
    bs = jnp.concatenate(rows, axis=0)
    rank = _block_rank(bs, None)
    lane = lax.broadcasted_iota(jnp.int32, bs.shape, 1)
    olane = lax.broadcasted_iota(jnp.int32, (MOBA_HEADS, 128), 1)
    out = jnp.zeros((MOBA_HEADS, 128), jnp.int32)
    for s in range(min(MOBA_TOPK, nb)):
        idx = jnp.sum(jnp.where(rank == s, lane, 0), axis=-1, keepdims=True)
        out = jnp.where(olane == s, idx, out)
    o_ref[0] = out


def moba_select_sample(q3, kmean_l):
    bs, _, W = q3.shape
    nb = kmean_l.shape[2]
    return pl.pallas_call(
        _moba_select_kernel,
        out_shape=jax.ShapeDtypeStruct((bs, MOBA_HEADS, 128), jnp.int32),
        grid=(bs,),
        in_specs=[
            pl.BlockSpec((1, 1, W), lambda b: (b, 0, 0)),
            pl.BlockSpec((1, 1, nb, W), lambda b: (0, b, 0, 0)),
        ],
        out_specs=pl.BlockSpec((1, MOBA_HEADS, 128), lambda b: (b, 0, 0)),
        compiler_params=_cparams("parallel"),
        name="moba_select_sample",
    )(q3, kmean_l)


def _moba_decode_kernel(sel_ref, pt_ref, q_ref, kn_ref, vn_ref, kp_ref, vp_ref, o_ref, m_ref, l_ref, acc_ref):
    del sel_ref, pt_ref
    s = pl.program_id(2)
    scale = MOBA_HEAD_DIM ** -0.5
    q = q_ref[0]

    @pl.when(s == 0)
    def _():
        m_ref[...] = jnp.sum(q * kn_ref[0], axis=-1, keepdims=True) * scale
        l_ref[...] = jnp.ones((1, 1), F32)
        acc_ref[...] = vn_ref[0]

    sc = _dot_nt(q.astype(BF16), kp_ref[0, 0].astype(BF16)) * scale
    m = m_ref[...]
    m_new = jnp.maximum(m, jnp.max(sc, axis=-1, keepdims=True))
    alpha = jnp.exp(m - m_new)
    p = jnp.exp(sc - m_new)
    l_ref[...] = l_ref[...] * alpha + jnp.sum(p, axis=-1, keepdims=True)
    acc_ref[...] = acc_ref[...] * alpha + _dot(p.astype(BF16), vp_ref[0, 0].astype(BF16))
    m_ref[...] = m_new

    @pl.when(s == pl.num_programs(2) - 1)
    def _():
        o_ref[0] = (acc_ref[...] / l_ref[...]).astype(o_ref.dtype)


def moba_decode(sel_flat, pt_flat, q3, k3, M3, cache_k4, cache_v4, *, layer, n_pages, n_sel):
    bs, _, W = q3.shape
    per_blk = MOBA_BLOCK // PAGE_SIZE
    Dh = MOBA_HEAD_DIM

    def page_map(b, h, s, sel, pt):
        blk = sel[(b * MOBA_HEADS + h) * n_sel + s // per_blk]
        return (layer, pt[b * n_pages + blk * per_blk + s % per_blk], 0, h)

    return pl.pallas_call(
        _moba_decode_kernel,
        out_shape=jax.ShapeDtypeStruct((bs, 1, W), F32),
        grid_spec=pltpu.PrefetchScalarGridSpec(
            num_scalar_prefetch=2,
            grid=(bs, MOBA_HEADS, n_sel * per_blk),
            in_specs=[
                pl.BlockSpec((1, 1, Dh), lambda b, h, s, sel, pt: (b, 0, h)),
                pl.BlockSpec((1, 1, Dh), lambda b, h, s, sel, pt: (b, 0, h)),
                pl.BlockSpec((1, 1, Dh), lambda b, h, s, sel, pt: (b, 0, 2 * MOBA_HEADS + h)),
                pl.BlockSpec((1, 1, PAGE_SIZE, Dh), page_map),
                pl.BlockSpec((1, 1, PAGE_SIZE, Dh), page_map),
            ],
            out_specs=pl.BlockSpec((1, 1, Dh), lambda b, h, s, sel, pt: (b, 0, h)),
            scratch_shapes=[pltpu.VMEM((1, 1), F32), pltpu.VMEM((1, 1), F32), pltpu.VMEM((1, Dh), F32)],
        ),
        compiler_params=_cparams("parallel", "parallel", "arbitrary"),
        name="moba_decode",
    )(sel_flat, pt_flat, q3, k3, M3, cache_k4, cache_v4)


def _merge_kernel(ya_ref, yb_ref, yc_ref, wa_ref, wb_ref, wc_ref, z0_ref, z1_ref, z2_ref, o_ref):
    acc = _sigmoid(z0_ref[...]) * _dot(ya_ref[...].astype(BF16), wa_ref[...])
    acc = acc + _sigmoid(z1_ref[...]) * _dot(yb_ref[...].astype(BF16), wb_ref[...])
    acc = acc + _sigmoid(z2_ref[...]) * _dot(yc_ref[...].astype(BF16), wc_ref[...])
    o_ref[...] = acc.astype(o_ref.dtype)


def gated_merge(ya, yb, yc, wa, wb, wc, Z, *, tn=1024, tm_pref=512):
    m = ya.shape[0]
    d = wa.shape[1]
    tm = _row_tile(m, tm_pref)
    nj = d // tn
    return pl.pallas_call(
        _merge_kernel,
        out_shape=jax.ShapeDtypeStruct((m, d), BF16),
        grid=(m // tm, nj),
        in_specs=[
            pl.BlockSpec((tm, ya.shape[1]), lambda i, j: (i, 0)),
            pl.BlockSpec((tm, yb.shape[1]), lambda i, j: (i, 0)),
            pl.BlockSpec((tm, yc.shape[1]), lambda i, j: (i, 0)),
            pl.BlockSpec((wa.shape[0], tn), lambda i, j: (0, j)),
            pl.BlockSpec((wb.shape[0], tn), lambda i, j: (0, j)),
            pl.BlockSpec((wc.shape[0], tn), lambda i, j: (0, j)),
            pl.BlockSpec((tm, tn), lambda i, j: (i, j)),
            pl.BlockSpec((tm, tn), lambda i, j: (i, nj + j)),
            pl.BlockSpec((tm, tn), lambda i, j: (i, 2 * nj + j)),
        ],
        out_specs=pl.BlockSpec((tm, tn), lambda i, j: (i, j)),
        compiler_params=_cparams("parallel", "arbitrary"),
        name="gated_merge",
    )(ya, yb, yc, wa, wb, wc, Z, Z, Z)


def _matmul_residual_kernel(a_ref, w_ref, x_ref, o_ref):
    o_ref[...] = x_ref[...] + _dot(a_ref[...], w_ref[...])


def matmul_residual(a, w, x, *, tn, tm_pref=512):
    m, k = a.shape
    n = w.shape[1]
    tm = _row_tile(m, tm_pref)
    return pl.pallas_call(
        _matmul_residual_kernel,
        out_shape=jax.ShapeDtypeStruct((m, n), F32),
        grid=(m // tm, n // tn),
        in_specs=[
            pl.BlockSpec((tm, k), lambda i, j: (i, 0)),
            pl.BlockSpec((k, tn), lambda i, j: (0, j)),
            pl.BlockSpec((tm, tn), lambda i, j: (i, j)),
        ],
        out_specs=pl.BlockSpec((tm, tn), lambda i, j: (i, j)),
        compiler_params=_cparams("parallel", "arbitrary"),
        name="matmul_residual",
    )(a, w, x)


def _ffn_in_kernel(x_ref, g_ref, wg_ref, wv_ref, o_ref, h_ref):
    @pl.when(pl.program_id(1) == 0)
    def _():
        x = x_ref[...]
        ms = jnp.mean(x * x, axis=-1, keepdims=True)
        h_ref[...] = (x * lax.rsqrt(ms + NORM_EPS) * g_ref[...]).astype(BF16)

    h = h_ref[...]
    gate = _dot(h, wg_ref[...])
    val = _dot(h, wv_ref[...])
    o_ref[...] = (gate * _sigmoid(gate) * val).astype(o_ref.dtype)


def ffn_in(x, g, w, *, tn=512, tm_pref=512):
    m, d = x.shape
    hidden = w.shape[1] // 2
    tm = _row_tile(m, tm_pref)
    nj = hidden // tn
    return pl.pallas_call(
        _ffn_in_kernel,
        out_shape=jax.ShapeDtypeStruct((m, hidden), BF16),
        grid=(m // tm, nj),
        in_specs=[
            pl.BlockSpec((tm, d), lambda i, j: (i, 0)),
            pl.BlockSpec((1, d), lambda i, j: (0, 0)),
            pl.BlockSpec((d, tn), lambda i, j: (0, j)),
            pl.BlockSpec((d, tn), lambda i, j: (0, nj + j)),
        ],
        out_specs=pl.BlockSpec((tm, tn), lambda i, j: (i, j)),
        scratch_shapes=[pltpu.VMEM((tm, d), BF16)],
        compiler_params=_cparams("parallel", "arbitrary"),
        name="ffn_in_swiglu",
    )(x, g.reshape(1, d), w, w)


def _rope_tables(pos):
    half = MOBA_HEAD_DIM // 2
    inv = ROPE_THETA ** (-jnp.arange(half, dtype=F32) / half)
    ang = pos.astype(F32)[:, None] * inv[None, :]
    cos, sin = jnp.cos(ang), jnp.sin(ang)
    return jnp.concatenate([cos, cos], axis=-1), jnp.concatenate([-sin, sin], axis=-1)


def _layer_weights(lw):
    o1 = RWKV_PROJ
    o2 = o1 + 3 * MOBA_WIDTH
    o3 = o2 + 2 * GLA_K_WIDTH + GLA_V_WIDTH
    o4 = o3 + GLA_GATE_RANK
    o5 = o4 + GLA_V_WIDTH
    w_in = lw['w_in']
    d = w_in.shape[0]
    w_g = jnp.concatenate(
        [w_in[:, o2:o3], w_in[:, o4:o5], w_in[:, o3:o4],
         jnp.zeros((d, GLA_GATE_PAD - GLA_GATE_RANK), w_in.dtype)], axis=1)
    return {
        'w_p': w_in[:, :o1].astype(BF16),
        'w_m': w_in[:, o1:o2].astype(BF16),
        'w_g': w_g.astype(BF16),
        'w_z': w_in[:, o5:].astype(BF16),
        'w_up_rwkv': lw['w_up_rwkv'].astype(BF16),
        'w_up_moba': lw['w_up_moba'].astype(BF16),
        'w_up_gla': lw['w_up_gla'].astype(BF16),
        'w_out': lw['w_out'].astype(BF16),
        'w_ffn_in': lw['w_ffn_in'].astype(BF16),
        'w_ffn_out': lw['w_ffn_out'].astype(BF16),
    }


def _mix_and_ffn(x, ya, yb, yc, Z, lw, wb):
    merged = gated_merge(ya, yb, yc, wb['w_up_rwkv'], wb['w_up_moba'], wb['w_up_gla'], Z)
    x = matmul_residual(merged, wb['w_out'], x, tn=1024)
    act = ffn_in(x, lw['norm_ffn'], wb['w_ffn_in'])
    return matmul_residual(act, wb['w_ffn_out'], x, tn=512)


def _prompt_layer(x, lw, wb, rope, *, batch, seq):
    g = lw['norm_mix']
    P = norm_matmul(x, g, wb['w_p'], RWKV_PROJ)
    M = norm_matmul(x, g, wb['w_m'], MOBA_WIDTH)
    G = norm_matmul(x, g, wb['w_g'], GLA_SECT)
    Z = norm_matmul(x, g, wb['w_z'], 1024)

    tt = min(seq, 256)
    ya, r_S, shift = rwkv_branch(
        P, jnp.zeros((batch, RWKV_PROJ), F32),
        jnp.zeros((batch, RWKV_HEADS, RWKV_HEAD_DIM, RWKV_HEAD_DIM), F32), lw,
        batch=batch, t_pad=seq, t_real=seq, tt=tt, chunk=64)
    yc, g_S = gla_branch(
        G, jnp.zeros((batch, GLA_HEADS, GLA_KEY_DIM, GLA_VALUE_DIM), F32), lw,
        batch=batch, t_pad=seq, t_real=seq, tt=tt, chunk=64, sub=16)

    nq = seq // MOBA_BLOCK
    q_rot, k_rot, kmean = moba_prep(M, rope[0], rope[1], lw['moba_q_norm'], lw['moba_k_norm'],
                                    tile=MOBA_BLOCK, n_pos_tiles=nq)
    yb = moba_attention_prompt(q_rot, k_rot, M, kmean.reshape(batch, nq, MOBA_WIDTH), batch=batch, seq=seq)

    x = _mix_and_ffn(x, ya, yb, yc, Z, lw, wb)
    k_new = k_rot.reshape(batch, seq, MOBA_HEADS, MOBA_HEAD_DIM)
    v_new = M[:, 2 * MOBA_WIDTH:].reshape(batch, seq, MOBA_HEADS, MOBA_HEAD_DIM)
    return x, k_new, v_new, r_S, shift, g_S


def _sample_layer(x, lw, wb, rope, shift0, rwkv_S0, gla_S0, kmean_l, pt_flat, cache_k4, cache_v4,
                  *, layer, n_pages):
    bs = x.shape[0]
    g = lw['norm_mix']
    P = norm_matmul(x, g, wb['w_p'], RWKV_PROJ)
    M = norm_matmul(x, g, wb['w_m'], MOBA_WIDTH)
    G = norm_matmul(x, g, wb['w_g'], GLA_SECT)
    Z = norm_matmul(x, g, wb['w_z'], 1024)

    pad_rows = lambda a: jnp.pad(a[:, None, :], ((0, 0), (0, SUBLANES - 1), (0, 0))).reshape(bs * SUBLANES, -1)
    ya, r_S, shift = rwkv_branch(pad_rows(P), shift0, rwkv_S0, lw,
                                 batch=bs, t_pad=SUBLANES, t_real=1, tt=SUBLANES, chunk=SUBLANES)
    yc, g_S = gla_branch(pad_rows(G), gla_S0, lw,
                         batch=bs, t_pad=SUBLANES, t_real=1, tt=SUBLANES, chunk=SUBLANES, sub=SUBLANES)
    ya = ya[::SUBLANES]
    yc = yc[::SUBLANES]

    q_rot, k_rot, _ = moba_prep(M, rope[0], rope[1], lw['moba_q_norm'], lw['moba_k_norm'],
                                tile=bs, n_pos_tiles=1)
    q3 = q_rot.reshape(bs, 1, MOBA_WIDTH)
    n_sel = min(MOBA_TOPK, kmean_l.shape[2])
    sel = moba_select_sample(q3, kmean_l)[:, :, :n_sel].reshape(-1)
    yb = moba_decode(sel, pt_flat, q3, k_rot.reshape(bs, 1, MOBA_WIDTH), M.reshape(bs, 1, 3 * MOBA_WIDTH),
                     cache_k4, cache_v4, layer=layer, n_pages=n_pages, n_sel=n_sel).reshape(bs, MOBA_WIDTH)

    x = _mix_and_ffn(x, ya, yb, yc, Z, lw, wb)
    k_new = k_rot.reshape(bs, 1, MOBA_HEADS, MOBA_HEAD_DIM)
    v_new = M[:, 2 * MOBA_WIDTH:].reshape(bs, 1, MOBA_HEADS, MOBA_HEAD_DIM)
    return x, k_new, v_new, r_S, shift, g_S


_LAYER_KEYS = ('norm_mix', 'w_in', 'rwkv_mu', 'rwkv_w0', 'rwkv_w_up', 'rwkv_a0', 'rwkv_a_up', 'rwkv_g_up',
               'rwkv_k_k', 'rwkv_k_a', 'rwkv_r_k', 'rwkv_ln_w', 'rwkv_ln_b', 'moba_q_norm', 'moba_k_norm',
               'gla_a_up', 'gla_a_bias', 'gla_o_norm', 'w_up_rwkv', 'w_up_moba', 'w_up_gla', 'w_out',
               'norm_ffn', 'w_ffn_in', 'w_ffn_out')


def kernel(x_prompt, x_sample, cache_k, cache_v, page_table, state_rwkv, state_rwkv_shift, state_gla, norm_mix, w_in, rwkv_mu, rwkv_w0, rwkv_w_up, rwkv_a0, rwkv_a_up, rwkv_g_up, rwkv_k_k, rwkv_k_a, rwkv_r_k, rwkv_ln_w, rwkv_ln_b, moba_q_norm, moba_k_norm, gla_a_up, gla_a_bias, gla_o_norm, w_up_rwkv, w_up_moba, w_up_gla, w_out, norm_ffn, w_ffn_in, w_ffn_out):
    stacked = dict(zip(_LAYER_KEYS, (
        norm_mix, w_in, rwkv_mu, rwkv_w0, rwkv_w_up, rwkv_a0, rwkv_a_up, rwkv_g_up, rwkv_k_k, rwkv_k_a,
        rwkv_r_k, rwkv_ln_w, rwkv_ln_b, moba_q_norm, moba_k_norm, gla_a_up, gla_a_bias, gla_o_norm,
        w_up_rwkv, w_up_moba, w_up_gla, w_out, norm_ffn, w_ffn_in, w_ffn_out)))
    depth = w_in.shape[0]
    bp, seq, d = x_prompt.shape
    bs, dec_seq, _ = x_sample.shape
    n_pages = page_table.shape[1]
    past_len = n_pages * cache_k.shape[2]
    assert dec_seq == 1 and cache_k.shape[2] == PAGE_SIZE
    assert seq % MOBA_BLOCK == 0 and past_len % MOBA_BLOCK == 0 and past_len // MOBA_BLOCK >= MOBA_TOPK

    n_pool = cache_k.shape[1]
    cache_k4 = cache_k.reshape(depth, n_pool, PAGE_SIZE, MOBA_WIDTH)
    cache_v4 = cache_v.reshape(depth, n_pool, PAGE_SIZE, MOBA_WIDTH)
    pt_flat = page_table.reshape(-1)
    kmean_past = kmean_from_pages(cache_k4, page_table)

    rope_p = _rope_tables(jnp.arange(seq))
    rope_s = _rope_tables(jnp.full((bs,), past_len))

    yp = x_prompt.reshape(bp * seq, d)
    ys = x_sample.reshape(bs, d)
    outs_p, outs_s = [], []
    for l in range(depth):
        lw = {k: v[l] for k, v in stacked.items()}
        wb = _layer_weights(lw)
        yp, *rest_p = _prompt_layer(yp, lw, wb, rope_p, batch=bp, seq=seq)
        outs_p.append(rest_p)
        ys, *rest_s = _sample_layer(ys, lw, wb, rope_s, state_rwkv_shift[l], state_rwkv[l], state_gla[l],
                                    kmean_past[l:l + 1], pt_flat, cache_k4, cache_v4, layer=l, n_pages=n_pages)
        outs_s.append(rest_s)
    stack = lambda outs, i: jnp.stack([o[i] for o in outs])
    return (yp.reshape(bp, seq, d), ys.reshape(bs, 1, d),
            stack(outs_p, 0), stack(outs_p, 1), stack(outs_p, 2), stack(outs_p, 3), stack(outs_p, 4),
            stack(outs_s, 0), stack(outs_s, 1), stack(outs_s, 2), stack(outs_s, 3), stack(outs_s, 4))
```

```python
import functools

import jax
import jax.numpy as jnp
from jax import lax
from jax.experimental import pallas as pl
from jax.experimental.pallas import tpu as pltpu

F32 = jnp.float32
BF16 = jnp.bfloat16
HIGHEST = lax.Precision.HIGHEST

PAGE_SIZE = 128
RWKV_HEADS = 8
RWKV_HEAD_DIM = 64
RWKV_WIDTH = RWKV_HEADS * RWKV_HEAD_DIM
RWKV_DECAY_RANK = 64
RWKV_ICLR_RANK = 64
RWKV_GATE_RANK = 128
RWKV_PROJ = 3 * RWKV_WIDTH + RWKV_DECAY_RANK + RWKV_ICLR_RANK + RWKV_GATE_RANK
RWKV_GN_EPS = 64e-5
MOBA_HEADS = 8
MOBA_HEAD_DIM = 128
MOBA_WIDTH = MOBA_HEADS * MOBA_HEAD_DIM
MOBA_BLOCK = 256
MOBA_TOPK = 3
ROPE_THETA = 10000.0
GLA_HEADS = 4
GLA_KEY_DIM = 64
GLA_VALUE_DIM = 128
GLA_K_WIDTH = GLA_HEADS * GLA_KEY_DIM
GLA_V_WIDTH = GLA_HEADS * GLA_VALUE_DIM
GLA_GATE_RANK = 16
GLA_GATE_PAD = 256
GLA_SECT = 2 * GLA_K_WIDTH + 2 * GLA_V_WIDTH + GLA_GATE_PAD
GLA_TAU = 16.0
N_BRANCHES = 3
NORM_EPS = 1e-6
NEG_INF = -1e30

VMEM_LIMIT_BYTES = 56 * 1024 * 1024
SUBLANES = 8

REC_PRECISION = HIGHEST
RWKV_CHUNK_PASSES = 1
GLA_CHUNK_PASSES = 1


def _cparams(*sem):
    return pltpu.CompilerParams(dimension_semantics=sem, vmem_limit_bytes=VMEM_LIMIT_BYTES)


def _dot(a, b, precision=None):
    return jnp.dot(a, b, preferred_element_type=F32, precision=precision)


def _dot_nt(a, b, precision=None):
    return lax.dot_general(a, b, (((1,), (1,)), ((), ())), preferred_element_type=F32, precision=precision)


def _dot_tn(a, b, precision=None):
    return lax.dot_general(a, b, (((0,), (0,)), ((), ())), preferred_element_type=F32, precision=precision)


def _mm(a, b, kind, passes):
    f = {'nn': _dot, 'nt': _dot_nt, 'tn': _dot_tn}[kind]
    if passes == 6:
        return f(a, b, HIGHEST)
    ah = a.astype(BF16)
    bh = b.astype(BF16)
    if passes == 1:
        return f(ah, bh)
    al = (a - ah.astype(F32)).astype(BF16)
    bl = (b - bh.astype(F32)).astype(BF16)
    return f(ah, bh) + f(ah, bl) + f(al, bh)


def _split3(x):
    hi = x.astype(BF16)
    r1 = x - hi.astype(F32)
    mid = r1.astype(BF16)
    lo = (r1 - mid.astype(F32)).astype(BF16)
    return hi, mid, lo


def _tril_ones(n, strict=False):
    r = lax.broadcasted_iota(jnp.int32, (n, n), 0)
    c = lax.broadcasted_iota(jnp.int32, (n, n), 1)
    return (r > c) if strict else (r >= c)


def _cumsum_rows(x, tri_bf16):
    hi, mid, lo = _split3(x)
    return _dot(tri_bf16, hi) + _dot(tri_bf16, mid) + _dot(tri_bf16, lo)


def _sigmoid(x):
    return 1.0 / (1.0 + jnp.exp(-x))


def _softplus(x):
    return jnp.maximum(x, 0.0) + jnp.log(1.0 + jnp.exp(-jnp.abs(x)))


def _row_tile(m, pref):
    return pref if m % pref == 0 else m


def _act_dtype(tile_rows):
    return BF16 if tile_rows % (2 * SUBLANES) == 0 else F32


def _norm_matmul_kernel(x_ref, g_ref, w_ref, o_ref, h_ref):
    @pl.when(pl.program_id(1) == 0)
    def _():
        x = x_ref[...]
        ms = jnp.mean(x * x, axis=-1, keepdims=True)
        h_ref[...] = (x * lax.rsqrt(ms + NORM_EPS) * g_ref[...]).astype(BF16)

    o_ref[...] = _dot(h_ref[...], w_ref[...]).astype(o_ref.dtype)


def norm_matmul(x, g, w, tn, out_dtype=F32, tm_pref=512):
    m, d = x.shape
    n = w.shape[1]
    tm = _row_tile(m, tm_pref)
    return pl.pallas_call(
        _norm_matmul_kernel,
        out_shape=jax.ShapeDtypeStruct((m, n), out_dtype),
        grid=(m // tm, n // tn),
        in_specs=[
            pl.BlockSpec((tm, d), lambda i, j: (i, 0)),
            pl.BlockSpec((1, d), lambda i, j: (0, 0)),
            pl.BlockSpec((d, tn), lambda i, j: (0, j)),
        ],
        out_specs=pl.BlockSpec((tm, tn), lambda i, j: (i, j)),
        scratch_shapes=[pltpu.VMEM((tm, d), BF16)],
        compiler_params=_cparams("parallel", "arbitrary"),
        name="norm_matmul",
    )(x, g.reshape(1, d), w)


def _rwkv_kernel(p_ref, shift0_ref, s0_ref, mu_ref, w0_ref, wup_ref, a0_ref, aup_ref, gup_ref,
                 kk_ref, ka_ref, rk_ref, lnw_ref, lnb_ref,
                 y_ref, sout_ref, shout_ref,
                 carry_ref, s_ref, r_s, k_s, v_s, kk_s, b_s, lw_s, y_s,
                 *, t_real, tt, chunk):
    t = pl.program_id(1)
    nt = pl.num_programs(1)
    W = RWKV_WIDTH
    N = RWKV_HEAD_DIM
    H = RWKV_HEADS
    C = chunk
    prec = REC_PRECISION
    cp = RWKV_CHUNK_PASSES

    @pl.when(t == 0)
    def _():
        carry_ref[...] = shift0_ref[0]
        s_ref[...] = s0_ref[0]

    P = p_ref[...]
    rowid = lax.broadcasted_iota(jnp.int32, (tt, 1), 0)
    prev = jnp.where(rowid == 0, carry_ref[...], pltpu.roll(P, 1, axis=0))
    last_row = (t_real - 1) % tt
    carry_ref[...] = P[last_row:last_row + 1, :]

    @pl.when(t == nt - 1)
    def _():
        shout_ref[0] = P[last_row:last_row + 1, :]

    Pm = P + (prev - P) * mu_ref[...]
    r = Pm[:, 0:W]
    k = Pm[:, W:2 * W]
    v = Pm[:, 2 * W:3 * W]
    o = 3 * W
    wd = Pm[:, o:o + RWKV_DECAY_RANK]
    ad = Pm[:, o + RWKV_DECAY_RANK:o + RWKV_DECAY_RANK + RWKV_ICLR_RANK]
    gd = Pm[:, o + RWKV_DECAY_RANK + RWKV_ICLR_RANK:]
    w_val = -_softplus(-(w0_ref[...] + _dot(jnp.tanh(wd), wup_ref[...], prec))) - 0.5
    lw = -jnp.exp(w_val)
    a = _sigmoid(a0_ref[...] + _dot(ad, aup_ref[...], prec))
    g = _dot(_sigmoid(gd), gup_ref[...], prec)
    kk = k * kk_ref[...]
    kmod = k * (1.0 + (a - 1.0) * ka_ref[...])
    if t_real % tt != 0:
        valid = (t * tt + rowid) < t_real
        lw = jnp.where(valid, lw, 0.0)
        kk = jnp.where(valid, kk, 0.0)
        kmod = jnp.where(valid, kmod, 0.0)
        v = jnp.where(valid, v, 0.0)
    seg_r = lax.broadcasted_iota(jnp.int32, (W, W), 0) // N
    seg_c = lax.broadcasted_iota(jnp.int32, (W, W), 1) // N
    seg = jnp.where(seg_r == seg_c, 1.0, 0.0).astype(BF16)
    hi, md, lo = _split3(kk * kk)
    ss = _dot(hi, seg) + _dot(md, seg) + _dot(lo, seg)
    kkn = kk / jnp.maximum(jnp.sqrt(ss), 1e-12)
    r_s[...] = r
    k_s[...] = kmod
    v_s[...] = v
    kk_s[...] = kkn
    b_s[...] = kkn * a
    lw_s[...] = lw

    tri = jnp.where(_tril_ones(C), 1.0, 0.0).astype(BF16)
    low_incl = _tril_ones(C)
    low_strict = _tril_ones(C, strict=True)
    eye = lax.broadcasted_iota(jnp.int32, (N, N), 0) == lax.broadcasted_iota(jnp.int32, (N, N), 1)
    mid = max(C // 2 - 1, 0)
    heads = range(H)
    hs = [slice(h * N, (h + 1) * N) for h in heads]

    def chunk_body(c, carry):
        rows = pl.ds(pl.multiple_of(c * C, C), C)
        lw_c = lw_s[rows, :]
        cum = _cumsum_rows(lw_c, tri)
        cume = cum - lw_c
        rho = cum[mid:mid + 1, :]
        cum_last = cum[C - 1:C, :]
        r_c = r_s[rows, :]
        k_c = k_s[rows, :]
        v_c = v_s[rows, :]
        kk_c = kk_s[rows, :]
        b_c = b_s[rows, :]
        e_out = jnp.exp(rho - cum)
        e_end = jnp.exp(cum_last - cum)
        kk_hat = kk_c * jnp.exp(cume - rho)
        r_hat = r_c * jnp.exp(cum - rho)
        k_til = k_c * e_out
        b_til = b_c * e_out
        kk_abs = kk_c * jnp.exp(cume)
        r_abs = r_c * jnp.exp(cum)
        k_end = k_c * e_end
        b_end = b_c * e_end
        gam = jnp.exp(cum_last)

        A = [_mm(jnp.concatenate([kk_hat[:, s], r_hat[:, s]], axis=0),
                 jnp.concatenate([k_til[:, s], b_til[:, s]], axis=0), 'nt', cp) for s in hs]
        Lk = [jnp.where(low_strict, x[0:C, 0:C], 0.0) for x in A]
        Lb = [jnp.where(low_strict, x[0:C, C:2 * C], 0.0) for x in A]
        Ark = [jnp.where(low_incl, x[C:2 * C, 0:C], 0.0) for x in A]
        Arb = [jnp.where(low_incl, x[C:2 * C, C:2 * C], 0.0) for x in A]
        X = [jnp.concatenate([kk_abs[:, s], _mm(Lk[h], v_c[:, s], 'nn', cp)], axis=1) for h, s in zip(heads, hs)]
        X = [X[h] - _mm(Lb[h], X[h], 'nn', cp) for h in heads]
        Lp = Lb
        p = 2
        while p < C:
            Lp = [_mm(x, x, 'nn', cp) for x in Lp]
            X = [X[h] + _mm(Lp[h], X[h], 'nn', cp) for h in heads]
            p *= 2
        RY = [jnp.concatenate([r_abs[:, s], _mm(Ark[h], v_c[:, s], 'nn', cp)], axis=1)
              - _mm(Arb[h], X[h], 'nn', cp) for h, s in zip(heads, hs)]
        MN = [_mm(X[h], b_end[:, s], 'tn', cp) for h, s in zip(heads, hs)]
        VK = [_mm(v_c[:, s], k_end[:, s], 'tn', cp) for s in hs]
        for h, s in zip(heads, hs):
            S = s_ref[h]
            Y = _dot_nt(RY[h][:, 0:N], S, HIGHEST) + RY[h][:, N:2 * N]
            trans = jnp.where(eye, jnp.broadcast_to(gam[:, s], (N, N)), 0.0) - MN[h][0:N]
            s_ref[h] = _dot(S, trans, HIGHEST) + (VK[h] - MN[h][N:2 * N])
            mu = jnp.mean(Y, axis=-1, keepdims=True)
            var = jnp.mean(jnp.square(Y - mu), axis=-1, keepdims=True)
            yn = (Y - mu) * lax.rsqrt(var + RWKV_GN_EPS) * lnw_ref[:, s] + lnb_ref[:, s]
            bonus = jnp.sum(r_c[:, s] * k_c[:, s] * rk_ref[:, s], axis=-1, keepdims=True) * v_c[:, s]
            y_s[rows, s] = yn + bonus
        return carry

    lax.fori_loop(0, tt // C, chunk_body, 0)
    y_ref[...] = (y_s[...] * g).astype(y_ref.dtype)

    @pl.when(t == nt - 1)
    def _():
        sout_ref[0] = s_ref[...]


def rwkv_branch(P, shift0, S0, lw, *, batch, t_pad, t_real, tt, chunk):
    nt = t_pad // tt
    W = RWKV_WIDTH
    vec = lambda a: a.reshape(1, -1)
    full = lambda shape: pl.BlockSpec(shape, lambda b, t: (0,) * len(shape))
    kern = functools.partial(_rwkv_kernel, t_real=t_real, tt=tt, chunk=chunk)
    y, s_out, sh_out = pl.pallas_call(
        kern,
        out_shape=(
            jax.ShapeDtypeStruct((batch * t_pad, W), _act_dtype(tt)),
            jax.ShapeDtypeStruct((batch, RWKV_HEADS, RWKV_HEAD_DIM, RWKV_HEAD_DIM), F32),
            jax.ShapeDtypeStruct((batch, 1, RWKV_PROJ), F32),
        ),
        grid=(batch, nt),
        in_specs=[
            pl.BlockSpec((tt, RWKV_PROJ), lambda b, t: (b * nt + t, 0)),
            pl.BlockSpec((1, 1, RWKV_PROJ), lambda b, t: (b, 0, 0)),
            pl.BlockSpec((1, RWKV_HEADS, RWKV_HEAD_DIM, RWKV_HEAD_DIM), lambda b, t: (b, 0, 0, 0)),
            full((1, RWKV_PROJ)),
            full((1, W)),
            full((RWKV_DECAY_RANK, W)),
            full((1, W)),
            full((RWKV_ICLR_RANK, W)),
            full((RWKV_GATE_RANK, W)),
            full((1, W)), full((1, W)), full((1, W)), full((1, W)), full((1, W)),
        ],
        out_specs=(
            pl.BlockSpec((tt, W), lambda b, t: (b * nt + t, 0)),
            pl.BlockSpec((1, RWKV_HEADS, RWKV_HEAD_DIM, RWKV_HEAD_DIM), lambda b, t: (b, 0, 0, 0)),
            pl.BlockSpec((1, 1, RWKV_PROJ), lambda b, t: (b, 0, 0)),
        ),
        scratch_shapes=[
            pltpu.VMEM((1, RWKV_PROJ), F32),
            pltpu.VMEM((RWKV_HEADS, RWKV_HEAD_DIM, RWKV_HEAD_DIM), F32),
        ] + [pltpu.VMEM((tt, W), F32) for _ in range(7)],
        compiler_params=_cparams("parallel", "arbitrary"),
        name="rwkv7_branch",
    )(P, shift0.reshape(batch, 1, RWKV_PROJ), S0,
      vec(lw['rwkv_mu']), vec(lw['rwkv_w0']), lw['rwkv_w_up'], vec(lw['rwkv_a0']), lw['rwkv_a_up'],
      lw['rwkv_g_up'], vec(lw['rwkv_k_k']), vec(lw['rwkv_k_a']), vec(lw['rwkv_r_k']),
      vec(lw['rwkv_ln_w']), vec(lw['rwkv_ln_b']))
    return y, s_out, sh_out.reshape(batch, RWKV_PROJ)


def _gla_kernel(g_ref, s0_ref, aup_ref, abias_ref, onorm_ref,
                y_ref, sout_ref,
                s_ref, la_s,
                *, t_real, tt, chunk, sub):
    t = pl.program_id(1)
    nt = pl.num_programs(1)
    KW, VW = GLA_K_WIDTH, GLA_V_WIDTH
    dk, dv = GLA_KEY_DIM, GLA_VALUE_DIM
    C = chunk
    cp = GLA_CHUNK_PASSES

    @pl.when(t == 0)
    def _():
        s_ref[...] = s0_ref[0]

    gl = g_ref[:, 2 * KW + 2 * VW:]
    x = _dot(gl, aup_ref[...], REC_PRECISION) + abias_ref[...]
    la = -_softplus(-x) * (1.0 / GLA_TAU)
    padded = t_real % tt != 0
    if padded:
        rowid = lax.broadcasted_iota(jnp.int32, (tt, 1), 0)
        valid = (t * tt + rowid) < t_real
        la = jnp.where(valid, la, 0.0)
    la_s[...] = la

    tri = jnp.where(_tril_ones(C), 1.0, 0.0).astype(BF16)
    ones_cv = jnp.ones((C, dv), BF16)
    low_incl = _tril_ones(C)
    crow = lax.broadcasted_iota(jnp.int32, (C, 1), 0)
    nsub = C // sub
    heads = range(GLA_HEADS)

    def chunk_body(c, carry):
        rows = pl.ds(pl.multiple_of(c * C, C), C)
        la_c = la_s[rows, :]
        cum_all = _cumsum_rows(la_c, tri)
        if padded:
            vmask = (t * tt + c * C + crow) < t_real
        q, k, v, cums = [], [], [], []
        for h in heads:
            q.append(g_ref[rows, h * dk:(h + 1) * dk] * (dk ** -0.5))
            k_h = g_ref[rows, KW + h * dk:KW + (h + 1) * dk]
            v_h = g_ref[rows, 2 * KW + h * dv:2 * KW + (h + 1) * dv]
            if padded:
                k_h = jnp.where(vmask, k_h, 0.0)
                v_h = jnp.where(vmask, v_h, 0.0)
            k.append(k_h)
            v.append(v_h)
            cums.append(cum_all[:, h * dk:(h + 1) * dk])
        bounds = [[jnp.zeros((1, dk), F32)] + [cm[i * sub - 1:i * sub, :] for i in range(1, nsub)] for cm in cums]
        qt = [q[h] * jnp.exp(cums[h] - jnp.concatenate(
            [jnp.broadcast_to(b, (sub, dk)) for b in bounds[h]], axis=0)) for h in heads]
        att = []
        for h in heads:
            att_rows = []
            for i in range(nsub):
                kt = k[h] * jnp.exp(jnp.where(crow < (i + 1) * sub, bounds[h][i] - cums[h], NEG_INF))
                att_rows.append(_mm(qt[h][i * sub:(i + 1) * sub], kt, 'nt', cp))
            att.append(jnp.where(low_incl, jnp.concatenate(att_rows, axis=0), 0.0))
        intra = [_mm(att[h], v[h], 'nn', cp) for h in heads]
        kv = [_mm(k[h] * jnp.exp(cums[h][C - 1:C, :] - cums[h]), v[h], 'tn', cp) for h in heads]
        tot = []
        for h in heads:
            hi, md, lo = _split3(la_c[:, h * dk:(h + 1) * dk])
            tot.append(_dot_tn(hi, ones_cv) + _dot_tn(md, ones_cv) + _dot_tn(lo, ones_cv))
        for h in heads:
            S = s_ref[h]
            o = intra[h] + _dot(q[h] * jnp.exp(cums[h]), S, HIGHEST)
            s_ref[h] = S * jnp.exp(tot[h]) + kv[h]
            r_h = g_ref[rows, 2 * KW + VW + h * dv:2 * KW + VW + (h + 1) * dv]
            ms = jnp.mean(o * o, axis=-1, keepdims=True)
            on = o * lax.rsqrt(ms + NORM_EPS) * onorm_ref[...]
            y_ref[rows, h * dv:(h + 1) * dv] = (on * (r_h * _sigmoid(r_h))).astype(y_ref.dtype)
        return carry

    lax.fori_loop(0, tt // C, chunk_body, 0)

    @pl.when(t == nt - 1)
    def _():
        sout_ref[0] = s_ref[...]


def gla_branch(G, S0, lw, *, batch, t_pad, t_real, tt, chunk, sub):
    nt = t_pad // tt
    full = lambda shape: pl.BlockSpec(shape, lambda b, t: (0,) * len(shape))
    aup = jnp.zeros((GLA_GATE_PAD, GLA_K_WIDTH), F32).at[:GLA_GATE_RANK].set(lw['gla_a_up'])
    kern = functools.partial(_gla_kernel, t_real=t_real, tt=tt, chunk=chunk, sub=sub)
    y, s_out = pl.pallas_call(
        kern,
        out_shape=(
            jax.ShapeDtypeStruct((batch * t_pad, GLA_V_WIDTH), _act_dtype(tt)),
            jax.ShapeDtypeStruct((batch, GLA_HEADS, GLA_KEY_DIM, GLA_VALUE_DIM), F32),
        ),
        grid=(batch, nt),
        in_specs=[
            pl.BlockSpec((tt, GLA_SECT), lambda b, t: (b * nt + t, 0)),
            pl.BlockSpec((1, GLA_HEADS, GLA_KEY_DIM, GLA_VALUE_DIM), lambda b, t: (b, 0, 0, 0)),
            full((GLA_GATE_PAD, GLA_K_WIDTH)),
            full((1, GLA_K_WIDTH)),
            full((1, GLA_VALUE_DIM)),
        ],
        out_specs=(
            pl.BlockSpec((tt, GLA_V_WIDTH), lambda b, t: (b * nt + t, 0)),
            pl.BlockSpec((1, GLA_HEADS, GLA_KEY_DIM, GLA_VALUE_DIM), lambda b, t: (b, 0, 0, 0)),
        ),
        scratch_shapes=[
            pltpu.VMEM((GLA_HEADS, GLA_KEY_DIM, GLA_VALUE_DIM), F32),
            pltpu.VMEM((tt, GLA_K_WIDTH), F32),
        ],
        compiler_params=_cparams("parallel", "arbitrary"),
        name="gla_branch",
    )(G, S0, aup, lw['gla_a_bias'].reshape(1, -1), lw['gla_o_norm'].reshape(1, -1))
    return y, s_out


def _norm_rope(x, g, cos, sin):
    ms = jnp.mean(x * x, axis=-1, keepdims=True)
    y = x * lax.rsqrt(ms + NORM_EPS) * g
    return y * cos + pltpu.roll(y, MOBA_HEAD_DIM // 2, axis=1) * sin


def _moba_prep_prompt_kernel(q_ref, k_ref, v_ref, cos_ref, sin_ref, gq_ref, gk_ref,
                             qo_ref, ko_ref, kb_ref, vt_ref, km_ref):
    Dh = MOBA_HEAD_DIM
    cos = cos_ref[...]
    sin = sin_ref[...]
    rows = q_ref.shape[0]
    for h in range(MOBA_HEADS):
        sl = slice(h * Dh, (h + 1) * Dh)
        qo_ref[:, sl] = _norm_rope(q_ref[:, sl], gq_ref[...], cos, sin)
        kr = _norm_rope(k_ref[:, sl], gk_ref[...], cos, sin)
        ko_ref[:, sl] = kr
        kb_ref[:, sl] = kr.astype(BF16)
        km_ref[0, :, sl] = jnp.sum(kr, axis=0, keepdims=True) * (1.0 / rows)
        vt_ref[0, h, 0] = v_ref[:, sl].T.astype(BF16)


def moba_prep_prompt(M, cos, sin, gq, gk, *, batch, seq):
    m = M.shape[0]
    W = MOBA_WIDTH
    blk = MOBA_BLOCK
    Dh = MOBA_HEAD_DIM
    nq = seq // blk
    nblk = m // blk
    return pl.pallas_call(
        _moba_prep_prompt_kernel,
        out_shape=(
            jax.ShapeDtypeStruct((m, W), F32),
            jax.ShapeDtypeStruct((m, W), F32),
            jax.ShapeDtypeStruct((m, W), BF16),
            jax.ShapeDtypeStruct((batch, MOBA_HEADS, nq, Dh, blk), BF16),
            jax.ShapeDtypeStruct((nblk, 1, W), F32),
        ),
        grid=(nblk,),
        in_specs=[
            pl.BlockSpec((blk, W), lambda i: (i, 0)),
            pl.BlockSpec((blk, W), lambda i: (i, 1)),
            pl.BlockSpec((blk, W), lambda i: (i, 2)),
            pl.BlockSpec((blk, Dh), lambda i: (i % nq, 0)),
            pl.BlockSpec((blk, Dh), lambda i: (i % nq, 0)),
            pl.BlockSpec((1, Dh), lambda i: (0, 0)),
            pl.BlockSpec((1, Dh), lambda i: (0, 0)),
        ],
        out_specs=(
            pl.BlockSpec((blk, W), lambda i: (i, 0)),
            pl.BlockSpec((blk, W), lambda i: (i, 0)),
            pl.BlockSpec((blk, W), lambda i: (i, 0)),
            pl.BlockSpec((1, MOBA_HEADS, 1, Dh, blk), lambda i: (i // nq, 0, i % nq, 0, 0)),
            pl.BlockSpec((1, 1, W), lambda i: (i, 0, 0)),
        ),
        compiler_params=_cparams("parallel"),
        name="moba_qkv_prep",
    )(M, M, M, cos, sin, gq.reshape(1, -1), gk.reshape(1, -1))


def _moba_prep_sample_kernel(q_ref, k_ref, cos_ref, sin_ref, gq_ref, gk_ref, qo_ref, ko_ref):
    Dh = MOBA_HEAD_DIM
    cos = cos_ref[...]
    sin = sin_ref[...]
    for h in range(MOBA_HEADS):
        sl = slice(h * Dh, (h + 1) * Dh)
        qo_ref[:, sl] = _norm_rope(q_ref[:, sl], gq_ref[...], cos, sin)
        ko_ref[:, sl] = _norm_rope(k_ref[:, sl], gk_ref[...], cos, sin)


def moba_prep_sample(M, cos, sin, gq, gk):
    m = M.shape[0]
    W = MOBA_WIDTH
    Dh = MOBA_HEAD_DIM
    return pl.pallas_call(
        _moba_prep_sample_kernel,
        out_shape=(jax.ShapeDtypeStruct((m, W), F32), jax.ShapeDtypeStruct((m, W), F32)),
        grid=(1,),
        in_specs=[
            pl.BlockSpec((m, W), lambda i: (0, 0)),
            pl.BlockSpec((m, W), lambda i: (0, 1)),
            pl.BlockSpec((m, Dh), lambda i: (0, 0)),
            pl.BlockSpec((m, Dh), lambda i: (0, 0)),
            pl.BlockSpec((1, Dh), lambda i: (0, 0)),
            pl.BlockSpec((1, Dh), lambda i: (0, 0)),
        ],
        out_specs=(pl.BlockSpec((m, W), lambda i: (0, 0)), pl.BlockSpec((m, W), lambda i: (0, 0))),
        compiler_params=_cparams("arbitrary"),
        name="moba_qk_prep_sample",
    )(M, M, cos, sin, gq.reshape(1, -1), gk.reshape(1, -1))


def _block_rank(bs, n_valid, axis):
    nb = bs.shape[axis]
    idx = lax.broadcasted_iota(jnp.int32, bs.shape, axis)
    rank = jnp.zeros(bs.shape, jnp.int32)
    for mm in range(nb):
        one = bs[mm:mm + 1, :] if axis == 0 else bs[:, mm:mm + 1]
        beats = (one > bs) | ((one == bs) & (idx > mm))
        if n_valid is not None:
            beats = beats & (mm < n_valid)
        rank = rank + jnp.where(beats, 1, 0)
    return rank


def _moba_attn_kernel(q_ref, kb_ref, vt_ref, km_ref, o_ref, sel_ref):
    i = pl.program_id(2)
    blk = MOBA_BLOCK
    scale = MOBA_HEAD_DIM ** -0.5
    qT = q_ref[...].T
    km = km_ref[0]
    nb = km.shape[0]
    grp = next(g for g in (4, 2, 1) if nb % g == 0)
    bsT = _dot(km, qT, HIGHEST)
    blk_id = lax.broadcasted_iota(jnp.int32, (nb, blk), 0)
    rank = _block_rank(bsT, i, 0)
    sel_ref[...] = jnp.where((blk_id < i) & (rank < MOBA_TOPK), 1.0, 0.0)
    qb = qT.astype(BF16)

    own = pl.ds(pl.multiple_of(i * blk, blk), blk)
    s = _dot(kb_ref[0, own, :], qb) * scale
    key_id = lax.broadcasted_iota(jnp.int32, (blk, blk), 0)
    qry_id = lax.broadcasted_iota(jnp.int32, (blk, blk), 1)
    s = jnp.where(key_id <= qry_id, s, NEG_INF)
    m0 = jnp.max(s, axis=0, keepdims=True)
    p = jnp.exp(s - m0)
    l0 = jnp.sum(p, axis=0, keepdims=True)
    acc0 = _dot(vt_ref[0, 0, i], p.astype(BF16))

    def scores(j):
        rows = pl.ds(pl.multiple_of(j * blk, blk), blk)
        picked = sel_ref[pl.ds(j, 1), :] > 0.0
        return jnp.where(picked, _dot(kb_ref[0, rows, :], qb) * scale, NEG_INF)

    def group(gi, carry):
        m, l, acc = carry
        js = [gi * grp + u for u in range(grp)]
        s = jnp.concatenate([scores(j) for j in js], axis=0)
        m_new = jnp.maximum(m, jnp.max(s, axis=0, keepdims=True))
        alpha = jnp.exp(m - m_new)
        p = jnp.exp(s - m_new)
        l = l * alpha + jnp.sum(p, axis=0, keepdims=True)
        pb = p.astype(BF16)
        acc = acc * alpha
        for u, j in enumerate(js):
            acc = acc + _dot(vt_ref[0, 0, j], pb[u * blk:(u + 1) * blk])
        return m_new, l, acc

    m, l, acc = lax.fori_loop(0, (i + grp - 1) // grp, group, (m0, l0, acc0))
    o_ref[...] = (acc / l).T.astype(o_ref.dtype)


def moba_attention_prompt(q_rot, kb, vt, kmean, *, batch, seq):
    W = MOBA_WIDTH
    blk = MOBA_BLOCK
    Dh = MOBA_HEAD_DIM
    nq = seq // blk
    nb = kmean.shape[1]
    return pl.pallas_call(
        _moba_attn_kernel,
        out_shape=jax.ShapeDtypeStruct((batch * seq, W), BF16),
        grid=(batch, MOBA_HEADS, nq),
        in_specs=[
            pl.BlockSpec((blk, Dh), lambda b, h, i: (b * nq + i, h)),
            pl.BlockSpec((1, seq, Dh), lambda b, h, i: (b, 0, h)),
            pl.BlockSpec((1, 1, nb, Dh, blk), lambda b, h, i: (b, h, 0, 0, 0)),
            pl.BlockSpec((1, nb, Dh), lambda b, h, i: (b, 0, h)),
        ],
        out_specs=pl.BlockSpec((blk, Dh), lambda b, h, i: (b * nq + i, h)),
        scratch_shapes=[pltpu.VMEM((nb, blk), F32)],
        compiler_params=_cparams("parallel", "parallel", "arbitrary"),
        name="moba_attention",
    )(q_rot, kb.reshape(batch, seq, W), vt, kmean)


PAGES_PER_STEP = 16
PAGES_PER_BLOCK = MOBA_BLOCK // PAGE_SIZE


def _kmean_pages_kernel(pt_ref, *refs):
    del pt_ref
    page_refs, o_ref = refs[:-1], refs[-1]
    for n in range(len(page_refs) // PAGES_PER_BLOCK):
        tot = page_refs[PAGES_PER_BLOCK * n][0, 0].sum(axis=0)
        for e in range(1, PAGES_PER_BLOCK):
            tot = tot + page_refs[PAGES_PER_BLOCK * n + e][0, 0].sum(axis=0)
        tot = tot * (1.0 / MOBA_BLOCK)
        for h in range(MOBA_HEADS):
            o_ref[0, 0, h, pl.ds(n, 1), :] = tot[h:h + 1, :]


def kmean_from_pages(cache_k, page_table):
    L = cache_k.shape[0]
    bs, n_pages = page_table.shape
    pps = min(PAGES_PER_STEP, n_pages)
    steps = n_pages // pps

    def page_spec(e):
        return pl.BlockSpec((1, 1, PAGE_SIZE, MOBA_HEADS, MOBA_HEAD_DIM),
                            lambda l, b, s, pt: (l, pt[b * n_pages + s * pps + e], 0, 0, 0))

    return pl.pallas_call(
        _kmean_pages_kernel,
        out_shape=jax.ShapeDtypeStruct((L, bs, MOBA_HEADS, n_pages // PAGES_PER_BLOCK, MOBA_HEAD_DIM), F32),
        grid_spec=pltpu.PrefetchScalarGridSpec(
            num_scalar_prefetch=1,
            grid=(L, bs, steps),
            in_specs=[page_spec(e) for e in range(pps)],
            out_specs=pl.BlockSpec((1, 1, MOBA_HEADS, pps // PAGES_PER_BLOCK, MOBA_HEAD_DIM),
                                   lambda l, b, s, pt: (l, b, 0, s, 0)),
        ),
        compiler_params=_cparams("parallel", "parallel", "arbitrary"),
        name="moba_paged_kmean",
    )(page_table.reshape(-1), *([cache_k] * pps))


def _moba_select_kernel(q_ref, km_ref, o_ref):
    Dh = MOBA_HEAD_DIM
    rows = []
    for h in range(MOBA_HEADS):
        rows.append(_dot_nt(q_ref[0, :, h * Dh:(h + 1) * Dh], km_ref[0, 0, h], HIGHEST))
    bs = jnp.concatenate(rows, axis=0)
    nb = bs.shape[1]
    rank = _block_rank(bs, None, 1)
    lane = lax.broadcasted_iota(jnp.int32, bs.shape, 1)
    olane = lax.broadcasted_iota(jnp.int32, (MOBA_HEADS, 128), 1)
    out = jnp.zeros((MOBA_HEADS, 128), jnp.int32)
    for s in range(min(MOBA_TOPK, nb)):
        idx = jnp.sum(jnp.where(rank == s, lane, 0), axis=-1, keepdims=True)
        out = jnp.where(olane == s, idx, out)
    o_ref[0] = out


def moba_select_sample(q3, kmean_past, *, layer):
    bs, _, W = q3.shape
    nb = kmean_past.shape[3]
    return pl.pallas_call(
        _moba_select_kernel,
        out_shape=jax.ShapeDtypeStruct((bs, MOBA_HEADS, 128), jnp.int32),
        grid=(bs,),
        in_specs=[
            pl.BlockSpec((1, 1, W), lambda b: (b, 0, 0)),
            pl.BlockSpec((1, 1, MOBA_HEADS, nb, MOBA_HEAD_DIM), lambda b: (layer, b, 0, 0, 0)),
        ],
        out_specs=pl.BlockSpec((1, MOBA_HEADS, 128), lambda b: (b, 0, 0)),
        compiler_params=_cparams("parallel"),
        name="moba_select_sample",
    )(q3, kmean_past)


def _moba_decode_kernel(sel_ref, pt_ref, q_ref, kn_ref, vn_ref, *refs, n_pg):
    del sel_ref, pt_ref
    kp_refs, vp_refs, o_ref = refs[:n_pg], refs[n_pg:2 * n_pg], refs[2 * n_pg]
    h = pl.program_id(1)
    scale = MOBA_HEAD_DIM ** -0.5
    rows = PAGE_SIZE * MOBA_HEADS
    q = q_ref[0]
    qb = q.astype(BF16)
    s_own = jnp.sum(q * kn_ref[0], axis=-1, keepdims=True) * scale
    mine = (lax.broadcasted_iota(jnp.int32, (1, rows), 1) % MOBA_HEADS) == h
    sc = [jnp.where(mine, _dot_nt(qb, kp[0, 0].reshape(rows, MOBA_HEAD_DIM).astype(BF16)) * scale, NEG_INF)
          for kp in kp_refs]
    m = s_own
    for x in sc:
        m = jnp.maximum(m, jnp.max(x, axis=-1, keepdims=True))
    p_own = jnp.exp(s_own - m)
    l = p_own
    acc = p_own * vn_ref[0]
    for x, vp in zip(sc, vp_refs):
        p = jnp.exp(x - m)
        l = l + jnp.sum(p, axis=-1, keepdims=True)
        acc = acc + _dot(p.astype(BF16), vp[0, 0].reshape(rows, MOBA_HEAD_DIM).astype(BF16))
    o_ref[0] = (acc / l).astype(o_ref.dtype)


def moba_decode(sel_flat, pt_flat, q3, k3, M3, cache_k, cache_v, *, layer, n_pages, n_sel):
    bs, _, W = q3.shape
    Dh = MOBA_HEAD_DIM
    n_pg = n_sel * PAGES_PER_BLOCK

    def page_spec(e):
        def page_map(b, h, sel, pt):
            blk = sel[(b * MOBA_HEADS + h) * n_sel + e // PAGES_PER_BLOCK]
            return (layer, pt[b * n_pages + blk * PAGES_PER_BLOCK + e % PAGES_PER_BLOCK], 0, 0, 0)
        return pl.BlockSpec((1, 1, PAGE_SIZE, MOBA_HEADS, Dh), page_map)

    vec = lambda col: pl.BlockSpec((1, 1, Dh), lambda b, h, sel, pt: (b, 0, col(h)))
    return pl.pallas_call(
        functools.partial(_moba_decode_kernel, n_pg=n_pg),
        out_shape=jax.ShapeDtypeStruct((bs, 1, W), F32),
        grid_spec=pltpu.PrefetchScalarGridSpec(
            num_scalar_prefetch=2,
            grid=(bs, MOBA_HEADS),
            in_specs=[vec(lambda h: h), vec(lambda h: h), vec(lambda h: 2 * MOBA_HEADS + h)]
            + [page_spec(e) for e in range(n_pg)] * 2,
            out_specs=vec(lambda h: h),
        ),
        compiler_params=_cparams("parallel", "arbitrary"),
        name="moba_decode",
    )(sel_flat, pt_flat, q3, k3, M3, *([cache_k] * n_pg), *([cache_v] * n_pg))


def _merge_kernel(ya_ref, yb_ref, yc_ref, wa_ref, wb_ref, wc_ref, z0_ref, z1_ref, z2_ref, o_ref):
    acc = _sigmoid(z0_ref[...]) * _dot(ya_ref[...].astype(BF16), wa_ref[...])
    acc = acc + _sigmoid(z1_ref[...]) * _dot(yb_ref[...].astype(BF16), wb_ref[...])
    acc = acc + _sigmoid(z2_ref[...]) * _dot(yc_ref[...].astype(BF16), wc_ref[...])
    o_ref[...] = acc.astype(o_ref.dtype)


def gated_merge(ya, yb, yc, wa, wb, wc, Z, *, tn=1024, tm_pref=512):
    m = ya.shape[0]
    d = wa.shape[1]
    tm = _row_tile(m, tm_pref)
    nj = d // tn
    return pl.pallas_call(
        _merge_kernel,
        out_shape=jax.ShapeDtypeStruct((m, d), BF16),
        grid=(m // tm, nj),
        in_specs=[
            pl.BlockSpec((tm, ya.shape[1]), lambda i, j: (i, 0)),
            pl.BlockSpec((tm, yb.shape[1]), lambda i, j: (i, 0)),
            pl.BlockSpec((tm, yc.shape[1]), lambda i, j: (i, 0)),
            pl.BlockSpec((wa.shape[0], tn), lambda i, j: (0, j)),
            pl.BlockSpec((wb.shape[0], tn), lambda i, j: (0, j)),
            pl.BlockSpec((wc.shape[0], tn), lambda i, j: (0, j)),
            pl.BlockSpec((tm, tn), lambda i, j: (i, j)),
            pl.BlockSpec((tm, tn), lambda i, j: (i, nj + j)),
            pl.BlockSpec((tm, tn), lambda i, j: (i, 2 * nj + j)),
        ],
        out_specs=pl.BlockSpec((tm, tn), lambda i, j: (i, j)),
        compiler_params=_cparams("parallel", "arbitrary"),
        name="gated_merge",
    )(ya, yb, yc, wa, wb, wc, Z, Z, Z)


def _matmul_residual_kernel(a_ref, w_ref, x_ref, o_ref):
    o_ref[...] = x_ref[...] + _dot(a_ref[...], w_ref[...])


def matmul_residual(a, w, x, *, tn, tm_pref=512):
    m, k = a.shape
    n = w.shape[1]
    tm = _row_tile(m, tm_pref)
    return pl.pallas_call(
        _matmul_residual_kernel,
        out_shape=jax.ShapeDtypeStruct((m, n), F32),
        grid=(m // tm, n // tn),
        in_specs=[
            pl.BlockSpec((tm, k), lambda i, j: (i, 0)),
            pl.BlockSpec((k, tn), lambda i, j: (0, j)),
            pl.BlockSpec((tm, tn), lambda i, j: (i, j)),
        ],
        out_specs=pl.BlockSpec((tm, tn), lambda i, j: (i, j)),
        compiler_params=_cparams("parallel", "arbitrary"),
        name="matmul_residual",
    )(a, w, x)


def _ffn_in_kernel(x_ref, g_ref, wg_ref, wv_ref, o_ref, h_ref):
    @pl.when(pl.program_id(1) == 0)
    def _():
        x = x_ref[...]
        ms = jnp.mean(x * x, axis=-1, keepdims=True)
        h_ref[...] = (x * lax.rsqrt(ms + NORM_EPS) * g_ref[...]).astype(BF16)

    h = h_ref[...]
    gate = _dot(h, wg_ref[...])
    val = _dot(h, wv_ref[...])
    o_ref[...] = (gate * _sigmoid(gate) * val).astype(o_ref.dtype)


def ffn_in(x, g, w, *, tn=512, tm_pref=512):
    m, d = x.shape
    hidden = w.shape[1] // 2
    tm = _row_tile(m, tm_pref)
    nj = hidden // tn
    return pl.pallas_call(
        _ffn_in_kernel,
        out_shape=jax.ShapeDtypeStruct((m, hidden), BF16),
        grid=(m // tm, nj),
        in_specs=[
            pl.BlockSpec((tm, d), lambda i, j: (i, 0)),
            pl.BlockSpec((1, d), lambda i, j: (0, 0)),
            pl.BlockSpec((d, tn), lambda i, j: (0, j)),
            pl.BlockSpec((d, tn), lambda i, j: (0, nj + j)),
        ],
        out_specs=pl.BlockSpec((tm, tn), lambda i, j: (i, j)),
        scratch_shapes=[pltpu.VMEM((tm, d), BF16)],
        compiler_params=_cparams("parallel", "arbitrary"),
        name="ffn_in_swiglu",
    )(x, g.reshape(1, d), w, w)


def _rope_tables(pos):
    half = MOBA_HEAD_DIM // 2
    inv = ROPE_THETA ** (-jnp.arange(half, dtype=F32) / half)
    ang = pos.astype(F32)[:, None] * inv[None, :]
    cos, sin = jnp.cos(ang), jnp.sin(ang)
    return jnp.concatenate([cos, cos], axis=-1), jnp.concatenate([-sin, sin], axis=-1)


def _layer_weights(lw):
    o1 = RWKV_PROJ
    o2 = o1 + 3 * MOBA_WIDTH
    o3 = o2 + 2 * GLA_K_WIDTH + GLA_V_WIDTH
    o4 = o3 + GLA_GATE_RANK
    o5 = o4 + GLA_V_WIDTH
    w_in = lw['w_in']
    d = w_in.shape[0]
    w_g = jnp.concatenate(
        [w_in[:, o2:o3], w_in[:, o4:o5], w_in[:, o3:o4],
         jnp.zeros((d, GLA_GATE_PAD - GLA_GATE_RANK), w_in.dtype)], axis=1)
    return {
        'w_p': w_in[:, :o1].astype(BF16),
        'w_m': w_in[:, o1:o2].astype(BF16),
        'w_g': w_g.astype(BF16),
        'w_z': w_in[:, o5:].astype(BF16),
        'w_up_rwkv': lw['w_up_rwkv'].astype(BF16),
        'w_up_moba': lw['w_up_moba'].astype(BF16),
        'w_up_gla': lw['w_up_gla'].astype(BF16),
        'w_out': lw['w_out'].astype(BF16),
        'w_ffn_in': lw['w_ffn_in'].astype(BF16),
        'w_ffn_out': lw['w_ffn_out'].astype(BF16),
    }


def _mix_and_ffn(x, ya, yb, yc, Z, lw, wb):
    merged = gated_merge(ya, yb, yc, wb['w_up_rwkv'], wb['w_up_moba'], wb['w_up_gla'], Z)
    x = matmul_residual(merged, wb['w_out'], x, tn=1024)
    act = ffn_in(x, lw['norm_ffn'], wb['w_ffn_in'])
    return matmul_residual(act, wb['w_ffn_out'], x, tn=512)


def _prompt_layer(x, lw, wb, rope, *, batch, seq):
    g = lw['norm_mix']
    P = norm_matmul(x, g, wb['w_p'], RWKV_PROJ)
    M = norm_matmul(x, g, wb['w_m'], MOBA_WIDTH)
    G = norm_matmul(x, g, wb['w_g'], GLA_SECT)
    Z = norm_matmul(x, g, wb['w_z'], 1024)

    tt = min(seq, 256)
    ya, r_S, shift = rwkv_branch(
        P, jnp.zeros((batch, RWKV_PROJ), F32),
        jnp.zeros((batch, RWKV_HEADS, RWKV_HEAD_DIM, RWKV_HEAD_DIM), F32), lw,
        batch=batch, t_pad=seq, t_real=seq, tt=tt, chunk=64)
    yc, g_S = gla_branch(
        G, jnp.zeros((batch, GLA_HEADS, GLA_KEY_DIM, GLA_VALUE_DIM), F32), lw,
        batch=batch, t_pad=seq, t_real=seq, tt=tt, chunk=64, sub=16)

    nq = seq // MOBA_BLOCK
    q_rot, k_rot, kb, vt, kmean = moba_prep_prompt(M, rope[0], rope[1], lw['moba_q_norm'], lw['moba_k_norm'],
                                                   batch=batch, seq=seq)
    yb = moba_attention_prompt(q_rot, kb, vt, kmean.reshape(batch, nq, MOBA_WIDTH), batch=batch, seq=seq)

    x = _mix_and_ffn(x, ya, yb, yc, Z, lw, wb)
    k_new = k_rot.reshape(batch, seq, MOBA_HEADS, MOBA_HEAD_DIM)
    v_new = M[:, 2 * MOBA_WIDTH:].reshape(batch, seq, MOBA_HEADS, MOBA_HEAD_DIM)
    return x, k_new, v_new, r_S, shift, g_S


def _sample_layer(x, lw, wb, rope, shift0, rwkv_S0, gla_S0, kmean_past, pt_flat, cache_k, cache_v,
                  *, layer, n_pages):
    bs = x.shape[0]
    g = lw['norm_mix']
    P = norm_matmul(x, g, wb['w_p'], RWKV_PROJ)
    M = norm_matmul(x, g, wb['w_m'], MOBA_WIDTH)
    G = norm_matmul(x, g, wb['w_g'], GLA_SECT)
    Z = norm_matmul(x, g, wb['w_z'], 1024)

    pad_rows = lambda a: jnp.pad(a[:, None, :], ((0, 0), (0, SUBLANES - 1), (0, 0))).reshape(bs * SUBLANES, -1)
    ya, r_S, shift = rwkv_branch(pad_rows(P), shift0, rwkv_S0, lw,
                                 batch=bs, t_pad=SUBLANES, t_real=1, tt=SUBLANES, chunk=SUBLANES)
    yc, g_S = gla_branch(pad_rows(G), gla_S0, lw,
                         batch=bs, t_pad=SUBLANES, t_real=1, tt=SUBLANES, chunk=SUBLANES, sub=SUBLANES)
    ya = ya[::SUBLANES]
    yc = yc[::SUBLANES]

    q_rot, k_rot = moba_prep_sample(M, rope[0], rope[1], lw['moba_q_norm'], lw['moba_k_norm'])
    q3 = q_rot.reshape(bs, 1, MOBA_WIDTH)
    n_sel = min(MOBA_TOPK, kmean_past.shape[3])
    sel = moba_select_sample(q3, kmean_past, layer=layer)[:, :, :n_sel].reshape(-1)
    yb = moba_decode(sel, pt_flat, q3, k_rot.reshape(bs, 1, MOBA_WIDTH), M.reshape(bs, 1, 3 * MOBA_WIDTH),
                     cache_k, cache_v, layer=layer, n_pages=n_pages, n_sel=n_sel).reshape(bs, MOBA_WIDTH)

    x = _mix_and_ffn(x, ya, yb, yc, Z, lw, wb)
    k_new = k_rot.reshape(bs, 1, MOBA_HEADS, MOBA_HEAD_DIM)
    v_new = M[:, 2 * MOBA_WIDTH:].reshape(bs, 1, MOBA_HEADS, MOBA_HEAD_DIM)
    return x, k_new, v_new, r_S, shift, g_S


_LAYER_KEYS = ('norm_mix', 'w_in', 'rwkv_mu', 'rwkv_w0', 'rwkv_w_up', 'rwkv_a0', 'rwkv_a_up', 'rwkv_g_up',
               'rwkv_k_k', 'rwkv_k_a', 'rwkv_r_k', 'rwkv_ln_w', 'rwkv_ln_b', 'moba_q_norm', 'moba_k_norm',
               'gla_a_up', 'gla_a_bias', 'gla_o_norm', 'w_up_rwkv', 'w_up_moba', 'w_up_gla', 'w_out',
               'norm_ffn', 'w_ffn_in', 'w_ffn_out')


def kernel(x_prompt, x_sample, cache_k, cache_v, page_table, state_rwkv, state_rwkv_shift, state_gla, norm_mix, w_in, rwkv_mu, rwkv_w0, rwkv_w_up, rwkv_a0, rwkv_a_up, rwkv_g_up, rwkv_k_k, rwkv_k_a, rwkv_r_k, rwkv_ln_w, rwkv_ln_b, moba_q_norm, moba_k_norm, gla_a_up, gla_a_bias, gla_o_norm, w_up_rwkv, w_up_moba, w_up_gla, w_out, norm_ffn, w_ffn_in, w_ffn_out):
    stacked = dict(zip(_LAYER_KEYS, (
        norm_mix, w_in, rwkv_mu, rwkv_w0, rwkv_w_up, rwkv_a0, rwkv_a_up, rwkv_g_up, rwkv_k_k, rwkv_k_a,
        rwkv_r_k, rwkv_ln_w, rwkv_ln_b, moba_q_norm, moba_k_norm, gla_a_up, gla_a_bias, gla_o_norm,
        w_up_rwkv, w_up_moba, w_up_gla, w_out, norm_ffn, w_ffn_in, w_ffn_out)))
    depth = w_in.shape[0]
    bp, seq, d = x_prompt.shape
    bs, dec_seq, _ = x_sample.shape
    n_pages = page_table.shape[1]
    past_len = n_pages * cache_k.shape[2]
    assert dec_seq == 1 and cache_k.shape[2:] == (PAGE_SIZE, MOBA_HEADS, MOBA_HEAD_DIM)
    assert seq % MOBA_BLOCK == 0 and past_len % MOBA_BLOCK == 0 and past_len // MOBA_BLOCK >= MOBA_TOPK

    pt_flat = page_table.reshape(-1)
    kmean_past = kmean_from_pages(cache_k, page_table)

    rope_p = _rope_tables(jnp.arange(seq))
    rope_s = _rope_tables(jnp.full((bs,), past_len))

    yp = x_prompt.reshape(bp * seq, d)
    ys = x_sample.reshape(bs, d)
    outs_p, outs_s = [], []
    for l in range(depth):
        lw = {k: v[l] for k, v in stacked.items()}
        wb = _layer_weights(lw)
        yp, *rest_p = _prompt_layer(yp, lw, wb, rope_p, batch=bp, seq=seq)
        outs_p.append(rest_p)
        ys, *rest_s = _sample_layer(ys, lw, wb, rope_s, state_rwkv_shift[l], state_rwkv[l], state_gla[l],
                                    kmean_past, pt_flat, cache_k, cache_v, layer=l, n_pages=n_pages)
        outs_s.append(rest_s)
    stack = lambda outs, i: jnp.stack([o[i] for o in outs])
    return (yp.reshape(bp, seq, d), ys.reshape(bs, 1, d),
            stack(outs_p, 0), stack(outs_p, 1), stack(outs_p, 2), stack(outs_p, 3), stack(outs_p, 4),
            stack(outs_s, 0), stack(outs_s, 1), stack(outs_s, 2), stack(outs_s, 3), stack(outs_s, 4))
```

```python
import functools

import jax
import jax.numpy as jnp
from jax import lax
from jax.experimental import pallas as pl
from jax.experimental.pallas import tpu as pltpu

F32 = jnp.float32
BF16 = jnp.bfloat16
HIGHEST = lax.Precision.HIGHEST

PAGE_SIZE = 128
RWKV_HEADS = 8
RWKV_HEAD_DIM = 64
RWKV_WIDTH = RWKV_HEADS * RWKV_HEAD_DIM
RWKV_DECAY_RANK = 64
RWKV_ICLR_RANK = 64
RWKV_GATE_RANK = 128
RWKV_PROJ = 3 * RWKV_WIDTH + RWKV_DECAY_RANK + RWKV_ICLR_RANK + RWKV_GATE_RANK
RWKV_GN_EPS = 64e-5
MOBA_HEADS = 8
MOBA_HEAD_DIM = 128
MOBA_WIDTH = MOBA_HEADS * MOBA_HEAD_DIM
MOBA_BLOCK = 256
MOBA_TOPK = 3
ROPE_THETA = 10000.0
GLA_HEADS = 4
GLA_KEY_DIM = 64
GLA_VALUE_DIM = 128
GLA_K_WIDTH = GLA_HEADS * GLA_KEY_DIM
GLA_V_WIDTH = GLA_HEADS * GLA_VALUE_DIM
GLA_GATE_RANK = 16
GLA_GATE_PAD = 256
GLA_SECT = 2 * GLA_K_WIDTH + 2 * GLA_V_WIDTH + GLA_GATE_PAD
GLA_TAU = 16.0
PROJ_SLOT = 2048
PROJ_TILE = 1024
COL_RWKV = 0
COL_GLA = PROJ_SLOT
COL_MOBA = 2 * PROJ_SLOT
COL_GATES = COL_MOBA + 3 * MOBA_WIDTH
N_BRANCHES = 3
NORM_EPS = 1e-6
NEG_INF = -1e30

VMEM_LIMIT_BYTES = 56 * 1024 * 1024
SUBLANES = 8

GATE_PASSES = 3
STATE_PASSES = 3
RWKV_CHUNK_PASSES = 1
GLA_CHUNK_PASSES = 1


def _cparams(*sem):
    return pltpu.CompilerParams(dimension_semantics=sem, vmem_limit_bytes=VMEM_LIMIT_BYTES)


def _dot(a, b, precision=None):
    return jnp.dot(a, b, preferred_element_type=F32, precision=precision)


def _dot_nt(a, b, precision=None):
    return lax.dot_general(a, b, (((1,), (1,)), ((), ())), preferred_element_type=F32, precision=precision)


def _dot_tn(a, b, precision=None):
    return lax.dot_general(a, b, (((0,), (0,)), ((), ())), preferred_element_type=F32, precision=precision)


def _mm(a, b, kind, passes):
    f = {'nn': _dot, 'nt': _dot_nt, 'tn': _dot_tn}[kind]
    if passes == 6:
        return f(a, b, HIGHEST)
    ah = a.astype(BF16)
    bh = b.astype(BF16)
    if passes == 1:
        return f(ah, bh)
    al = (a - ah.astype(F32)).astype(BF16)
    bl = (b - bh.astype(F32)).astype(BF16)
    return f(ah, bh) + f(ah, bl) + f(al, bh)


def _split3(x):
    hi = x.astype(BF16)
    r1 = x - hi.astype(F32)
    mid = r1.astype(BF16)
    lo = (r1 - mid.astype(F32)).astype(BF16)
    return hi, mid, lo


def _tril_ones(n, strict=False):
    r = lax.broadcasted_iota(jnp.int32, (n, n), 0)
    c = lax.broadcasted_iota(jnp.int32, (n, n), 1)
    return (r > c) if strict else (r >= c)


def _cumsum_rows(x, tri_bf16):
    hi, mid, lo = _split3(x)
    return _dot(tri_bf16, hi) + _dot(tri_bf16, mid) + _dot(tri_bf16, lo)


def _sigmoid(x):
    return 1.0 / (1.0 + jnp.exp(-x))


def _softplus(x):
    return jnp.maximum(x, 0.0) + jnp.log(1.0 + jnp.exp(-jnp.abs(x)))


def _row_tile(m, pref):
    return pref if m % pref == 0 else m


def _act_dtype(tile_rows):
    return BF16 if tile_rows % (2 * SUBLANES) == 0 else F32


def _norm_matmul_kernel(x_ref, g_ref, w_ref, o_ref, h_ref):
    @pl.when(pl.program_id(1) == 0)
    def _():
        x = x_ref[...]
        ms = jnp.mean(x * x, axis=-1, keepdims=True)
        h_ref[...] = (x * lax.rsqrt(ms + NORM_EPS) * g_ref[...]).astype(BF16)

    o_ref[...] = _dot(h_ref[...], w_ref[...]).astype(o_ref.dtype)


def norm_matmul(x, g, w, layer, *, tn=1024, tm_pref=1024):
    m, d = x.shape
    n = w.shape[2]
    tm = _row_tile(m, tm_pref)
    return pl.pallas_call(
        _norm_matmul_kernel,
        out_shape=jax.ShapeDtypeStruct((m, n), F32),
        grid=(m // tm, n // tn),
        in_specs=[
            pl.BlockSpec((tm, d), lambda i, j: (i, 0)),
            pl.BlockSpec((1, d), lambda i, j: (0, 0)),
            pl.BlockSpec((None, d, tn), lambda i, j: (layer, 0, j)),
        ],
        out_specs=pl.BlockSpec((tm, tn), lambda i, j: (i, j)),
        scratch_shapes=[pltpu.VMEM((tm, d), BF16)],
        compiler_params=_cparams("parallel", "arbitrary"),
        name="norm_matmul",
    )(x, g.reshape(1, d), w)


def _rwkv_kernel(p_ref, shift0_ref, s0_ref, mu_ref, w0_ref, wup_ref, a0_ref, aup_ref, gup_ref,
                 kk_ref, ka_ref, rk_ref, lnw_ref, lnb_ref,
                 y_ref, sout_ref, shout_ref,
                 carry_ref, s_ref, r_s, k_s, v_s, kk_s, b_s, lw_s, y_s,
                 *, t_real, tt, chunk):
    t = pl.program_id(1)
    nt = pl.num_programs(1)
    W = RWKV_WIDTH
    N = RWKV_HEAD_DIM
    H = RWKV_HEADS
    C = chunk
    cp = RWKV_CHUNK_PASSES

    @pl.when(t == 0)
    def _():
        carry_ref[...] = shift0_ref[0]
        s_ref[...] = s0_ref[0]

    P = p_ref[...]
    rowid = lax.broadcasted_iota(jnp.int32, (tt, 1), 0)
    prev = jnp.where(rowid == 0, carry_ref[...], pltpu.roll(P, 1, axis=0))
    last_row = (t_real - 1) % tt
    carry_ref[...] = P[last_row:last_row + 1, :]

    @pl.when(t == nt - 1)
    def _():
        shout_ref[0] = P[last_row:last_row + 1, :]

    Pm = P + (prev - P) * mu_ref[...]
    r = Pm[:, 0:W]
    k = Pm[:, W:2 * W]
    v = Pm[:, 2 * W:3 * W]
    o = 3 * W
    wd = Pm[:, o:o + RWKV_DECAY_RANK]
    ad = Pm[:, o + RWKV_DECAY_RANK:o + RWKV_DECAY_RANK + RWKV_ICLR_RANK]
    gd = Pm[:, o + RWKV_DECAY_RANK + RWKV_ICLR_RANK:]
    w_val = -_softplus(-(w0_ref[...] + _mm(jnp.tanh(wd), wup_ref[...], 'nn', GATE_PASSES))) - 0.5
    lw = -jnp.exp(w_val)
    a = _sigmoid(a0_ref[...] + _mm(ad, aup_ref[...], 'nn', GATE_PASSES))
    g = _mm(_sigmoid(gd), gup_ref[...], 'nn', GATE_PASSES)
    kk = k * kk_ref[...]
    kmod = k * (1.0 + (a - 1.0) * ka_ref[...])
    if t_real % tt != 0:
        valid = (t * tt + rowid) < t_real
        lw = jnp.where(valid, lw, 0.0)
        kk = jnp.where(valid, kk, 0.0)
        kmod = jnp.where(valid, kmod, 0.0)
        v = jnp.where(valid, v, 0.0)
    seg_r = lax.broadcasted_iota(jnp.int32, (W, W), 0) // N
    seg_c = lax.broadcasted_iota(jnp.int32, (W, W), 1) // N
    seg = jnp.where(seg_r == seg_c, 1.0, 0.0).astype(BF16)
    hi, md, lo = _split3(kk * kk)
    ss = _dot(hi, seg) + _dot(md, seg) + _dot(lo, seg)
    kkn = kk / jnp.maximum(jnp.sqrt(ss), 1e-12)
    r_s[...] = r
    k_s[...] = kmod
    v_s[...] = v
    kk_s[...] = kkn
    b_s[...] = kkn * a
    lw_s[...] = lw

    tri = jnp.where(_tril_ones(C), 1.0, 0.0).astype(BF16)
    low_incl = _tril_ones(C)
    low_strict = _tril_ones(C, strict=True)
    eye = lax.broadcasted_iota(jnp.int32, (N, N), 0) == lax.broadcasted_iota(jnp.int32, (N, N), 1)
    mid = max(C // 2 - 1, 0)
    heads = range(H)
    hs = [slice(h * N, (h + 1) * N) for h in heads]

    def chunk_body(c, carry):
        rows = pl.ds(pl.multiple_of(c * C, C), C)
        lw_c = lw_s[rows, :]
        cum = _cumsum_rows(lw_c, tri)
        cume = cum - lw_c
        rho = cum[mid:mid + 1, :]
        cum_last = cum[C - 1:C, :]
        r_c = r_s[rows, :]
        k_c = k_s[rows, :]
        v_c = v_s[rows, :]
        kk_c = kk_s[rows, :]
        b_c = b_s[rows, :]
        e_out = jnp.exp(rho - cum)
        e_end = jnp.exp(cum_last - cum)
        kk_hat = kk_c * jnp.exp(cume - rho)
        r_hat = r_c * jnp.exp(cum - rho)
        k_til = k_c * e_out
        b_til = b_c * e_out
        kk_abs = kk_c * jnp.exp(cume)
        r_abs = r_c * jnp.exp(cum)
        k_end = k_c * e_end
        b_end = b_c * e_end
        gam = jnp.exp(cum_last)

        A = [_mm(jnp.concatenate([kk_hat[:, s], r_hat[:, s]], axis=0),
                 jnp.concatenate([k_til[:, s], b_til[:, s]], axis=0), 'nt', cp) for s in hs]
        Lk = [jnp.where(low_strict, x[0:C, 0:C], 0.0) for x in A]
        Lb = [jnp.where(low_strict, x[0:C, C:2 * C], 0.0) for x in A]
        Ark = [jnp.where(low_incl, x[C:2 * C, 0:C], 0.0) for x in A]
        Arb = [jnp.where(low_incl, x[C:2 * C, C:2 * C], 0.0) for x in A]
        X = [jnp.concatenate([kk_abs[:, s], _mm(Lk[h], v_c[:, s], 'nn', cp)], axis=1) for h, s in zip(heads, hs)]
        X = [X[h] - _mm(Lb[h], X[h], 'nn', cp) for h in heads]
        Lp = Lb
        p = 2
        while p < C:
            Lp = [_mm(x, x, 'nn', cp) for x in Lp]
            X = [X[h] + _mm(Lp[h], X[h], 'nn', cp) for h in heads]
            p *= 2
        RY = [jnp.concatenate([r_abs[:, s], _mm(Ark[h], v_c[:, s], 'nn', cp)], axis=1)
              - _mm(Arb[h], X[h], 'nn', cp) for h, s in zip(heads, hs)]
        MN = [_mm(X[h], b_end[:, s], 'tn', cp) for h, s in zip(heads, hs)]
        VK = [_mm(v_c[:, s], k_end[:, s], 'tn', cp) for s in hs]
        for h, s in zip(heads, hs):
            S = s_ref[h]
            Y = _mm(RY[h][:, 0:N], S, 'nt', STATE_PASSES) + RY[h][:, N:2 * N]
            trans = jnp.where(eye, jnp.broadcast_to(gam[:, s], (N, N)), 0.0) - MN[h][0:N]
            s_ref[h] = _mm(S, trans, 'nn', STATE_PASSES) + (VK[h] - MN[h][N:2 * N])
            mu = jnp.mean(Y, axis=-1, keepdims=True)
            var = jnp.mean(jnp.square(Y - mu), axis=-1, keepdims=True)
            yn = (Y - mu) * lax.rsqrt(var + RWKV_GN_EPS) * lnw_ref[:, s] + lnb_ref[:, s]
            bonus = jnp.sum(r_c[:, s] * k_c[:, s] * rk_ref[:, s], axis=-1, keepdims=True) * v_c[:, s]
            y_s[rows, s] = yn + bonus
        return carry

    lax.fori_loop(0, tt // C, chunk_body, 0)
    y_ref[...] = (y_s[...] * g).astype(y_ref.dtype)

    @pl.when(t == nt - 1)
    def _():
        sout_ref[0] = s_ref[...]


def rwkv_branch(P, shift0, S0, lw, *, batch, t_pad, t_real, tt, chunk):
    nt = t_pad // tt
    W = RWKV_WIDTH
    vec = lambda a: a.reshape(1, -1)
    full = lambda shape: pl.BlockSpec(shape, lambda b, t: (0,) * len(shape))
    kern = functools.partial(_rwkv_kernel, t_real=t_real, tt=tt, chunk=chunk)
    y, s_out, sh_out = pl.pallas_call(
        kern,
        out_shape=(
            jax.ShapeDtypeStruct((batch * t_pad, W), _act_dtype(tt)),
            jax.ShapeDtypeStruct((batch, RWKV_HEADS, RWKV_HEAD_DIM, RWKV_HEAD_DIM), F32),
            jax.ShapeDtypeStruct((batch, 1, RWKV_PROJ), F32),
        ),
        grid=(batch, nt),
        in_specs=[
            pl.BlockSpec((tt, RWKV_PROJ), lambda b, t: (b * nt + t, 0)),
            pl.BlockSpec((1, 1, RWKV_PROJ), lambda b, t: (b, 0, 0)),
            pl.BlockSpec((1, RWKV_HEADS, RWKV_HEAD_DIM, RWKV_HEAD_DIM), lambda b, t: (b, 0, 0, 0)),
            full((1, RWKV_PROJ)),
            full((1, W)),
            full((RWKV_DECAY_RANK, W)),
            full((1, W)),
            full((RWKV_ICLR_RANK, W)),
            full((RWKV_GATE_RANK, W)),
            full((1, W)), full((1, W)), full((1, W)), full((1, W)), full((1, W)),
        ],
        out_specs=(
            pl.BlockSpec((tt, W), lambda b, t: (b * nt + t, 0)),
            pl.BlockSpec((1, RWKV_HEADS, RWKV_HEAD_DIM, RWKV_HEAD_DIM), lambda b, t: (b, 0, 0, 0)),
            pl.BlockSpec((1, 1, RWKV_PROJ), lambda b, t: (b, 0, 0)),
        ),
        scratch_shapes=[
            pltpu.VMEM((1, RWKV_PROJ), F32),
            pltpu.VMEM((RWKV_HEADS, RWKV_HEAD_DIM, RWKV_HEAD_DIM), F32),
        ] + [pltpu.VMEM((tt, W), F32) for _ in range(7)],
        compiler_params=_cparams("parallel", "arbitrary"),
        name="rwkv7_branch",
    )(P, shift0.reshape(batch, 1, RWKV_PROJ), S0,
      vec(lw['rwkv_mu']), vec(lw['rwkv_w0']), lw['rwkv_w_up'], vec(lw['rwkv_a0']), lw['rwkv_a_up'],
      lw['rwkv_g_up'], vec(lw['rwkv_k_k']), vec(lw['rwkv_k_a']), vec(lw['rwkv_r_k']),
      vec(lw['rwkv_ln_w']), vec(lw['rwkv_ln_b']))
    return y, s_out, sh_out.reshape(batch, RWKV_PROJ)


def _gla_kernel(g_ref, s0_ref, aup_ref, abias_ref, onorm_ref,
                y_ref, sout_ref,
                s_ref, la_s,
                *, t_real, tt, chunk, sub):
    t = pl.program_id(1)
    nt = pl.num_programs(1)
    KW, VW = GLA_K_WIDTH, GLA_V_WIDTH
    dk, dv = GLA_KEY_DIM, GLA_VALUE_DIM
    C = chunk
    cp = GLA_CHUNK_PASSES

    @pl.when(t == 0)
    def _():
        s_ref[...] = s0_ref[0]

    gl = g_ref[:, 2 * KW + 2 * VW:GLA_SECT]
    x = _mm(gl, aup_ref[...], 'nn', GATE_PASSES) + abias_ref[...]
    la = -_softplus(-x) * (1.0 / GLA_TAU)
    padded = t_real % tt != 0
    if padded:
        rowid = lax.broadcasted_iota(jnp.int32, (tt, 1), 0)
        valid = (t * tt + rowid) < t_real
        la = jnp.where(valid, la, 0.0)
    la_s[...] = la

    tri = jnp.where(_tril_ones(C), 1.0, 0.0).astype(BF16)
    ones_cv = jnp.ones((C, dv), BF16)
    low_incl = _tril_ones(C)
    crow = lax.broadcasted_iota(jnp.int32, (C, 1), 0)
    nsub = C // sub
    heads = range(GLA_HEADS)

    def chunk_body(c, carry):
        rows = pl.ds(pl.multiple_of(c * C, C), C)
        la_c = la_s[rows, :]
        cum_all = _cumsum_rows(la_c, tri)
        if padded:
            vmask = (t * tt + c * C + crow) < t_real
        q, k, v, cums = [], [], [], []
        for h in heads:
            q.append(g_ref[rows, h * dk:(h + 1) * dk] * (dk ** -0.5))
            k_h = g_ref[rows, KW + h * dk:KW + (h + 1) * dk]
            v_h = g_ref[rows, 2 * KW + h * dv:2 * KW + (h + 1) * dv]
            if padded:
                k_h = jnp.where(vmask, k_h, 0.0)
                v_h = jnp.where(vmask, v_h, 0.0)
            k.append(k_h)
            v.append(v_h)
            cums.append(cum_all[:, h * dk:(h + 1) * dk])
        bounds = [[jnp.zeros((1, dk), F32)] + [cm[i * sub - 1:i * sub, :] for i in range(1, nsub)] for cm in cums]
        qt = [q[h] * jnp.exp(cums[h] - jnp.concatenate(
            [jnp.broadcast_to(b, (sub, dk)) for b in bounds[h]], axis=0)) for h in heads]
        att = []
        for h in heads:
            att_rows = []
            for i in range(nsub):
                kt = k[h] * jnp.exp(jnp.where(crow < (i + 1) * sub, bounds[h][i] - cums[h], NEG_INF))
                att_rows.append(_mm(qt[h][i * sub:(i + 1) * sub], kt, 'nt', cp))
            att.append(jnp.where(low_incl, jnp.concatenate(att_rows, axis=0), 0.0))
        intra = [_mm(att[h], v[h], 'nn', cp) for h in heads]
        kv = [_mm(k[h] * jnp.exp(cums[h][C - 1:C, :] - cums[h]), v[h], 'tn', STATE_PASSES) for h in heads]
        tot = []
        for h in heads:
            hi, md, lo = _split3(la_c[:, h * dk:(h + 1) * dk])
            tot.append(_dot_tn(hi, ones_cv) + _dot_tn(md, ones_cv) + _dot_tn(lo, ones_cv))
        for h in heads:
            S = s_ref[h]
            o = intra[h] + _mm(q[h] * jnp.exp(cums[h]), S, 'nn', STATE_PASSES)
            s_ref[h] = S * jnp.exp(tot[h]) + kv[h]
            r_h = g_ref[rows, 2 * KW + VW + h * dv:2 * KW + VW + (h + 1) * dv]
            ms = jnp.mean(o * o, axis=-1, keepdims=True)
            on = o * lax.rsqrt(ms + NORM_EPS) * onorm_ref[...]
            y_ref[rows, h * dv:(h + 1) * dv] = (on * (r_h * _sigmoid(r_h))).astype(y_ref.dtype)
        return carry

    lax.fori_loop(0, tt // C, chunk_body, 0)

    @pl.when(t == nt - 1)
    def _():
        sout_ref[0] = s_ref[...]


def gla_branch(G, S0, lw, *, batch, t_pad, t_real, tt, chunk, sub, col_block=0):
    nt = t_pad // tt
    full = lambda shape: pl.BlockSpec(shape, lambda b, t: (0,) * len(shape))
    aup = jnp.zeros((GLA_GATE_PAD, GLA_K_WIDTH), F32).at[:GLA_GATE_RANK].set(lw['gla_a_up'])
    kern = functools.partial(_gla_kernel, t_real=t_real, tt=tt, chunk=chunk, sub=sub)
    y, s_out = pl.pallas_call(
        kern,
        out_shape=(
            jax.ShapeDtypeStruct((batch * t_pad, GLA_V_WIDTH), _act_dtype(tt)),
            jax.ShapeDtypeStruct((batch, GLA_HEADS, GLA_KEY_DIM, GLA_VALUE_DIM), F32),
        ),
        grid=(batch, nt),
        in_specs=[
            pl.BlockSpec((tt, PROJ_SLOT), lambda b, t: (b * nt + t, col_block)),
            pl.BlockSpec((1, GLA_HEADS, GLA_KEY_DIM, GLA_VALUE_DIM), lambda b, t: (b, 0, 0, 0)),
            full((GLA_GATE_PAD, GLA_K_WIDTH)),
            full((1, GLA_K_WIDTH)),
            full((1, GLA_VALUE_DIM)),
        ],
        out_specs=(
            pl.BlockSpec((tt, GLA_V_WIDTH), lambda b, t: (b * nt + t, 0)),
            pl.BlockSpec((1, GLA_HEADS, GLA_KEY_DIM, GLA_VALUE_DIM), lambda b, t: (b, 0, 0, 0)),
        ),
        scratch_shapes=[
            pltpu.VMEM((GLA_HEADS, GLA_KEY_DIM, GLA_VALUE_DIM), F32),
            pltpu.VMEM((tt, GLA_K_WIDTH), F32),
        ],
        compiler_params=_cparams("parallel", "arbitrary"),
        name="gla_branch",
    )(G, S0, aup, lw['gla_a_bias'].reshape(1, -1), lw['gla_o_norm'].reshape(1, -1))
    return y, s_out


def _norm_rope(x, g, cos, sin):
    ms = jnp.mean(x * x, axis=-1, keepdims=True)
    y = x * lax.rsqrt(ms + NORM_EPS) * g
    return y * cos + pltpu.roll(y, MOBA_HEAD_DIM // 2, axis=1) * sin


def _moba_prep_prompt_kernel(q_ref, k_ref, v_ref, cos_ref, sin_ref, gq_ref, gk_ref,
                             qo_ref, ko_ref, kb_ref, vt_ref, km_ref):
    Dh = MOBA_HEAD_DIM
    cos = cos_ref[...]
    sin = sin_ref[...]
    rows = q_ref.shape[0]
    for h in range(MOBA_HEADS):
        sl = slice(h * Dh, (h + 1) * Dh)
        qo_ref[:, sl] = _norm_rope(q_ref[:, sl], gq_ref[...], cos, sin)
        kr = _norm_rope(k_ref[:, sl], gk_ref[...], cos, sin)
        ko_ref[:, sl] = kr
        kb_ref[:, sl] = kr.astype(BF16)
        km_ref[0, :, sl] = jnp.sum(kr, axis=0, keepdims=True) * (1.0 / rows)
        vt_ref[0, h, 0] = v_ref[:, sl].T.astype(BF16)


def moba_prep_prompt(M, cos, sin, gq, gk, *, batch, seq, col0=0):
    m = M.shape[0]
    W = MOBA_WIDTH
    blk = MOBA_BLOCK
    Dh = MOBA_HEAD_DIM
    nq = seq // blk
    nblk = m // blk
    return pl.pallas_call(
        _moba_prep_prompt_kernel,
        out_shape=(
            jax.ShapeDtypeStruct((m, W), F32),
            jax.ShapeDtypeStruct((m, W), F32),
            jax.ShapeDtypeStruct((m, W), BF16),
            jax.ShapeDtypeStruct((batch, MOBA_HEADS, nq, Dh, blk), BF16),
            jax.ShapeDtypeStruct((nblk, 1, W), F32),
        ),
        grid=(nblk,),
        in_specs=[
            pl.BlockSpec((blk, W), lambda i: (i, col0)),
            pl.BlockSpec((blk, W), lambda i: (i, col0 + 1)),
            pl.BlockSpec((blk, W), lambda i: (i, col0 + 2)),
            pl.BlockSpec((blk, Dh), lambda i: (i % nq, 0)),
            pl.BlockSpec((blk, Dh), lambda i: (i % nq, 0)),
            pl.BlockSpec((1, Dh), lambda i: (0, 0)),
            pl.BlockSpec((1, Dh), lambda i: (0, 0)),
        ],
        out_specs=(
            pl.BlockSpec((blk, W), lambda i: (i, 0)),
            pl.BlockSpec((blk, W), lambda i: (i, 0)),
            pl.BlockSpec((blk, W), lambda i: (i, 0)),
            pl.BlockSpec((1, MOBA_HEADS, 1, Dh, blk), lambda i: (i // nq, 0, i % nq, 0, 0)),
            pl.BlockSpec((1, 1, W), lambda i: (i, 0, 0)),
        ),
        compiler_params=_cparams("parallel"),
        name="moba_qkv_prep",
    )(M, M, M, cos, sin, gq.reshape(1, -1), gk.reshape(1, -1))


def _moba_prep_sample_kernel(q_ref, k_ref, cos_ref, sin_ref, gq_ref, gk_ref, qo_ref, ko_ref):
    Dh = MOBA_HEAD_DIM
    cos = cos_ref[...]
    sin = sin_ref[...]
    for h in range(MOBA_HEADS):
        sl = slice(h * Dh, (h + 1) * Dh)
        qo_ref[:, sl] = _norm_rope(q_ref[:, sl], gq_ref[...], cos, sin)
        ko_ref[:, sl] = _norm_rope(k_ref[:, sl], gk_ref[...], cos, sin)


def moba_prep_sample(M, cos, sin, gq, gk, *, col0=0):
    m = M.shape[0]
    W = MOBA_WIDTH
    Dh = MOBA_HEAD_DIM
    return pl.pallas_call(
        _moba_prep_sample_kernel,
        out_shape=(jax.ShapeDtypeStruct((m, W), F32), jax.ShapeDtypeStruct((m, W), F32)),
        grid=(1,),
        in_specs=[
            pl.BlockSpec((m, W), lambda i: (0, col0)),
            pl.BlockSpec((m, W), lambda i: (0, col0 + 1)),
            pl.BlockSpec((m, Dh), lambda i: (0, 0)),
            pl.BlockSpec((m, Dh), lambda i: (0, 0)),
            pl.BlockSpec((1, Dh), lambda i: (0, 0)),
            pl.BlockSpec((1, Dh), lambda i: (0, 0)),
        ],
        out_specs=(pl.BlockSpec((m, W), lambda i: (0, 0)), pl.BlockSpec((m, W), lambda i: (0, 0))),
        compiler_params=_cparams("arbitrary"),
        name="moba_qk_prep_sample",
    )(M, M, cos, sin, gq.reshape(1, -1), gk.reshape(1, -1))


def _block_rank(bs, n_valid, axis):
    nb = bs.shape[axis]
    idx = lax.broadcasted_iota(jnp.int32, bs.shape, axis)
    rank = jnp.zeros(bs.shape, jnp.int32)
    for mm in range(nb):
        one = bs[mm:mm + 1, :] if axis == 0 else bs[:, mm:mm + 1]
        beats = (one > bs) | ((one == bs) & (idx > mm))
        if n_valid is not None:
            beats = beats & (mm < n_valid)
        rank = rank + jnp.where(beats, 1, 0)
    return rank


def _moba_attn_kernel(q_ref, kb_ref, vt_ref, km_ref, o_ref, sel_ref):
    i = pl.program_id(2)
    blk = MOBA_BLOCK
    scale = MOBA_HEAD_DIM ** -0.5
    qT = q_ref[...].T
    km = km_ref[0]
    nb = km.shape[0]
    grp = next(g for g in (4, 2, 1) if nb % g == 0)
    bsT = _dot(km, qT, HIGHEST)
    blk_id = lax.broadcasted_iota(jnp.int32, (nb, blk), 0)
    rank = _block_rank(bsT, i, 0)
    sel_ref[...] = jnp.where((blk_id < i) & (rank < MOBA_TOPK), 1.0, 0.0)
    qb = qT.astype(BF16)

    own = pl.ds(pl.multiple_of(i * blk, blk), blk)
    s = _dot(kb_ref[0, own, :], qb) * scale
    key_id = lax.broadcasted_iota(jnp.int32, (blk, blk), 0)
    qry_id = lax.broadcasted_iota(jnp.int32, (blk, blk), 1)
    s = jnp.where(key_id <= qry_id, s, NEG_INF)
    m0 = jnp.max(s, axis=0, keepdims=True)
    p = jnp.exp(s - m0)
    l0 = jnp.sum(p, axis=0, keepdims=True)
    acc0 = _dot(vt_ref[0, 0, i], p.astype(BF16))

    def scores(j):
        rows = pl.ds(pl.multiple_of(j * blk, blk), blk)
        picked = sel_ref[pl.ds(j, 1), :] > 0.0
        return jnp.where(picked, _dot(kb_ref[0, rows, :], qb) * scale, NEG_INF)

    def group(gi, carry):
        m, l, acc = carry
        js = [gi * grp + u for u in range(grp)]
        s = jnp.concatenate([scores(j) for j in js], axis=0)
        m_new = jnp.maximum(m, jnp.max(s, axis=0, keepdims=True))
        alpha = jnp.exp(m - m_new)
        p = jnp.exp(s - m_new)
        l = l * alpha + jnp.sum(p, axis=0, keepdims=True)
        pb = p.astype(BF16)
        acc = acc * alpha
        for u, j in enumerate(js):
            acc = acc + _dot(vt_ref[0, 0, j], pb[u * blk:(u + 1) * blk])
        return m_new, l, acc

    m, l, acc = lax.fori_loop(0, (i + grp - 1) // grp, group, (m0, l0, acc0))
    o_ref[...] = (acc / l).T.astype(o_ref.dtype)


def moba_attention_prompt(q_rot, kb, vt, kmean, *, batch, seq):
    W = MOBA_WIDTH
    blk = MOBA_BLOCK
    Dh = MOBA_HEAD_DIM
    nq = seq // blk
    nb = kmean.shape[1]
    return pl.pallas_call(
        _moba_attn_kernel,
        out_shape=jax.ShapeDtypeStruct((batch * seq, W), BF16),
        grid=(batch, MOBA_HEADS, nq),
        in_specs=[
            pl.BlockSpec((blk, Dh), lambda b, h, i: (b * nq + i, h)),
            pl.BlockSpec((1, seq, Dh), lambda b, h, i: (b, 0, h)),
            pl.BlockSpec((1, 1, nb, Dh, blk), lambda b, h, i: (b, h, 0, 0, 0)),
            pl.BlockSpec((1, nb, Dh), lambda b, h, i: (b, 0, h)),
        ],
        out_specs=pl.BlockSpec((blk, Dh), lambda b, h, i: (b * nq + i, h)),
        scratch_shapes=[pltpu.VMEM((nb, blk), F32)],
        compiler_params=_cparams("parallel", "parallel", "arbitrary"),
        name="moba_attention",
    )(q_rot, kb.reshape(batch, seq, W), vt, kmean)


PAGES_PER_STEP = 16
PAGES_PER_BLOCK = MOBA_BLOCK // PAGE_SIZE


def _kmean_pages_kernel(pt_ref, *refs):
    del pt_ref
    page_refs, o_ref = refs[:-1], refs[-1]
    for n in range(len(page_refs) // PAGES_PER_BLOCK):
        tot = page_refs[PAGES_PER_BLOCK * n][0, 0].sum(axis=0)
        for e in range(1, PAGES_PER_BLOCK):
            tot = tot + page_refs[PAGES_PER_BLOCK * n + e][0, 0].sum(axis=0)
        tot = tot * (1.0 / MOBA_BLOCK)
        for h in range(MOBA_HEADS):
            o_ref[0, 0, h, pl.ds(n, 1), :] = tot[h:h + 1, :]


def kmean_from_pages(cache_k, page_table):
    L = cache_k.shape[0]
    bs, n_pages = page_table.shape
    pps = min(PAGES_PER_STEP, n_pages)
    steps = n_pages // pps

    def page_spec(e):
        return pl.BlockSpec((1, 1, PAGE_SIZE, MOBA_HEADS, MOBA_HEAD_DIM),
                            lambda l, b, s, pt: (l, pt[b * n_pages + s * pps + e], 0, 0, 0))

    return pl.pallas_call(
        _kmean_pages_kernel,
        out_shape=jax.ShapeDtypeStruct((L, bs, MOBA_HEADS, n_pages // PAGES_PER_BLOCK, MOBA_HEAD_DIM), F32),
        grid_spec=pltpu.PrefetchScalarGridSpec(
            num_scalar_prefetch=1,
            grid=(L, bs, steps),
            in_specs=[page_spec(e) for e in range(pps)],
            out_specs=pl.BlockSpec((1, 1, MOBA_HEADS, pps // PAGES_PER_BLOCK, MOBA_HEAD_DIM),
                                   lambda l, b, s, pt: (l, b, 0, s, 0)),
        ),
        compiler_params=_cparams("parallel", "parallel", "arbitrary"),
        name="moba_paged_kmean",
    )(page_table.reshape(-1), *([cache_k] * pps))


def _moba_select_kernel(q_ref, km_ref, o_ref):
    Dh = MOBA_HEAD_DIM
    rows = []
    for h in range(MOBA_HEADS):
        rows.append(_dot_nt(q_ref[0, :, h * Dh:(h + 1) * Dh], km_ref[0, 0, h], HIGHEST))
    bs = jnp.concatenate(rows, axis=0)
    nb = bs.shape[1]
    rank = _block_rank(bs, None, 1)
    lane = lax.broadcasted_iota(jnp.int32, bs.shape, 1)
    olane = lax.broadcasted_iota(jnp.int32, (MOBA_HEADS, 128), 1)
    out = jnp.zeros((MOBA_HEADS, 128), jnp.int32)
    for s in range(min(MOBA_TOPK, nb)):
        idx = jnp.sum(jnp.where(rank == s, lane, 0), axis=-1, keepdims=True)
        out = jnp.where(olane == s, idx, out)
    o_ref[0] = out


def moba_select_sample(q3, kmean_past, *, layer):
    bs, _, W = q3.shape
    nb = kmean_past.shape[3]
    return pl.pallas_call(
        _moba_select_kernel,
        out_shape=jax.ShapeDtypeStruct((bs, MOBA_HEADS, 128), jnp.int32),
        grid=(bs,),
        in_specs=[
            pl.BlockSpec((1, 1, W), lambda b: (b, 0, 0)),
            pl.BlockSpec((1, 1, MOBA_HEADS, nb, MOBA_HEAD_DIM), lambda b: (layer, b, 0, 0, 0)),
        ],
        out_specs=pl.BlockSpec((1, MOBA_HEADS, 128), lambda b: (b, 0, 0)),
        compiler_params=_cparams("parallel"),
        name="moba_select_sample",
    )(q3, kmean_past)


def _moba_decode_kernel(sel_ref, pt_ref, q_ref, kn_ref, vn_ref, *refs, n_pg):
    del sel_ref, pt_ref
    kp_refs, vp_refs, o_ref = refs[:n_pg], refs[n_pg:2 * n_pg], refs[2 * n_pg]
    h = pl.program_id(1)
    scale = MOBA_HEAD_DIM ** -0.5
    rows = PAGE_SIZE * MOBA_HEADS
    q = q_ref[0]
    qb = q.astype(BF16)
    s_own = jnp.sum(q * kn_ref[0], axis=-1, keepdims=True) * scale
    mine = (lax.broadcasted_iota(jnp.int32, (1, rows), 1) % MOBA_HEADS) == h
    sc = [jnp.where(mine, _dot_nt(qb, kp[0, 0].reshape(rows, MOBA_HEAD_DIM).astype(BF16)) * scale, NEG_INF)
          for kp in kp_refs]
    m = s_own
    for x in sc:
        m = jnp.maximum(m, jnp.max(x, axis=-1, keepdims=True))
    p_own = jnp.exp(s_own - m)
    l = p_own
    acc = p_own * vn_ref[0]
    for x, vp in zip(sc, vp_refs):
        p = jnp.exp(x - m)
        l = l + jnp.sum(p, axis=-1, keepdims=True)
        acc = acc + _dot(p.astype(BF16), vp[0, 0].reshape(rows, MOBA_HEAD_DIM).astype(BF16))
    o_ref[0] = (acc / l).astype(o_ref.dtype)


def moba_decode(sel_flat, pt_flat, q3, k3, M3, cache_k, cache_v, *, layer, n_pages, n_sel, v_col):
    bs, _, W = q3.shape
    Dh = MOBA_HEAD_DIM
    n_pg = n_sel * PAGES_PER_BLOCK

    def page_spec(e):
        def page_map(b, h, sel, pt):
            blk = sel[(b * MOBA_HEADS + h) * n_sel + e // PAGES_PER_BLOCK]
            return (layer, pt[b * n_pages + blk * PAGES_PER_BLOCK + e % PAGES_PER_BLOCK], 0, 0, 0)
        return pl.BlockSpec((1, 1, PAGE_SIZE, MOBA_HEADS, Dh), page_map)

    vec = lambda col: pl.BlockSpec((1, 1, Dh), lambda b, h, sel, pt: (b, 0, col(h)))
    return pl.pallas_call(
        functools.partial(_moba_decode_kernel, n_pg=n_pg),
        out_shape=jax.ShapeDtypeStruct((bs, 1, W), F32),
        grid_spec=pltpu.PrefetchScalarGridSpec(
            num_scalar_prefetch=2,
            grid=(bs, MOBA_HEADS),
            in_specs=[vec(lambda h: h), vec(lambda h: h), vec(lambda h: v_col // Dh + h)]
            + [page_spec(e) for e in range(n_pg)] * 2,
            out_specs=vec(lambda h: h),
        ),
        compiler_params=_cparams("parallel", "arbitrary"),
        name="moba_decode",
    )(sel_flat, pt_flat, q3, k3, M3, *([cache_k] * n_pg), *([cache_v] * n_pg))


def _merge_kernel(ya_ref, yb_ref, yc_ref, wa_ref, wb_ref, wc_ref, z0_ref, z1_ref, z2_ref, o_ref):
    acc = _sigmoid(z0_ref[...]) * _dot(ya_ref[...].astype(BF16), wa_ref[...])
    acc = acc + _sigmoid(z1_ref[...]) * _dot(yb_ref[...].astype(BF16), wb_ref[...])
    acc = acc + _sigmoid(z2_ref[...]) * _dot(yc_ref[...].astype(BF16), wc_ref[...])
    o_ref[...] = acc.astype(o_ref.dtype)


def gated_merge(ya, yb, yc, wa, wb, wc, proj, layer, *, z_col, tn=PROJ_TILE, tm_pref=512):
    m = ya.shape[0]
    d = wa.shape[2]
    tm = _row_tile(m, tm_pref)
    nj = d // tn
    z0 = z_col // tn
    wspec = lambda w: pl.BlockSpec((None, w.shape[1], tn), lambda i, j: (layer, 0, j))
    return pl.pallas_call(
        _merge_kernel,
        out_shape=jax.ShapeDtypeStruct((m, d), BF16),
        grid=(m // tm, nj),
        in_specs=[
            pl.BlockSpec((tm, ya.shape[1]), lambda i, j: (i, 0)),
            pl.BlockSpec((tm, yb.shape[1]), lambda i, j: (i, 0)),
            pl.BlockSpec((tm, yc.shape[1]), lambda i, j: (i, 0)),
            wspec(wa), wspec(wb), wspec(wc),
            pl.BlockSpec((tm, tn), lambda i, j: (i, z0 + j)),
            pl.BlockSpec((tm, tn), lambda i, j: (i, z0 + nj + j)),
            pl.BlockSpec((tm, tn), lambda i, j: (i, z0 + 2 * nj + j)),
        ],
        out_specs=pl.BlockSpec((tm, tn), lambda i, j: (i, j)),
        compiler_params=_cparams("parallel", "arbitrary"),
        name="gated_merge",
    )(ya, yb, yc, wa, wb, wc, proj, proj, proj)


def _matmul_residual_kernel(a_ref, w_ref, x_ref, o_ref):
    o_ref[...] = x_ref[...] + _dot(a_ref[...], w_ref[...])


def matmul_residual(a, w, x, layer, *, tn, tm_pref=512):
    m, k = a.shape
    n = w.shape[2]
    tm = _row_tile(m, tm_pref)
    return pl.pallas_call(
        _matmul_residual_kernel,
        out_shape=jax.ShapeDtypeStruct((m, n), F32),
        grid=(m // tm, n // tn),
        in_specs=[
            pl.BlockSpec((tm, k), lambda i, j: (i, 0)),
            pl.BlockSpec((None, k, tn), lambda i, j: (layer, 0, j)),
            pl.BlockSpec((tm, tn), lambda i, j: (i, j)),
        ],
        out_specs=pl.BlockSpec((tm, tn), lambda i, j: (i, j)),
        compiler_params=_cparams("parallel", "arbitrary"),
        name="matmul_residual",
    )(a, w, x)


def _ffn_in_kernel(x_ref, g_ref, wg_ref, wv_ref, o_ref, h_ref):
    @pl.when(pl.program_id(1) == 0)
    def _():
        x = x_ref[...]
        ms = jnp.mean(x * x, axis=-1, keepdims=True)
        h_ref[...] = (x * lax.rsqrt(ms + NORM_EPS) * g_ref[...]).astype(BF16)

    h = h_ref[...]
    gate = _dot(h, wg_ref[...])
    val = _dot(h, wv_ref[...])
    o_ref[...] = (gate * _sigmoid(gate) * val).astype(o_ref.dtype)


def ffn_in(x, g, w, layer, *, tn=512, tm_pref=1024):
    m, d = x.shape
    hidden = w.shape[2] // 2
    tm = _row_tile(m, tm_pref)
    nj = hidden // tn
    return pl.pallas_call(
        _ffn_in_kernel,
        out_shape=jax.ShapeDtypeStruct((m, hidden), BF16),
        grid=(m // tm, nj),
        in_specs=[
            pl.BlockSpec((tm, d), lambda i, j: (i, 0)),
            pl.BlockSpec((1, d), lambda i, j: (0, 0)),
            pl.BlockSpec((None, d, tn), lambda i, j: (layer, 0, j)),
            pl.BlockSpec((None, d, tn), lambda i, j: (layer, 0, nj + j)),
        ],
        out_specs=pl.BlockSpec((tm, tn), lambda i, j: (i, j)),
        scratch_shapes=[pltpu.VMEM((tm, d), BF16)],
        compiler_params=_cparams("parallel", "arbitrary"),
        name="ffn_in_swiglu",
    )(x, g.reshape(1, d), w, w)


def _rope_tables(pos):
    half = MOBA_HEAD_DIM // 2
    inv = ROPE_THETA ** (-jnp.arange(half, dtype=F32) / half)
    ang = pos.astype(F32)[:, None] * inv[None, :]
    cos, sin = jnp.cos(ang), jnp.sin(ang)
    return jnp.concatenate([cos, cos], axis=-1), jnp.concatenate([-sin, sin], axis=-1)


def _bf16_weights(w):
    o1 = RWKV_PROJ
    o2 = o1 + 3 * MOBA_WIDTH
    o3 = o2 + 2 * GLA_K_WIDTH + GLA_V_WIDTH
    o4 = o3 + GLA_GATE_RANK
    o5 = o4 + GLA_V_WIDTH
    w_in = w['w_in']
    zeros = lambda n: jnp.zeros(w_in.shape[:2] + (n,), w_in.dtype)
    gla = [w_in[..., o2:o3], w_in[..., o4:o5], w_in[..., o3:o4]]
    packed = jnp.concatenate(
        [w_in[..., :o1], zeros(PROJ_SLOT - o1)] + gla + [zeros(PROJ_SLOT - (o5 - o2))]
        + [w_in[..., o1:o2], w_in[..., o5:]], axis=-1)
    out = {k: w[k].astype(BF16) for k in ('w_up_rwkv', 'w_up_moba', 'w_up_gla', 'w_out', 'w_ffn_in', 'w_ffn_out')}
    out['w_in'] = packed.astype(BF16)
    return out


def _mix_and_ffn(x, ya, yb, yc, proj, lw, wb, layer):
    merged = gated_merge(ya, yb, yc, wb['w_up_rwkv'], wb['w_up_moba'], wb['w_up_gla'], proj, layer,
                         z_col=COL_GATES)
    x = matmul_residual(merged, wb['w_out'], x, layer, tn=1024, tm_pref=1024)
    act = ffn_in(x, lw['norm_ffn'], wb['w_ffn_in'], layer)
    return matmul_residual(act, wb['w_ffn_out'], x, layer, tn=512)


def _prompt_layer(x, lw, wb, rope, layer, *, batch, seq):
    proj = norm_matmul(x, lw['norm_mix'], wb['w_in'], layer)

    tt = min(seq, 256)
    ya, r_S, shift = rwkv_branch(
        proj, jnp.zeros((batch, RWKV_PROJ), F32),
        jnp.zeros((batch, RWKV_HEADS, RWKV_HEAD_DIM, RWKV_HEAD_DIM), F32), lw,
        batch=batch, t_pad=seq, t_real=seq, tt=tt, chunk=64)
    yc, g_S = gla_branch(
        proj, jnp.zeros((batch, GLA_HEADS, GLA_KEY_DIM, GLA_VALUE_DIM), F32), lw,
        batch=batch, t_pad=seq, t_real=seq, tt=tt, chunk=64, sub=16, col_block=COL_GLA // PROJ_SLOT)

    nq = seq // MOBA_BLOCK
    q_rot, k_rot, kb, vt, kmean = moba_prep_prompt(proj, rope[0], rope[1], lw['moba_q_norm'], lw['moba_k_norm'],
                                                   batch=batch, seq=seq, col0=COL_MOBA // MOBA_WIDTH)
    yb = moba_attention_prompt(q_rot, kb, vt, kmean.reshape(batch, nq, MOBA_WIDTH), batch=batch, seq=seq)

    x = _mix_and_ffn(x, ya, yb, yc, proj, lw, wb, layer)
    k_new = k_rot.reshape(batch, seq, MOBA_HEADS, MOBA_HEAD_DIM)
    v_new = proj[:, COL_MOBA + 2 * MOBA_WIDTH:COL_GATES].reshape(batch, seq, MOBA_HEADS, MOBA_HEAD_DIM)
    return x, k_new, v_new, r_S, shift, g_S


def _sample_layer(x, lw, wb, rope, shift0, rwkv_S0, gla_S0, kmean_past, pt_flat, cache_k, cache_v,
                  *, layer, n_pages):
    bs = x.shape[0]
    proj = norm_matmul(x, lw['norm_mix'], wb['w_in'], layer)

    pad_rows = lambda a: jnp.pad(a[:, None, :], ((0, 0), (0, SUBLANES - 1), (0, 0))).reshape(bs * SUBLANES, -1)
    ya, r_S, shift = rwkv_branch(pad_rows(proj[:, COL_RWKV:COL_RWKV + RWKV_PROJ]), shift0, rwkv_S0, lw,
                                 batch=bs, t_pad=SUBLANES, t_real=1, tt=SUBLANES, chunk=SUBLANES)
    yc, g_S = gla_branch(pad_rows(proj[:, COL_GLA:COL_GLA + PROJ_SLOT]), gla_S0, lw,
                         batch=bs, t_pad=SUBLANES, t_real=1, tt=SUBLANES, chunk=SUBLANES, sub=SUBLANES)
    ya = ya[::SUBLANES]
    yc = yc[::SUBLANES]

    q_rot, k_rot = moba_prep_sample(proj, rope[0], rope[1], lw['moba_q_norm'], lw['moba_k_norm'],
                                    col0=COL_MOBA // MOBA_WIDTH)
    q3 = q_rot.reshape(bs, 1, MOBA_WIDTH)
    n_sel = min(MOBA_TOPK, kmean_past.shape[3])
    sel = moba_select_sample(q3, kmean_past, layer=layer)[:, :, :n_sel].reshape(-1)
    v_col = COL_MOBA + 2 * MOBA_WIDTH
    yb = moba_decode(sel, pt_flat, q3, k_rot.reshape(bs, 1, MOBA_WIDTH), proj.reshape(bs, 1, -1),
                     cache_k, cache_v, layer=layer, n_pages=n_pages, n_sel=n_sel, v_col=v_col).reshape(bs, MOBA_WIDTH)

    x = _mix_and_ffn(x, ya, yb, yc, proj, lw, wb, layer)
    k_new = k_rot.reshape(bs, 1, MOBA_HEADS, MOBA_HEAD_DIM)
    v_new = proj[:, v_col:COL_GATES].reshape(bs, 1, MOBA_HEADS, MOBA_HEAD_DIM)
    return x, k_new, v_new, r_S, shift, g_S


_LAYER_KEYS = ('norm_mix', 'w_in', 'rwkv_mu', 'rwkv_w0', 'rwkv_w_up', 'rwkv_a0', 'rwkv_a_up', 'rwkv_g_up',
               'rwkv_k_k', 'rwkv_k_a', 'rwkv_r_k', 'rwkv_ln_w', 'rwkv_ln_b', 'moba_q_norm', 'moba_k_norm',
               'gla_a_up', 'gla_a_bias', 'gla_o_norm', 'w_up_rwkv', 'w_up_moba', 'w_up_gla', 'w_out',
               'norm_ffn', 'w_ffn_in', 'w_ffn_out')


def kernel(x_prompt, x_sample, cache_k, cache_v, page_table, state_rwkv, state_rwkv_shift, state_gla, norm_mix, w_in, rwkv_mu, rwkv_w0, rwkv_w_up, rwkv_a0, rwkv_a_up, rwkv_g_up, rwkv_k_k, rwkv_k_a, rwkv_r_k, rwkv_ln_w, rwkv_ln_b, moba_q_norm, moba_k_norm, gla_a_up, gla_a_bias, gla_o_norm, w_up_rwkv, w_up_moba, w_up_gla, w_out, norm_ffn, w_ffn_in, w_ffn_out):
    stacked = dict(zip(_LAYER_KEYS, (
        norm_mix, w_in, rwkv_mu, rwkv_w0, rwkv_w_up, rwkv_a0, rwkv_a_up, rwkv_g_up, rwkv_k_k, rwkv_k_a,
        rwkv_r_k, rwkv_ln_w, rwkv_ln_b, moba_q_norm, moba_k_norm, gla_a_up, gla_a_bias, gla_o_norm,
        w_up_rwkv, w_up_moba, w_up_gla, w_out, norm_ffn, w_ffn_in, w_ffn_out)))
    depth = w_in.shape[0]
    bp, seq, d = x_prompt.shape
    bs, dec_seq, _ = x_sample.shape
    n_pages = page_table.shape[1]
    past_len = n_pages * cache_k.shape[2]
    assert dec_seq == 1 and cache_k.shape[2:] == (PAGE_SIZE, MOBA_HEADS, MOBA_HEAD_DIM)
    assert seq % MOBA_BLOCK == 0 and past_len % MOBA_BLOCK == 0 and past_len // MOBA_BLOCK >= MOBA_TOPK

    pt_flat = page_table.reshape(-1)
    kmean_past = kmean_from_pages(cache_k, page_table)

    rope_p = _rope_tables(jnp.arange(seq))
    rope_s = _rope_tables(jnp.full((bs,), past_len))

    yp = x_prompt.reshape(bp * seq, d)
    ys = x_sample.reshape(bs, d)
    wb = _bf16_weights(stacked)
    small = [k for k in _LAYER_KEYS if k not in wb]
    outs_p, outs_s = [], []
    for l in range(depth):
        lw = {k: stacked[k][l] for k in small}
        yp, *rest_p = _prompt_layer(yp, lw, wb, rope_p, l, batch=bp, seq=seq)
        outs_p.append(rest_p)
        ys, *rest_s = _sample_layer(ys, lw, wb, rope_s, state_rwkv_shift[l], state_rwkv[l], state_gla[l],
                                    kmean_past, pt_flat, cache_k, cache_v, layer=l, n_pages=n_pages)
        outs_s.append(rest_s)
    stack = lambda outs, i: jnp.stack([o[i] for o in outs])
    return (yp.reshape(bp, seq, d), ys.reshape(bs, 1, d),
            stack(outs_p, 0), stack(outs_p, 1), stack(outs_p, 2), stack(outs_p, 3), stack(outs_p, 4),
            stack(outs_s, 0), stack(outs_s, 1), stack(outs_s, 2), stack(outs_s, 3), stack(outs_s, 4))
```

```python
import functools

import jax
import jax.numpy as jnp
from jax import lax
from jax.experimental import pallas as pl
from jax.experimental.pallas import tpu as pltpu

F32 = jnp.float32
BF16 = jnp.bfloat16
HIGHEST = lax.Precision.HIGHEST

PAGE_SIZE = 128
RWKV_HEADS = 8
RWKV_HEAD_DIM = 64
RWKV_WIDTH = RWKV_HEADS * RWKV_HEAD_DIM
RWKV_DECAY_RANK = 64
RWKV_ICLR_RANK = 64
RWKV_GATE_RANK = 128
RWKV_PROJ = 3 * RWKV_WIDTH + RWKV_DECAY_RANK + RWKV_ICLR_RANK + RWKV_GATE_RANK
RWKV_GN_EPS = 64e-5
MOBA_HEADS = 8
MOBA_HEAD_DIM = 128
MOBA_WIDTH = MOBA_HEADS * MOBA_HEAD_DIM
MOBA_BLOCK = 256
MOBA_TOPK = 3
ROPE_THETA = 10000.0
GLA_HEADS = 4
GLA_KEY_DIM = 64
GLA_VALUE_DIM = 128
GLA_K_WIDTH = GLA_HEADS * GLA_KEY_DIM
GLA_V_WIDTH = GLA_HEADS * GLA_VALUE_DIM
GLA_GATE_RANK = 16
GLA_GATE_PAD = 256
GLA_SECT = 2 * GLA_K_WIDTH + 2 * GLA_V_WIDTH + GLA_GATE_PAD
GLA_TAU = 16.0
PROJ_SLOT = 2048
PROJ_TILE = 1024
COL_RWKV = 0
COL_GLA = PROJ_SLOT
COL_MOBA = 2 * PROJ_SLOT
COL_GATES = COL_MOBA + 3 * MOBA_WIDTH
N_BRANCHES = 3
NORM_EPS = 1e-6
NEG_INF = -1e30

VMEM_LIMIT_BYTES = 56 * 1024 * 1024
SUBLANES = 8

GATE_PASSES = 3
STATE_PASSES = 3
RWKV_CHUNK_PASSES = 1
GLA_CHUNK_PASSES = 1


def _cparams(*sem):
    return pltpu.CompilerParams(dimension_semantics=sem, vmem_limit_bytes=VMEM_LIMIT_BYTES)


def _dot(a, b, precision=None):
    return jnp.dot(a, b, preferred_element_type=F32, precision=precision)


def _dot_nt(a, b, precision=None):
    return lax.dot_general(a, b, (((1,), (1,)), ((), ())), preferred_element_type=F32, precision=precision)


def _dot_tn(a, b, precision=None):
    return lax.dot_general(a, b, (((0,), (0,)), ((), ())), preferred_element_type=F32, precision=precision)


def _mm(a, b, kind, passes):
    f = {'nn': _dot, 'nt': _dot_nt, 'tn': _dot_tn}[kind]
    if passes == 6:
        return f(a, b, HIGHEST)
    ah = a.astype(BF16)
    bh = b.astype(BF16)
    if passes == 1:
        return f(ah, bh)
    al = (a - ah.astype(F32)).astype(BF16)
    bl = (b - bh.astype(F32)).astype(BF16)
    return f(ah, bh) + f(ah, bl) + f(al, bh)


def _split3(x):
    hi = x.astype(BF16)
    r1 = x - hi.astype(F32)
    mid = r1.astype(BF16)
    lo = (r1 - mid.astype(F32)).astype(BF16)
    return hi, mid, lo


def _tril_ones(n, strict=False):
    r = lax.broadcasted_iota(jnp.int32, (n, n), 0)
    c = lax.broadcasted_iota(jnp.int32, (n, n), 1)
    return (r > c) if strict else (r >= c)


def _cumsum_rows(x, tri_bf16):
    hi, mid, lo = _split3(x)
    return _dot(tri_bf16, hi) + _dot(tri_bf16, mid) + _dot(tri_bf16, lo)


def _sigmoid(x):
    return 1.0 / (1.0 + jnp.exp(-x))


def _softplus(x):
    return jnp.maximum(x, 0.0) + jnp.log(1.0 + jnp.exp(-jnp.abs(x)))


def _row_tile(m, pref):
    return pref if m % pref == 0 else m


def _act_dtype(tile_rows):
    return BF16 if tile_rows % (2 * SUBLANES) == 0 else F32


def _norm_matmul_kernel(x_ref, g_ref, w_ref, o_ref, h_ref):
    @pl.when(pl.program_id(1) == 0)
    def _():
        x = x_ref[...]
        ms = jnp.mean(x * x, axis=-1, keepdims=True)
        h_ref[...] = (x * lax.rsqrt(ms + NORM_EPS) * g_ref[...]).astype(BF16)

    o_ref[...] = _dot(h_ref[...], w_ref[...]).astype(o_ref.dtype)


def norm_matmul(x, g, w, layer, *, tn=1024, tm_pref=1024):
    m, d = x.shape
    n = w.shape[2]
    tm = _row_tile(m, tm_pref)
    return pl.pallas_call(
        _norm_matmul_kernel,
        out_shape=jax.ShapeDtypeStruct((m, n), F32),
        grid=(m // tm, n // tn),
        in_specs=[
            pl.BlockSpec((tm, d), lambda i, j: (i, 0)),
            pl.BlockSpec((1, d), lambda i, j: (0, 0)),
            pl.BlockSpec((None, d, tn), lambda i, j: (layer, 0, j)),
        ],
        out_specs=pl.BlockSpec((tm, tn), lambda i, j: (i, j)),
        scratch_shapes=[pltpu.VMEM((tm, d), BF16)],
        compiler_params=_cparams("parallel", "arbitrary"),
        name="norm_matmul",
    )(x, g.reshape(1, d), w)


def _rwkv_kernel(p_ref, shift0_ref, s0_ref, mu_ref, w0_ref, wup_ref, a0_ref, aup_ref, gup_ref,
                 kk_ref, ka_ref, rk_ref, lnw_ref, lnb_ref,
                 y_ref, sout_ref, shout_ref,
                 carry_ref, s_ref, r_s, k_s, v_s, kk_s, b_s, lw_s, y_s,
                 *, t_real, tt, chunk):
    t = pl.program_id(1)
    nt = pl.num_programs(1)
    W = RWKV_WIDTH
    N = RWKV_HEAD_DIM
    H = RWKV_HEADS
    C = chunk
    cp = RWKV_CHUNK_PASSES

    @pl.when(t == 0)
    def _():
        carry_ref[...] = shift0_ref[0]
        s_ref[...] = s0_ref[0]

    P = p_ref[...]
    rowid = lax.broadcasted_iota(jnp.int32, (tt, 1), 0)
    prev = jnp.where(rowid == 0, carry_ref[...], pltpu.roll(P, 1, axis=0))
    last_row = (t_real - 1) % tt
    carry_ref[...] = P[last_row:last_row + 1, :]

    @pl.when(t == nt - 1)
    def _():
        shout_ref[0] = P[last_row:last_row + 1, :]

    Pm = P + (prev - P) * mu_ref[...]
    r = Pm[:, 0:W]
    k = Pm[:, W:2 * W]
    v = Pm[:, 2 * W:3 * W]
    o = 3 * W
    wd = Pm[:, o:o + RWKV_DECAY_RANK]
    ad = Pm[:, o + RWKV_DECAY_RANK:o + RWKV_DECAY_RANK + RWKV_ICLR_RANK]
    gd = Pm[:, o + RWKV_DECAY_RANK + RWKV_ICLR_RANK:]
    w_val = -_softplus(-(w0_ref[...] + _mm(jnp.tanh(wd), wup_ref[...], 'nn', GATE_PASSES))) - 0.5
    lw = -jnp.exp(w_val)
    a = _sigmoid(a0_ref[...] + _mm(ad, aup_ref[...], 'nn', GATE_PASSES))
    g = _mm(_sigmoid(gd), gup_ref[...], 'nn', GATE_PASSES)
    kk = k * kk_ref[...]
    kmod = k * (1.0 + (a - 1.0) * ka_ref[...])
    if t_real % tt != 0:
        valid = (t * tt + rowid) < t_real
        lw = jnp.where(valid, lw, 0.0)
        kk = jnp.where(valid, kk, 0.0)
        kmod = jnp.where(valid, kmod, 0.0)
        v = jnp.where(valid, v, 0.0)
    seg_r = lax.broadcasted_iota(jnp.int32, (W, W), 0) // N
    seg_c = lax.broadcasted_iota(jnp.int32, (W, W), 1) // N
    seg = jnp.where(seg_r == seg_c, 1.0, 0.0).astype(BF16)
    hi, md, lo = _split3(kk * kk)
    ss = _dot(hi, seg) + _dot(md, seg) + _dot(lo, seg)
    kkn = kk / jnp.maximum(jnp.sqrt(ss), 1e-12)
    r_s[...] = r
    k_s[...] = kmod
    v_s[...] = v
    kk_s[...] = kkn
    b_s[...] = kkn * a
    lw_s[...] = lw

    tri = jnp.where(_tril_ones(C), 1.0, 0.0).astype(BF16)
    low_incl = _tril_ones(C)
    low_strict = _tril_ones(C, strict=True)
    eye = lax.broadcasted_iota(jnp.int32, (N, N), 0) == lax.broadcasted_iota(jnp.int32, (N, N), 1)
    mid = max(C // 2 - 1, 0)
    heads = range(H)
    hs = [slice(h * N, (h + 1) * N) for h in heads]

    def chunk_body(c, carry):
        rows = pl.ds(pl.multiple_of(c * C, C), C)
        lw_c = lw_s[rows, :]
        cum = _cumsum_rows(lw_c, tri)
        cume = cum - lw_c
        rho = cum[mid:mid + 1, :]
        cum_last = cum[C - 1:C, :]
        r_c = r_s[rows, :]
        k_c = k_s[rows, :]
        v_c = v_s[rows, :]
        kk_c = kk_s[rows, :]
        b_c = b_s[rows, :]
        e_out = jnp.exp(rho - cum)
        e_end = jnp.exp(cum_last - cum)
        kk_hat = kk_c * jnp.exp(cume - rho)
        r_hat = r_c * jnp.exp(cum - rho)
        k_til = k_c * e_out
        b_til = b_c * e_out
        kk_abs = kk_c * jnp.exp(cume)
        r_abs = r_c * jnp.exp(cum)
        k_end = k_c * e_end
        b_end = b_c * e_end
        gam = jnp.exp(cum_last)

        A = [_mm(jnp.concatenate([kk_hat[:, s], r_hat[:, s]], axis=0),
                 jnp.concatenate([k_til[:, s], b_til[:, s]], axis=0), 'nt', cp) for s in hs]
        Lk = [jnp.where(low_strict, x[0:C, 0:C], 0.0) for x in A]
        Lb = [jnp.where(low_strict, x[0:C, C:2 * C], 0.0) for x in A]
        Ark = [jnp.where(low_incl, x[C:2 * C, 0:C], 0.0) for x in A]
        Arb = [jnp.where(low_incl, x[C:2 * C, C:2 * C], 0.0) for x in A]
        X = [jnp.concatenate([kk_abs[:, s], _mm(Lk[h], v_c[:, s], 'nn', cp)], axis=1) for h, s in zip(heads, hs)]
        X = [X[h] - _mm(Lb[h], X[h], 'nn', cp) for h in heads]
        Lp = Lb
        p = 2
        while p < C:
            Lp = [_mm(x, x, 'nn', cp) for x in Lp]
            X = [X[h] + _mm(Lp[h], X[h], 'nn', cp) for h in heads]
            p *= 2
        RY = [jnp.concatenate([r_abs[:, s], _mm(Ark[h], v_c[:, s], 'nn', cp)], axis=1)
              - _mm(Arb[h], X[h], 'nn', cp) for h, s in zip(heads, hs)]
        MN = [_mm(X[h], b_end[:, s], 'tn', cp) for h, s in zip(heads, hs)]
        VK = [_mm(v_c[:, s], k_end[:, s], 'tn', cp) for s in hs]
        for h, s in zip(heads, hs):
            S = s_ref[h]
            Y = _mm(RY[h][:, 0:N], S, 'nt', STATE_PASSES) + RY[h][:, N:2 * N]
            trans = jnp.where(eye, jnp.broadcast_to(gam[:, s], (N, N)), 0.0) - MN[h][0:N]
            s_ref[h] = _mm(S, trans, 'nn', STATE_PASSES) + (VK[h] - MN[h][N:2 * N])
            mu = jnp.mean(Y, axis=-1, keepdims=True)
            var = jnp.mean(jnp.square(Y - mu), axis=-1, keepdims=True)
            yn = (Y - mu) * lax.rsqrt(var + RWKV_GN_EPS) * lnw_ref[:, s] + lnb_ref[:, s]
            bonus = jnp.sum(r_c[:, s] * k_c[:, s] * rk_ref[:, s], axis=-1, keepdims=True) * v_c[:, s]
            y_s[rows, s] = yn + bonus
        return carry

    lax.fori_loop(0, tt // C, chunk_body, 0)
    y_ref[...] = (y_s[...] * g).astype(y_ref.dtype)

    @pl.when(t == nt - 1)
    def _():
        sout_ref[0] = s_ref[...]


def rwkv_branch(P, shift0, S0, lw, *, batch, t_pad, t_real, tt, chunk):
    nt = t_pad // tt
    W = RWKV_WIDTH
    vec = lambda a: a.reshape(1, -1)
    full = lambda shape: pl.BlockSpec(shape, lambda b, t: (0,) * len(shape))
    kern = functools.partial(_rwkv_kernel, t_real=t_real, tt=tt, chunk=chunk)
    y, s_out, sh_out = pl.pallas_call(
        kern,
        out_shape=(
            jax.ShapeDtypeStruct((batch * t_pad, W), _act_dtype(tt)),
            jax.ShapeDtypeStruct((batch, RWKV_HEADS, RWKV_HEAD_DIM, RWKV_HEAD_DIM), F32),
            jax.ShapeDtypeStruct((batch, 1, RWKV_PROJ), F32),
        ),
        grid=(batch, nt),
        in_specs=[
            pl.BlockSpec((tt, RWKV_PROJ), lambda b, t: (b * nt + t, 0)),
            pl.BlockSpec((1, 1, RWKV_PROJ), lambda b, t: (b, 0, 0)),
            pl.BlockSpec((1, RWKV_HEADS, RWKV_HEAD_DIM, RWKV_HEAD_DIM), lambda b, t: (b, 0, 0, 0)),
            full((1, RWKV_PROJ)),
            full((1, W)),
            full((RWKV_DECAY_RANK, W)),
            full((1, W)),
            full((RWKV_ICLR_RANK, W)),
            full((RWKV_GATE_RANK, W)),
            full((1, W)), full((1, W)), full((1, W)), full((1, W)), full((1, W)),
        ],
        out_specs=(
            pl.BlockSpec((tt, W), lambda b, t: (b * nt + t, 0)),
            pl.BlockSpec((1, RWKV_HEADS, RWKV_HEAD_DIM, RWKV_HEAD_DIM), lambda b, t: (b, 0, 0, 0)),
            pl.BlockSpec((1, 1, RWKV_PROJ), lambda b, t: (b, 0, 0)),
        ),
        scratch_shapes=[
            pltpu.VMEM((1, RWKV_PROJ), F32),
            pltpu.VMEM((RWKV_HEADS, RWKV_HEAD_DIM, RWKV_HEAD_DIM), F32),
        ] + [pltpu.VMEM((tt, W), F32) for _ in range(7)],
        compiler_params=_cparams("parallel", "arbitrary"),
        name="rwkv7_branch",
    )(P, shift0.reshape(batch, 1, RWKV_PROJ), S0,
      vec(lw['rwkv_mu']), vec(lw['rwkv_w0']), lw['rwkv_w_up'], vec(lw['rwkv_a0']), lw['rwkv_a_up'],
      lw['rwkv_g_up'], vec(lw['rwkv_k_k']), vec(lw['rwkv_k_a']), vec(lw['rwkv_r_k']),
      vec(lw['rwkv_ln_w']), vec(lw['rwkv_ln_b']))
    return y, s_out, sh_out.reshape(batch, RWKV_PROJ)


def _gla_kernel(g_ref, s0_ref, aup_ref, abias_ref, onorm_ref,
                y_ref, sout_ref,
                s_ref, la_s,
                *, t_real, tt, chunk, sub):
    t = pl.program_id(1)
    nt = pl.num_programs(1)
    KW, VW = GLA_K_WIDTH, GLA_V_WIDTH
    dk, dv = GLA_KEY_DIM, GLA_VALUE_DIM
    C = chunk
    cp = GLA_CHUNK_PASSES

    @pl.when(t == 0)
    def _():
        s_ref[...] = s0_ref[0]

    gl = g_ref[:, 2 * KW + 2 * VW:GLA_SECT]
    x = _mm(gl, aup_ref[...], 'nn', GATE_PASSES) + abias_ref[...]
    la = -_softplus(-x) * (1.0 / GLA_TAU)
    padded = t_real % tt != 0
    if padded:
        rowid = lax.broadcasted_iota(jnp.int32, (tt, 1), 0)
        valid = (t * tt + rowid) < t_real
        la = jnp.where(valid, la, 0.0)
    la_s[...] = la

    tri = jnp.where(_tril_ones(C), 1.0, 0.0).astype(BF16)
    ones_cv = jnp.ones((C, dv), BF16)
    low_incl = _tril_ones(C)
    crow = lax.broadcasted_iota(jnp.int32, (C, 1), 0)
    nsub = C // sub
    heads = range(GLA_HEADS)

    def chunk_body(c, carry):
        rows = pl.ds(pl.multiple_of(c * C, C), C)
        la_c = la_s[rows, :]
        cum_all = _cumsum_rows(la_c, tri)
        if padded:
            vmask = (t * tt + c * C + crow) < t_real
        q, k, v, cums = [], [], [], []
        for h in heads:
            q.append(g_ref[rows, h * dk:(h + 1) * dk] * (dk ** -0.5))
            k_h = g_ref[rows, KW + h * dk:KW + (h + 1) * dk]
            v_h = g_ref[rows, 2 * KW + h * dv:2 * KW + (h + 1) * dv]
            if padded:
                k_h = jnp.where(vmask, k_h, 0.0)
                v_h = jnp.where(vmask, v_h, 0.0)
            k.append(k_h)
            v.append(v_h)
            cums.append(cum_all[:, h * dk:(h + 1) * dk])
        bounds = [[jnp.zeros((1, dk), F32)] + [cm[i * sub - 1:i * sub, :] for i in range(1, nsub)] for cm in cums]
        qt = [q[h] * jnp.exp(cums[h] - jnp.concatenate(
            [jnp.broadcast_to(b, (sub, dk)) for b in bounds[h]], axis=0)) for h in heads]
        att = []
        for h in heads:
            att_rows = []
            for i in range(nsub):
                kt = k[h] * jnp.exp(jnp.where(crow < (i + 1) * sub, bounds[h][i] - cums[h], NEG_INF))
                att_rows.append(_mm(qt[h][i * sub:(i + 1) * sub], kt, 'nt', cp))
            att.append(jnp.where(low_incl, jnp.concatenate(att_rows, axis=0), 0.0))
        intra = [_mm(att[h], v[h], 'nn', cp) for h in heads]
        kv = [_mm(k[h] * jnp.exp(cums[h][C - 1:C, :] - cums[h]), v[h], 'tn', STATE_PASSES) for h in heads]
        tot = []
        for h in heads:
            hi, md, lo = _split3(la_c[:, h * dk:(h + 1) * dk])
            tot.append(_dot_tn(hi, ones_cv) + _dot_tn(md, ones_cv) + _dot_tn(lo, ones_cv))
        for h in heads:
            S = s_ref[h]
            o = intra[h] + _mm(q[h] * jnp.exp(cums[h]), S, 'nn', STATE_PASSES)
            s_ref[h] = S * jnp.exp(tot[h]) + kv[h]
            r_h = g_ref[rows, 2 * KW + VW + h * dv:2 * KW + VW + (h + 1) * dv]
            ms = jnp.mean(o * o, axis=-1, keepdims=True)
            on = o * lax.rsqrt(ms + NORM_EPS) * onorm_ref[...]
            y_ref[rows, h * dv:(h + 1) * dv] = (on * (r_h * _sigmoid(r_h))).astype(y_ref.dtype)
        return carry

    lax.fori_loop(0, tt // C, chunk_body, 0)

    @pl.when(t == nt - 1)
    def _():
        sout_ref[0] = s_ref[...]


def gla_branch(G, S0, lw, *, batch, t_pad, t_real, tt, chunk, sub, col_block=0):
    nt = t_pad // tt
    full = lambda shape: pl.BlockSpec(shape, lambda b, t: (0,) * len(shape))
    aup = jnp.zeros((GLA_GATE_PAD, GLA_K_WIDTH), F32).at[:GLA_GATE_RANK].set(lw['gla_a_up'])
    kern = functools.partial(_gla_kernel, t_real=t_real, tt=tt, chunk=chunk, sub=sub)
    y, s_out = pl.pallas_call(
        kern,
        out_shape=(
            jax.ShapeDtypeStruct((batch * t_pad, GLA_V_WIDTH), _act_dtype(tt)),
            jax.ShapeDtypeStruct((batch, GLA_HEADS, GLA_KEY_DIM, GLA_VALUE_DIM), F32),
        ),
        grid=(batch, nt),
        in_specs=[
            pl.BlockSpec((tt, PROJ_SLOT), lambda b, t: (b * nt + t, col_block)),
            pl.BlockSpec((1, GLA_HEADS, GLA_KEY_DIM, GLA_VALUE_DIM), lambda b, t: (b, 0, 0, 0)),
            full((GLA_GATE_PAD, GLA_K_WIDTH)),
            full((1, GLA_K_WIDTH)),
            full((1, GLA_VALUE_DIM)),
        ],
        out_specs=(
            pl.BlockSpec((tt, GLA_V_WIDTH), lambda b, t: (b * nt + t, 0)),
            pl.BlockSpec((1, GLA_HEADS, GLA_KEY_DIM, GLA_VALUE_DIM), lambda b, t: (b, 0, 0, 0)),
        ),
        scratch_shapes=[
            pltpu.VMEM((GLA_HEADS, GLA_KEY_DIM, GLA_VALUE_DIM), F32),
            pltpu.VMEM((tt, GLA_K_WIDTH), F32),
        ],
        compiler_params=_cparams("parallel", "arbitrary"),
        name="gla_branch",
    )(G, S0, aup, lw['gla_a_bias'].reshape(1, -1), lw['gla_o_norm'].reshape(1, -1))
    return y, s_out


def _norm_rope(x, g, cos, sin):
    ms = jnp.mean(x * x, axis=-1, keepdims=True)
    y = x * lax.rsqrt(ms + NORM_EPS) * g
    return y * cos + pltpu.roll(y, MOBA_HEAD_DIM // 2, axis=1) * sin


def _moba_prep_prompt_kernel(q_ref, k_ref, v_ref, cos_ref, sin_ref, gq_ref, gk_ref,
                             qo_ref, ko_ref, vo_ref, kb_ref, vt_ref, km_ref):
    Dh = MOBA_HEAD_DIM
    cos = cos_ref[...]
    sin = sin_ref[...]
    rows = q_ref.shape[0]
    vo_ref[...] = v_ref[...]
    for h in range(MOBA_HEADS):
        sl = slice(h * Dh, (h + 1) * Dh)
        qo_ref[:, sl] = _norm_rope(q_ref[:, sl], gq_ref[...], cos, sin)
        kr = _norm_rope(k_ref[:, sl], gk_ref[...], cos, sin)
        ko_ref[:, sl] = kr
        kb_ref[:, sl] = kr.astype(BF16)
        km_ref[0, :, sl] = jnp.sum(kr, axis=0, keepdims=True) * (1.0 / rows)
        vt_ref[0, h, 0] = v_ref[:, sl].T.astype(BF16)


def moba_prep_prompt(M, cos, sin, gq, gk, *, batch, seq, col0=0):
    m = M.shape[0]
    W = MOBA_WIDTH
    blk = MOBA_BLOCK
    Dh = MOBA_HEAD_DIM
    nq = seq // blk
    nblk = m // blk
    return pl.pallas_call(
        _moba_prep_prompt_kernel,
        out_shape=(
            jax.ShapeDtypeStruct((m, W), F32),
            jax.ShapeDtypeStruct((m, W), F32),
            jax.ShapeDtypeStruct((m, W), F32),
            jax.ShapeDtypeStruct((m, W), BF16),
            jax.ShapeDtypeStruct((batch, MOBA_HEADS, nq, Dh, blk), BF16),
            jax.ShapeDtypeStruct((nblk, 1, W), F32),
        ),
        grid=(nblk,),
        in_specs=[
            pl.BlockSpec((blk, W), lambda i: (i, col0)),
            pl.BlockSpec((blk, W), lambda i: (i, col0 + 1)),
            pl.BlockSpec((blk, W), lambda i: (i, col0 + 2)),
            pl.BlockSpec((blk, Dh), lambda i: (i % nq, 0)),
            pl.BlockSpec((blk, Dh), lambda i: (i % nq, 0)),
            pl.BlockSpec((1, Dh), lambda i: (0, 0)),
            pl.BlockSpec((1, Dh), lambda i: (0, 0)),
        ],
        out_specs=(
            pl.BlockSpec((blk, W), lambda i: (i, 0)),
            pl.BlockSpec((blk, W), lambda i: (i, 0)),
            pl.BlockSpec((blk, W), lambda i: (i, 0)),
            pl.BlockSpec((blk, W), lambda i: (i, 0)),
            pl.BlockSpec((1, MOBA_HEADS, 1, Dh, blk), lambda i: (i // nq, 0, i % nq, 0, 0)),
            pl.BlockSpec((1, 1, W), lambda i: (i, 0, 0)),
        ),
        compiler_params=_cparams("parallel"),
        name="moba_qkv_prep",
    )(M, M, M, cos, sin, gq.reshape(1, -1), gk.reshape(1, -1))


def _moba_prep_sample_kernel(q_ref, k_ref, cos_ref, sin_ref, gq_ref, gk_ref, qo_ref, ko_ref):
    Dh = MOBA_HEAD_DIM
    cos = cos_ref[...]
    sin = sin_ref[...]
    for h in range(MOBA_HEADS):
        sl = slice(h * Dh, (h + 1) * Dh)
        qo_ref[:, sl] = _norm_rope(q_ref[:, sl], gq_ref[...], cos, sin)
        ko_ref[:, sl] = _norm_rope(k_ref[:, sl], gk_ref[...], cos, sin)


def moba_prep_sample(M, cos, sin, gq, gk, *, col0=0):
    m = M.shape[0]
    W = MOBA_WIDTH
    Dh = MOBA_HEAD_DIM
    return pl.pallas_call(
        _moba_prep_sample_kernel,
        out_shape=(jax.ShapeDtypeStruct((m, W), F32), jax.ShapeDtypeStruct((m, W), F32)),
        grid=(1,),
        in_specs=[
            pl.BlockSpec((m, W), lambda i: (0, col0)),
            pl.BlockSpec((m, W), lambda i: (0, col0 + 1)),
            pl.BlockSpec((m, Dh), lambda i: (0, 0)),
            pl.BlockSpec((m, Dh), lambda i: (0, 0)),
            pl.BlockSpec((1, Dh), lambda i: (0, 0)),
            pl.BlockSpec((1, Dh), lambda i: (0, 0)),
        ],
        out_specs=(pl.BlockSpec((m, W), lambda i: (0, 0)), pl.BlockSpec((m, W), lambda i: (0, 0))),
        compiler_params=_cparams("arbitrary"),
        name="moba_qk_prep_sample",
    )(M, M, cos, sin, gq.reshape(1, -1), gk.reshape(1, -1))


def _block_rank(bs, n_valid, axis):
    nb = bs.shape[axis]
    idx = lax.broadcasted_iota(jnp.int32, bs.shape, axis)
    rank = jnp.zeros(bs.shape, jnp.int32)
    for mm in range(nb):
        one = bs[mm:mm + 1, :] if axis == 0 else bs[:, mm:mm + 1]
        beats = (one > bs) | ((one == bs) & (idx > mm))
        if n_valid is not None:
            beats = beats & (mm < n_valid)
        rank = rank + jnp.where(beats, 1, 0)
    return rank


def _moba_attn_kernel(q_ref, kb_ref, vt_ref, km_ref, o_ref, sel_ref, s_ref):
    i = pl.program_id(2)
    blk = MOBA_BLOCK
    scale = MOBA_HEAD_DIM ** -0.5
    qT = q_ref[...].T
    km = km_ref[0]
    nb = km.shape[0]
    grp = next(g for g in (4, 2, 1) if nb % g == 0)
    bsT = _dot(km, qT, HIGHEST)
    blk_id = lax.broadcasted_iota(jnp.int32, (nb, blk), 0)
    rank = _block_rank(bsT, i, 0)
    sel_ref[...] = jnp.where((blk_id < i) & (rank < MOBA_TOPK), 1.0, 0.0)
    qb = qT.astype(BF16)
    causal = (lax.broadcasted_iota(jnp.int32, (blk, blk), 0)
              <= lax.broadcasted_iota(jnp.int32, (blk, blk), 1))
    n_groups = i // grp + 1

    def score_group(gi, m):
        for u in range(grp):
            j = gi * grp + u
            rows = pl.ds(pl.multiple_of(j * blk, blk), blk)
            picked = sel_ref[pl.ds(j, 1), :] > 0.0
            keep = picked | ((j == i) & causal)
            s = jnp.where(keep, _dot(kb_ref[0, rows, :], qb) * scale, NEG_INF)
            s_ref[rows, :] = s
            m = jnp.maximum(m, jnp.max(s, axis=0, keepdims=True))
        return m

    m = lax.fori_loop(0, n_groups, score_group, jnp.full((1, blk), NEG_INF, F32))

    def value_group(gi, carry):
        l, acc = carry
        for u in range(grp):
            j = gi * grp + u
            rows = pl.ds(pl.multiple_of(j * blk, blk), blk)
            p = jnp.exp(s_ref[rows, :] - m)
            l = l + jnp.sum(p, axis=0, keepdims=True)
            acc = acc + _dot(vt_ref[0, 0, j], p.astype(BF16))
        return l, acc

    l, acc = lax.fori_loop(0, n_groups, value_group,
                           (jnp.zeros((1, blk), F32), jnp.zeros((MOBA_HEAD_DIM, blk), F32)))
    o_ref[...] = (acc / l).T.astype(o_ref.dtype)


def moba_attention_prompt(q_rot, kb, vt, kmean, *, batch, seq):
    W = MOBA_WIDTH
    blk = MOBA_BLOCK
    Dh = MOBA_HEAD_DIM
    nq = seq // blk
    nb = kmean.shape[1]
    return pl.pallas_call(
        _moba_attn_kernel,
        out_shape=jax.ShapeDtypeStruct((batch * seq, W), BF16),
        grid=(batch, MOBA_HEADS, nq),
        in_specs=[
            pl.BlockSpec((blk, Dh), lambda b, h, i: (b * nq + i, h)),
            pl.BlockSpec((1, seq, Dh), lambda b, h, i: (b, 0, h)),
            pl.BlockSpec((1, 1, nb, Dh, blk), lambda b, h, i: (b, h, 0, 0, 0)),
            pl.BlockSpec((1, nb, Dh), lambda b, h, i: (b, 0, h)),
        ],
        out_specs=pl.BlockSpec((blk, Dh), lambda b, h, i: (b * nq + i, h)),
        scratch_shapes=[pltpu.VMEM((nb, blk), F32), pltpu.VMEM((nb * blk, blk), F32)],
        compiler_params=_cparams("parallel", "parallel", "arbitrary"),
        name="moba_attention",
    )(q_rot, kb.reshape(batch, seq, W), vt, kmean)


PAGES_PER_STEP = 16
PAGES_PER_BLOCK = MOBA_BLOCK // PAGE_SIZE


def _moba_select_kernel(q_ref, km_ref, o_ref):
    Dh = MOBA_HEAD_DIM
    rows = []
    for h in range(MOBA_HEADS):
        rows.append(_dot_nt(q_ref[0, :, h * Dh:(h + 1) * Dh], km_ref[0, h], HIGHEST))
    bs = jnp.concatenate(rows, axis=0)
    nb = bs.shape[1]
    rank = _block_rank(bs, None, 1)
    lane = lax.broadcasted_iota(jnp.int32, bs.shape, 1)
    olane = lax.broadcasted_iota(jnp.int32, (MOBA_HEADS, 128), 1)
    out = jnp.zeros((MOBA_HEADS, 128), jnp.int32)
    for s in range(min(MOBA_TOPK, nb)):
        idx = jnp.sum(jnp.where(rank == s, lane, 0), axis=-1, keepdims=True)
        out = jnp.where(olane == s, idx, out)
    o_ref[0] = out


def moba_select_sample(q3, kmean_past):
    bs, _, W = q3.shape
    nb = kmean_past.shape[2]
    return pl.pallas_call(
        _moba_select_kernel,
        out_shape=jax.ShapeDtypeStruct((bs, MOBA_HEADS, 128), jnp.int32),
        grid=(bs,),
        in_specs=[
            pl.BlockSpec((1, 1, W), lambda b: (b, 0, 0)),
            pl.BlockSpec((1, MOBA_HEADS, nb, MOBA_HEAD_DIM), lambda b: (b, 0, 0, 0)),
        ],
        out_specs=pl.BlockSpec((1, MOBA_HEADS, 128), lambda b: (b, 0, 0)),
        compiler_params=_cparams("parallel"),
        name="moba_select_sample",
    )(q3, kmean_past)


def _moba_decode_kernel(sel_ref, pt_ref, q_ref, kn_ref, vn_ref, *refs, n_pg):
    del sel_ref, pt_ref
    kp_refs, vp_refs, o_ref = refs[:n_pg], refs[n_pg:2 * n_pg], refs[2 * n_pg]
    h = pl.program_id(1)
    scale = MOBA_HEAD_DIM ** -0.5
    rows = PAGE_SIZE * MOBA_HEADS
    q = q_ref[0]
    qb = q.astype(BF16)
    s_own = jnp.sum(q * kn_ref[0], axis=-1, keepdims=True) * scale
    mine = (lax.broadcasted_iota(jnp.int32, (1, rows), 1) % MOBA_HEADS) == h
    sc = [jnp.where(mine, _dot_nt(qb, kp[0, 0].reshape(rows, MOBA_HEAD_DIM).astype(BF16)) * scale, NEG_INF)
          for kp in kp_refs]
    m = s_own
    for x in sc:
        m = jnp.maximum(m, jnp.max(x, axis=-1, keepdims=True))
    p_own = jnp.exp(s_own - m)
    l = p_own
    acc = p_own * vn_ref[0]
    for x, vp in zip(sc, vp_refs):
        p = jnp.exp(x - m)
        l = l + jnp.sum(p, axis=-1, keepdims=True)
        acc = acc + _dot(p.astype(BF16), vp[0, 0].reshape(rows, MOBA_HEAD_DIM).astype(BF16))
    o_ref[0] = (acc / l).astype(o_ref.dtype)


def moba_decode(sel_flat, pt_flat, q3, k3, M3, cache_k, cache_v, *, layer, n_pages, n_sel, v_col):
    bs, _, W = q3.shape
    Dh = MOBA_HEAD_DIM
    n_pg = n_sel * PAGES_PER_BLOCK

    def page_spec(e):
        def page_map(b, h, sel, pt):
            blk = sel[(b * MOBA_HEADS + h) * n_sel + e // PAGES_PER_BLOCK]
            return (layer, pt[b * n_pages + blk * PAGES_PER_BLOCK + e % PAGES_PER_BLOCK], 0, 0, 0)
        return pl.BlockSpec((1, 1, PAGE_SIZE, MOBA_HEADS, Dh), page_map)

    vec = lambda col: pl.BlockSpec((1, 1, Dh), lambda b, h, sel, pt: (b, 0, col(h)))
    return pl.pallas_call(
        functools.partial(_moba_decode_kernel, n_pg=n_pg),
        out_shape=jax.ShapeDtypeStruct((bs, 1, W), F32),
        grid_spec=pltpu.PrefetchScalarGridSpec(
            num_scalar_prefetch=2,
            grid=(bs, MOBA_HEADS),
            in_specs=[vec(lambda h: h), vec(lambda h: h), vec(lambda h: v_col // Dh + h)]
            + [page_spec(e) for e in range(n_pg)] * 2,
            out_specs=vec(lambda h: h),
        ),
        compiler_params=_cparams("parallel", "arbitrary"),
        name="moba_decode",
    )(sel_flat, pt_flat, q3, k3, M3, *([cache_k] * n_pg), *([cache_v] * n_pg))


def _merge_kernel(ya_ref, yb_ref, yc_ref, wa_ref, wb_ref, wc_ref, z0_ref, z1_ref, z2_ref, o_ref):
    acc = _sigmoid(z0_ref[...]) * _dot(ya_ref[...].astype(BF16), wa_ref[...])
    acc = acc + _sigmoid(z1_ref[...]) * _dot(yb_ref[...].astype(BF16), wb_ref[...])
    acc = acc + _sigmoid(z2_ref[...]) * _dot(yc_ref[...].astype(BF16), wc_ref[...])
    o_ref[...] = acc.astype(o_ref.dtype)


def gated_merge(ya, yb, yc, wa, wb, wc, proj, layer, *, z_col, tn=PROJ_TILE, tm_pref=512):
    m = ya.shape[0]
    d = wa.shape[2]
    tm = _row_tile(m, tm_pref)
    nj = d // tn
    z0 = z_col // tn
    wspec = lambda w: pl.BlockSpec((None, w.shape[1], tn), lambda i, j: (layer, 0, j))
    return pl.pallas_call(
        _merge_kernel,
        out_shape=jax.ShapeDtypeStruct((m, d), BF16),
        grid=(m // tm, nj),
        in_specs=[
            pl.BlockSpec((tm, ya.shape[1]), lambda i, j: (i, 0)),
            pl.BlockSpec((tm, yb.shape[1]), lambda i, j: (i, 0)),
            pl.BlockSpec((tm, yc.shape[1]), lambda i, j: (i, 0)),
            wspec(wa), wspec(wb), wspec(wc),
            pl.BlockSpec((tm, tn), lambda i, j: (i, z0 + j)),
            pl.BlockSpec((tm, tn), lambda i, j: (i, z0 + nj + j)),
            pl.BlockSpec((tm, tn), lambda i, j: (i, z0 + 2 * nj + j)),
        ],
        out_specs=pl.BlockSpec((tm, tn), lambda i, j: (i, j)),
        compiler_params=_cparams("parallel", "arbitrary"),
        name="gated_merge",
    )(ya, yb, yc, wa, wb, wc, proj, proj, proj)


def _matmul_residual_kernel(a_ref, w_ref, x_ref, o_ref):
    o_ref[...] = x_ref[...] + _dot(a_ref[...], w_ref[...])


def matmul_residual(a, w, x, layer, *, tn, tm_pref=512):
    m, k = a.shape
    n = w.shape[2]
    tm = _row_tile(m, tm_pref)
    return pl.pallas_call(
        _matmul_residual_kernel,
        out_shape=jax.ShapeDtypeStruct((m, n), F32),
        grid=(m // tm, n // tn),
        in_specs=[
            pl.BlockSpec((tm, k), lambda i, j: (i, 0)),
            pl.BlockSpec((None, k, tn), lambda i, j: (layer, 0, j)),
            pl.BlockSpec((tm, tn), lambda i, j: (i, j)),
        ],
        out_specs=pl.BlockSpec((tm, tn), lambda i, j: (i, j)),
        compiler_params=_cparams("parallel", "arbitrary"),
        name="matmul_residual",
    )(a, w, x)


def _ffn_in_kernel(x_ref, g_ref, wg_ref, wv_ref, o_ref, h_ref):
    @pl.when(pl.program_id(1) == 0)
    def _():
        x = x_ref[...]
        ms = jnp.mean(x * x, axis=-1, keepdims=True)
        h_ref[...] = (x * lax.rsqrt(ms + NORM_EPS) * g_ref[...]).astype(BF16)

    h = h_ref[...]
    gate = _dot(h, wg_ref[...])
    val = _dot(h, wv_ref[...])
    o_ref[...] = (gate * _sigmoid(gate) * val).astype(o_ref.dtype)


def _ffn_in_key_means_kernel(pt_ref, x_ref, g_ref, wg_ref, wv_ref, *refs, n_pg, page_steps):
    del pt_ref
    page_refs, (o_ref, km_ref, h_ref) = refs[:n_pg], refs[n_pg:]
    _ffn_in_kernel(x_ref, g_ref, wg_ref, wv_ref, o_ref, h_ref)

    @pl.when(pl.program_id(0) * pl.num_programs(1) + pl.program_id(1) < page_steps)
    def _():
        for n in range(n_pg // PAGES_PER_BLOCK):
            tot = page_refs[PAGES_PER_BLOCK * n][0, 0].sum(axis=0)
            for e in range(1, PAGES_PER_BLOCK):
                tot = tot + page_refs[PAGES_PER_BLOCK * n + e][0, 0].sum(axis=0)
            tot = tot * (1.0 / MOBA_BLOCK)
            for hd in range(MOBA_HEADS):
                km_ref[0, hd, pl.ds(n, 1), :] = tot[hd:hd + 1, :]


def ffn_in(x, g, w, layer, *, tn=512, tm_pref=1024):
    m, d = x.shape
    hidden = w.shape[2] // 2
    tm = _row_tile(m, tm_pref)
    nj = hidden // tn
    return pl.pallas_call(
        _ffn_in_kernel,
        out_shape=jax.ShapeDtypeStruct((m, hidden), BF16),
        grid=(m // tm, nj),
        in_specs=[
            pl.BlockSpec((tm, d), lambda i, j: (i, 0)),
            pl.BlockSpec((1, d), lambda i, j: (0, 0)),
            pl.BlockSpec((None, d, tn), lambda i, j: (layer, 0, j)),
            pl.BlockSpec((None, d, tn), lambda i, j: (layer, 0, nj + j)),
        ],
        out_specs=pl.BlockSpec((tm, tn), lambda i, j: (i, j)),
        scratch_shapes=[pltpu.VMEM((tm, d), BF16)],
        compiler_params=_cparams("parallel", "arbitrary"),
        name="ffn_in_swiglu",
    )(x, g.reshape(1, d), w, w)


def ffn_in_key_means(x, g, w, layer, page_table, cache_k, *, tn=512, tm_pref=1024):
    m, d = x.shape
    hidden = w.shape[2] // 2
    tm = _row_tile(m, tm_pref)
    nj = hidden // tn
    bs, n_pages = page_table.shape
    pps = min(PAGES_PER_STEP, n_pages)
    per_seq = n_pages // pps
    page_steps = bs * per_seq
    assert n_pages % pps == 0 and page_steps <= (m // tm) * nj

    def page_pos(i, j):
        t = jnp.minimum(i * nj + j, page_steps - 1)
        return t // per_seq, t % per_seq

    def page_spec(e):
        def page_map(i, j, pt):
            b, s = page_pos(i, j)
            return (layer, pt[b * n_pages + s * pps + e], 0, 0, 0)
        return pl.BlockSpec((1, 1, PAGE_SIZE, MOBA_HEADS, MOBA_HEAD_DIM), page_map)

    def km_map(i, j, pt):
        b, s = page_pos(i, j)
        return (b, 0, s, 0)

    return pl.pallas_call(
        functools.partial(_ffn_in_key_means_kernel, n_pg=pps, page_steps=page_steps),
        out_shape=(
            jax.ShapeDtypeStruct((m, hidden), BF16),
            jax.ShapeDtypeStruct((bs, MOBA_HEADS, n_pages // PAGES_PER_BLOCK, MOBA_HEAD_DIM), F32),
        ),
        grid_spec=pltpu.PrefetchScalarGridSpec(
            num_scalar_prefetch=1,
            grid=(m // tm, nj),
            in_specs=[
                pl.BlockSpec((tm, d), lambda i, j, pt: (i, 0)),
                pl.BlockSpec((1, d), lambda i, j, pt: (0, 0)),
                pl.BlockSpec((None, d, tn), lambda i, j, pt: (layer, 0, j)),
                pl.BlockSpec((None, d, tn), lambda i, j, pt: (layer, 0, nj + j)),
            ] + [page_spec(e) for e in range(pps)],
            out_specs=(
                pl.BlockSpec((tm, tn), lambda i, j, pt: (i, j)),
                pl.BlockSpec((1, MOBA_HEADS, pps // PAGES_PER_BLOCK, MOBA_HEAD_DIM), km_map),
            ),
            scratch_shapes=[pltpu.VMEM((tm, d), BF16)],
        ),
        compiler_params=_cparams("arbitrary", "arbitrary"),
        name="ffn_in_swiglu_key_means",
    )(page_table.reshape(-1), x, g.reshape(1, d), w, w, *([cache_k] * pps))


def _rope_tables(pos):
    half = MOBA_HEAD_DIM // 2
    inv = ROPE_THETA ** (-jnp.arange(half, dtype=F32) / half)
    ang = pos.astype(F32)[:, None] * inv[None, :]
    cos, sin = jnp.cos(ang), jnp.sin(ang)
    return jnp.concatenate([cos, cos], axis=-1), jnp.concatenate([-sin, sin], axis=-1)


def _bf16_weights(w):
    o1 = RWKV_PROJ
    o2 = o1 + 3 * MOBA_WIDTH
    o3 = o2 + 2 * GLA_K_WIDTH + GLA_V_WIDTH
    o4 = o3 + GLA_GATE_RANK
    o5 = o4 + GLA_V_WIDTH
    w_in = w['w_in']
    total = COL_GATES + w_in.shape[-1] - o5

    def place(lo, hi, at):
        return jnp.pad(w_in[..., lo:hi], ((0, 0), (0, 0), (at, total - at - (hi - lo))))

    packed = (place(0, o1, COL_RWKV)
              + place(o2, o3, COL_GLA) + place(o4, o5, COL_GLA + o3 - o2)
              + place(o3, o4, COL_GLA + o3 - o2 + o5 - o4)
              + place(o1, o2, COL_MOBA) + place(o5, w_in.shape[-1], COL_GATES))
    out = {k: w[k].astype(BF16) for k in ('w_up_rwkv', 'w_up_moba', 'w_up_gla', 'w_out', 'w_ffn_in', 'w_ffn_out')}
    out['w_in'] = packed.astype(BF16)
    return out


def _mix_and_ffn(x, ya, yb, yc, proj, lw, wb, layer, paged_keys=None):
    merged = gated_merge(ya, yb, yc, wb['w_up_rwkv'], wb['w_up_moba'], wb['w_up_gla'], proj, layer,
                         z_col=COL_GATES)
    x = matmul_residual(merged, wb['w_out'], x, layer, tn=1024, tm_pref=1024)
    if paged_keys is None:
        act, kmean = ffn_in(x, lw['norm_ffn'], wb['w_ffn_in'], layer), None
    else:
        act, kmean = ffn_in_key_means(x, lw['norm_ffn'], wb['w_ffn_in'], layer, *paged_keys)
    return matmul_residual(act, wb['w_ffn_out'], x, layer, tn=512), kmean


def _prompt_layer(x, lw, wb, rope, layer, paged_keys, *, batch, seq):
    proj = norm_matmul(x, lw['norm_mix'], wb['w_in'], layer)

    tt = min(seq, 256)
    ya, r_S, shift = rwkv_branch(
        proj, jnp.zeros((batch, RWKV_PROJ), F32),
        jnp.zeros((batch, RWKV_HEADS, RWKV_HEAD_DIM, RWKV_HEAD_DIM), F32), lw,
        batch=batch, t_pad=seq, t_real=seq, tt=tt, chunk=64)
    yc, g_S = gla_branch(
        proj, jnp.zeros((batch, GLA_HEADS, GLA_KEY_DIM, GLA_VALUE_DIM), F32), lw,
        batch=batch, t_pad=seq, t_real=seq, tt=tt, chunk=64, sub=16, col_block=COL_GLA // PROJ_SLOT)

    nq = seq // MOBA_BLOCK
    q_rot, k_rot, v_out, kb, vt, kmean = moba_prep_prompt(
        proj, rope[0], rope[1], lw['moba_q_norm'], lw['moba_k_norm'],
        batch=batch, seq=seq, col0=COL_MOBA // MOBA_WIDTH)
    yb = moba_attention_prompt(q_rot, kb, vt, kmean.reshape(batch, nq, MOBA_WIDTH), batch=batch, seq=seq)

    x, kmean_past = _mix_and_ffn(x, ya, yb, yc, proj, lw, wb, layer, paged_keys)
    k_new = k_rot.reshape(batch, seq, MOBA_HEADS, MOBA_HEAD_DIM)
    v_new = v_out.reshape(batch, seq, MOBA_HEADS, MOBA_HEAD_DIM)
    return (x, k_new, v_new, r_S, shift, g_S), kmean_past


def _sample_layer(x, lw, wb, rope, shift0, rwkv_S0, gla_S0, kmean_past, pt_flat, cache_k, cache_v,
                  *, layer, n_pages):
    bs = x.shape[0]
    proj = norm_matmul(x, lw['norm_mix'], wb['w_in'], layer)

    pad_rows = lambda a: jnp.pad(a[:, None, :], ((0, 0), (0, SUBLANES - 1), (0, 0))).reshape(bs * SUBLANES, -1)
    ya, r_S, shift = rwkv_branch(pad_rows(proj[:, COL_RWKV:COL_RWKV + RWKV_PROJ]), shift0, rwkv_S0, lw,
                                 batch=bs, t_pad=SUBLANES, t_real=1, tt=SUBLANES, chunk=SUBLANES)
    yc, g_S = gla_branch(pad_rows(proj[:, COL_GLA:COL_GLA + PROJ_SLOT]), gla_S0, lw,
                         batch=bs, t_pad=SUBLANES, t_real=1, tt=SUBLANES, chunk=SUBLANES, sub=SUBLANES)
    ya = ya[::SUBLANES]
    yc = yc[::SUBLANES]

    q_rot, k_rot = moba_prep_sample(proj, rope[0], rope[1], lw['moba_q_norm'], lw['moba_k_norm'],
                                    col0=COL_MOBA // MOBA_WIDTH)
    q3 = q_rot.reshape(bs, 1, MOBA_WIDTH)
    n_sel = min(MOBA_TOPK, kmean_past.shape[2])
    sel = moba_select_sample(q3, kmean_past)[:, :, :n_sel].reshape(-1)
    v_col = COL_MOBA + 2 * MOBA_WIDTH
    yb = moba_decode(sel, pt_flat, q3, k_rot.reshape(bs, 1, MOBA_WIDTH), proj.reshape(bs, 1, -1),
                     cache_k, cache_v, layer=layer, n_pages=n_pages, n_sel=n_sel, v_col=v_col).reshape(bs, MOBA_WIDTH)

    x, _ = _mix_and_ffn(x, ya, yb, yc, proj, lw, wb, layer)
    k_new = k_rot.reshape(bs, 1, MOBA_HEADS, MOBA_HEAD_DIM)
    v_new = proj[:, v_col:COL_GATES].reshape(bs, 1, MOBA_HEADS, MOBA_HEAD_DIM)
    return x, k_new, v_new, r_S, shift, g_S


_LAYER_KEYS = ('norm_mix', 'w_in', 'rwkv_mu', 'rwkv_w0', 'rwkv_w_up', 'rwkv_a0', 'rwkv_a_up', 'rwkv_g_up',
               'rwkv_k_k', 'rwkv_k_a', 'rwkv_r_k', 'rwkv_ln_w', 'rwkv_ln_b', 'moba_q_norm', 'moba_k_norm',
               'gla_a_up', 'gla_a_bias', 'gla_o_norm', 'w_up_rwkv', 'w_up_moba', 'w_up_gla', 'w_out',
               'norm_ffn', 'w_ffn_in', 'w_ffn_out')


def kernel(x_prompt, x_sample, cache_k, cache_v, page_table, state_rwkv, state_rwkv_shift, state_gla, norm_mix, w_in, rwkv_mu, rwkv_w0, rwkv_w_up, rwkv_a0, rwkv_a_up, rwkv_g_up, rwkv_k_k, rwkv_k_a, rwkv_r_k, rwkv_ln_w, rwkv_ln_b, moba_q_norm, moba_k_norm, gla_a_up, gla_a_bias, gla_o_norm, w_up_rwkv, w_up_moba, w_up_gla, w_out, norm_ffn, w_ffn_in, w_ffn_out):
    stacked = dict(zip(_LAYER_KEYS, (
        norm_mix, w_in, rwkv_mu, rwkv_w0, rwkv_w_up, rwkv_a0, rwkv_a_up, rwkv_g_up, rwkv_k_k, rwkv_k_a,
        rwkv_r_k, rwkv_ln_w, rwkv_ln_b, moba_q_norm, moba_k_norm, gla_a_up, gla_a_bias, gla_o_norm,
        w_up_rwkv, w_up_moba, w_up_gla, w_out, norm_ffn, w_ffn_in, w_ffn_out)))
    depth = w_in.shape[0]
    bp, seq, d = x_prompt.shape
    bs, dec_seq, _ = x_sample.shape
    n_pages = page_table.shape[1]
    past_len = n_pages * cache_k.shape[2]
    assert dec_seq == 1 and cache_k.shape[2:] == (PAGE_SIZE, MOBA_HEADS, MOBA_HEAD_DIM)
    assert seq % MOBA_BLOCK == 0 and past_len % MOBA_BLOCK == 0 and past_len // MOBA_BLOCK >= MOBA_TOPK

    pt_flat = page_table.reshape(-1)

    rope_p = _rope_tables(jnp.arange(seq))
    rope_s = _rope_tables(jnp.full((bs,), past_len))

    yp = x_prompt.reshape(bp * seq, d)
    ys = x_sample.reshape(bs, d)
    wb = _bf16_weights(stacked)
    small = [k for k in _LAYER_KEYS if k not in wb]
    outs_p, outs_s = [], []
    for l in range(depth):
        lw = {k: stacked[k][l] for k in small}
        (yp, *rest_p), kmean_past = _prompt_layer(yp, lw, wb, rope_p, l, (page_table, cache_k), batch=bp, seq=seq)
        outs_p.append(rest_p)
        ys, *rest_s = _sample_layer(ys, lw, wb, rope_s, state_rwkv_shift[l], state_rwkv[l], state_gla[l],
                                    kmean_past, pt_flat, cache_k, cache_v, layer=l, n_pages=n_pages)
        outs_s.append(rest_s)
    stack = lambda outs, i: jnp.stack([o[i] for o in outs])
    return (yp.reshape(bp, seq, d), ys.reshape(bs, 1, d),
            stack(outs_p, 0), stack(outs_p, 1), stack(outs_p, 2), stack(outs_p, 3), stack(outs_p, 4),
            stack(outs_s, 0), stack(outs_s, 1), stack(outs_s, 2), stack(outs_s, 3), stack(outs_s, 4))
```

```python
import functools

import jax
import jax.numpy as jnp
from jax import lax
from jax.experimental import pallas as pl
from jax.experimental.pallas import tpu as pltpu

F32 = jnp.float32
BF16 = jnp.bfloat16
HIGHEST = lax.Precision.HIGHEST

PAGE_SIZE = 128
RWKV_HEADS = 8
RWKV_HEAD_DIM = 64
RWKV_WIDTH = RWKV_HEADS * RWKV_HEAD_DIM
RWKV_DECAY_RANK = 64
RWKV_ICLR_RANK = 64
RWKV_GATE_RANK = 128
RWKV_PROJ = 3 * RWKV_WIDTH + RWKV_DECAY_RANK + RWKV_ICLR_RANK + RWKV_GATE_RANK
RWKV_GN_EPS = 64e-5
MOBA_HEADS = 8
MOBA_HEAD_DIM = 128
MOBA_WIDTH = MOBA_HEADS * MOBA_HEAD_DIM
MOBA_BLOCK = 256
MOBA_TOPK = 3
ROPE_THETA = 10000.0
GLA_HEADS = 4
GLA_KEY_DIM = 64
GLA_VALUE_DIM = 128
GLA_K_WIDTH = GLA_HEADS * GLA_KEY_DIM
GLA_V_WIDTH = GLA_HEADS * GLA_VALUE_DIM
GLA_GATE_RANK = 16
GLA_GATE_PAD = 256
GLA_SECT = 2 * GLA_K_WIDTH + 2 * GLA_V_WIDTH + GLA_GATE_PAD
GLA_TAU = 16.0
PROJ_SLOT = 2048
PROJ_TILE = 1024
COL_RWKV = 0
COL_GLA = PROJ_SLOT
COL_MOBA = 2 * PROJ_SLOT
COL_GATES = COL_MOBA + 3 * MOBA_WIDTH
N_BRANCHES = 3
NORM_EPS = 1e-6
NEG_INF = -1e30
LOG2_E = 1.4426950408889634
VT_ROWS = MOBA_HEAD_DIM + 16

VMEM_LIMIT_BYTES = 56 * 1024 * 1024
SUBLANES = 8

GATE_PASSES = 3
STATE_PASSES = 3
RWKV_CHUNK = 128
RWKV_CHUNK_PASSES = 1
GLA_CHUNK_PASSES = 1


def _cparams(*sem):
    return pltpu.CompilerParams(dimension_semantics=sem, vmem_limit_bytes=VMEM_LIMIT_BYTES)


def _dot(a, b, precision=None):
    return jnp.dot(a, b, preferred_element_type=F32, precision=precision)


def _dot_nt(a, b, precision=None):
    return lax.dot_general(a, b, (((1,), (1,)), ((), ())), preferred_element_type=F32, precision=precision)


def _dot_tn(a, b, precision=None):
    return lax.dot_general(a, b, (((0,), (0,)), ((), ())), preferred_element_type=F32, precision=precision)


def _mm(a, b, kind, passes):
    f = {'nn': _dot, 'nt': _dot_nt, 'tn': _dot_tn}[kind]
    if passes == 6:
        return f(a, b, HIGHEST)
    ah = a.astype(BF16)
    bh = b.astype(BF16)
    if passes == 1:
        return f(ah, bh)
    al = (a - ah.astype(F32)).astype(BF16)
    bl = (b - bh.astype(F32)).astype(BF16)
    return f(ah, bh) + f(ah, bl) + f(al, bh)


def _split3(x):
    hi = x.astype(BF16)
    r1 = x - hi.astype(F32)
    mid = r1.astype(BF16)
    lo = (r1 - mid.astype(F32)).astype(BF16)
    return hi, mid, lo


def _tril_ones(n, strict=False):
    r = lax.broadcasted_iota(jnp.int32, (n, n), 0)
    c = lax.broadcasted_iota(jnp.int32, (n, n), 1)
    return (r > c) if strict else (r >= c)


def _cumsum_rows(x, tri_bf16):
    hi, mid, lo = _split3(x)
    return _dot(tri_bf16, hi) + _dot(tri_bf16, mid) + _dot(tri_bf16, lo)


def _sigmoid(x):
    return 1.0 / (1.0 + jnp.exp(-x))


def _softplus(x):
    return jnp.maximum(x, 0.0) + jnp.log(1.0 + jnp.exp(-jnp.abs(x)))


def _row_tile(m, pref):
    return pref if m % pref == 0 else m


def _act_dtype(tile_rows):
    return BF16 if tile_rows % (2 * SUBLANES) == 0 else F32


def _norm_matmul_kernel(x_ref, g_ref, w_ref, o_ref, h_ref):
    @pl.when(pl.program_id(1) == 0)
    def _():
        x = x_ref[...]
        ms = jnp.mean(x * x, axis=-1, keepdims=True)
        h_ref[...] = (x * lax.rsqrt(ms + NORM_EPS) * g_ref[...]).astype(BF16)

    o_ref[...] = _dot(h_ref[...], w_ref[...]).astype(o_ref.dtype)


def norm_matmul(x, g, w, layer, *, tn=1024, tm_pref=1024):
    m, d = x.shape
    n = w.shape[2]
    tm = _row_tile(m, tm_pref)
    return pl.pallas_call(
        _norm_matmul_kernel,
        out_shape=jax.ShapeDtypeStruct((m, n), F32),
        grid=(m // tm, n // tn),
        in_specs=[
            pl.BlockSpec((tm, d), lambda i, j: (i, 0)),
            pl.BlockSpec((1, d), lambda i, j: (0, 0)),
            pl.BlockSpec((None, d, tn), lambda i, j: (layer, 0, j)),
        ],
        out_specs=pl.BlockSpec((tm, tn), lambda i, j: (i, j)),
        scratch_shapes=[pltpu.VMEM((tm, d), BF16)],
        compiler_params=_cparams("parallel", "arbitrary"),
        name="norm_matmul",
    )(x, g.reshape(1, d), w)


def _rwkv_kernel(p_ref, shift0_ref, s0_ref, mu_ref, w0_ref, wup_ref, a0_ref, aup_ref, gup_ref,
                 kk_ref, ka_ref, rk_ref, lnw_ref, lnb_ref,
                 y_ref, sout_ref, shout_ref,
                 carry_ref, s_ref, r_s, k_s, v_s, kk_s, b_s, lw_s, y_s,
                 *, t_real, tt, chunk):
    t = pl.program_id(1)
    nt = pl.num_programs(1)
    W = RWKV_WIDTH
    N = RWKV_HEAD_DIM
    H = RWKV_HEADS
    C = chunk
    cp = RWKV_CHUNK_PASSES

    @pl.when(t == 0)
    def _():
        carry_ref[...] = shift0_ref[0]
        s_ref[...] = s0_ref[0]

    P = p_ref[...]
    rowid = lax.broadcasted_iota(jnp.int32, (tt, 1), 0)
    prev = jnp.where(rowid == 0, carry_ref[...], pltpu.roll(P, 1, axis=0))
    last_row = (t_real - 1) % tt
    carry_ref[...] = P[last_row:last_row + 1, :]

    @pl.when(t == nt - 1)
    def _():
        shout_ref[0] = P[last_row:last_row + 1, :]

    Pm = P + (prev - P) * mu_ref[...]
    r = Pm[:, 0:W]
    k = Pm[:, W:2 * W]
    v = Pm[:, 2 * W:3 * W]
    o = 3 * W
    wd = Pm[:, o:o + RWKV_DECAY_RANK]
    ad = Pm[:, o + RWKV_DECAY_RANK:o + RWKV_DECAY_RANK + RWKV_ICLR_RANK]
    gd = Pm[:, o + RWKV_DECAY_RANK + RWKV_ICLR_RANK:]
    w_val = -_softplus(-(w0_ref[...] + _mm(jnp.tanh(wd), wup_ref[...], 'nn', GATE_PASSES))) - 0.5
    lw = -jnp.exp(w_val)
    a = _sigmoid(a0_ref[...] + _mm(ad, aup_ref[...], 'nn', GATE_PASSES))
    g = _mm(_sigmoid(gd), gup_ref[...], 'nn', GATE_PASSES)
    kk = k * kk_ref[...]
    kmod = k * (1.0 + (a - 1.0) * ka_ref[...])
    if t_real % tt != 0:
        valid = (t * tt + rowid) < t_real
        lw = jnp.where(valid, lw, 0.0)
        kk = jnp.where(valid, kk, 0.0)
        kmod = jnp.where(valid, kmod, 0.0)
        v = jnp.where(valid, v, 0.0)
    seg_r = lax.broadcasted_iota(jnp.int32, (W, W), 0) // N
    seg_c = lax.broadcasted_iota(jnp.int32, (W, W), 1) // N
    seg = jnp.where(seg_r == seg_c, 1.0, 0.0).astype(BF16)
    hi, md, lo = _split3(kk * kk)
    ss = _dot(hi, seg) + _dot(md, seg) + _dot(lo, seg)
    kkn = kk / jnp.maximum(jnp.sqrt(ss), 1e-12)
    r_s[...] = r
    k_s[...] = kmod
    v_s[...] = v
    kk_s[...] = kkn
    b_s[...] = kkn * a
    lw_s[...] = lw

    tri = jnp.where(_tril_ones(C), 1.0, 0.0).astype(BF16)
    low_incl = _tril_ones(C)
    low_strict = _tril_ones(C, strict=True)
    eye = lax.broadcasted_iota(jnp.int32, (N, N), 0) == lax.broadcasted_iota(jnp.int32, (N, N), 1)
    mid = max(C // 2 - 1, 0)
    heads = range(H)
    hs = [slice(h * N, (h + 1) * N) for h in heads]

    def chunk_body(c, carry):
        rows = pl.ds(pl.multiple_of(c * C, C), C)
        lw_c = lw_s[rows, :]
        cum = _cumsum_rows(lw_c, tri)
        cume = cum - lw_c
        rho = cum[mid:mid + 1, :]
        cum_last = cum[C - 1:C, :]
        r_c = r_s[rows, :]
        k_c = k_s[rows, :]
        v_c = v_s[rows, :]
        kk_c = kk_s[rows, :]
        b_c = b_s[rows, :]
        e_out = jnp.exp(rho - cum)
        e_end = jnp.exp(cum_last - cum)
        kk_hat = kk_c * jnp.exp(cume - rho)
        r_hat = r_c * jnp.exp(cum - rho)
        k_til = k_c * e_out
        b_til = b_c * e_out
        kk_abs = kk_c * jnp.exp(cume)
        r_abs = r_c * jnp.exp(cum)
        k_end = k_c * e_end
        b_end = b_c * e_end
        gam = jnp.exp(cum_last)

        A = [_mm(jnp.concatenate([kk_hat[:, s], r_hat[:, s]], axis=0),
                 jnp.concatenate([k_til[:, s], b_til[:, s]], axis=0), 'nt', cp) for s in hs]
        Lk = [jnp.where(low_strict, x[0:C, 0:C], 0.0) for x in A]
        Lb = [jnp.where(low_strict, x[0:C, C:2 * C], 0.0) for x in A]
        Ark = [jnp.where(low_incl, x[C:2 * C, 0:C], 0.0) for x in A]
        Arb = [jnp.where(low_incl, x[C:2 * C, C:2 * C], 0.0) for x in A]
        X = [jnp.concatenate([kk_abs[:, s], _mm(Lk[h], v_c[:, s], 'nn', cp)], axis=1) for h, s in zip(heads, hs)]
        X = [X[h] - _mm(Lb[h], X[h], 'nn', cp) for h in heads]
        Lp = Lb
        p = 2
        while p < C:
            Lp = [_mm(x, x, 'nn', cp) for x in Lp]
            X = [X[h] + _mm(Lp[h], X[h], 'nn', cp) for h in heads]
            p *= 2
        RY = [jnp.concatenate([r_abs[:, s], _mm(Ark[h], v_c[:, s], 'nn', cp)], axis=1)
              - _mm(Arb[h], X[h], 'nn', cp) for h, s in zip(heads, hs)]
        MN = [_mm(X[h], b_end[:, s], 'tn', cp) for h, s in zip(heads, hs)]
        VK = [_mm(v_c[:, s], k_end[:, s], 'tn', cp) for s in hs]
        for h, s in zip(heads, hs):
            S = s_ref[h]
            Y = _mm(RY[h][:, 0:N], S, 'nt', STATE_PASSES) + RY[h][:, N:2 * N]
            trans = jnp.where(eye, jnp.broadcast_to(gam[:, s], (N, N)), 0.0) - MN[h][0:N]
            s_ref[h] = _mm(S, trans, 'nn', STATE_PASSES) + (VK[h] - MN[h][N:2 * N])
            mu = jnp.mean(Y, axis=-1, keepdims=True)
            var = jnp.mean(jnp.square(Y - mu), axis=-1, keepdims=True)
            yn = (Y - mu) * lax.rsqrt(var + RWKV_GN_EPS) * lnw_ref[:, s] + lnb_ref[:, s]
            bonus = jnp.sum(r_c[:, s] * k_c[:, s] * rk_ref[:, s], axis=-1, keepdims=True) * v_c[:, s]
            y_s[rows, s] = yn + bonus
        return carry

    lax.fori_loop(0, tt // C, chunk_body, 0)
    y_ref[...] = (y_s[...] * g).astype(y_ref.dtype)

    @pl.when(t == nt - 1)
    def _():
        sout_ref[0] = s_ref[...]


def rwkv_branch(P, shift0, S0, lw, *, batch, t_pad, t_real, tt, chunk):
    nt = t_pad // tt
    W = RWKV_WIDTH
    vec = lambda a: a.reshape(1, -1)
    full = lambda shape: pl.BlockSpec(shape, lambda b, t: (0,) * len(shape))
    kern = functools.partial(_rwkv_kernel, t_real=t_real, tt=tt, chunk=chunk)
    y, s_out, sh_out = pl.pallas_call(
        kern,
        out_shape=(
            jax.ShapeDtypeStruct((batch * t_pad, W), _act_dtype(tt)),
            jax.ShapeDtypeStruct((batch, RWKV_HEADS, RWKV_HEAD_DIM, RWKV_HEAD_DIM), F32),
            jax.ShapeDtypeStruct((batch, 1, RWKV_PROJ), F32),
        ),
        grid=(batch, nt),
        in_specs=[
            pl.BlockSpec((tt, RWKV_PROJ), lambda b, t: (b * nt + t, 0)),
            pl.BlockSpec((1, 1, RWKV_PROJ), lambda b, t: (b, 0, 0)),
            pl.BlockSpec((1, RWKV_HEADS, RWKV_HEAD_DIM, RWKV_HEAD_DIM), lambda b, t: (b, 0, 0, 0)),
            full((1, RWKV_PROJ)),
            full((1, W)),
            full((RWKV_DECAY_RANK, W)),
            full((1, W)),
            full((RWKV_ICLR_RANK, W)),
            full((RWKV_GATE_RANK, W)),
            full((1, W)), full((1, W)), full((1, W)), full((1, W)), full((1, W)),
        ],
        out_specs=(
            pl.BlockSpec((tt, W), lambda b, t: (b * nt + t, 0)),
            pl.BlockSpec((1, RWKV_HEADS, RWKV_HEAD_DIM, RWKV_HEAD_DIM), lambda b, t: (b, 0, 0, 0)),
            pl.BlockSpec((1, 1, RWKV_PROJ), lambda b, t: (b, 0, 0)),
        ),
        scratch_shapes=[
            pltpu.VMEM((1, RWKV_PROJ), F32),
            pltpu.VMEM((RWKV_HEADS, RWKV_HEAD_DIM, RWKV_HEAD_DIM), F32),
        ] + [pltpu.VMEM((tt, W), F32) for _ in range(7)],
        compiler_params=_cparams("parallel", "arbitrary"),
        name="rwkv7_branch",
    )(P, shift0.reshape(batch, 1, RWKV_PROJ), S0,
      vec(lw['rwkv_mu']), vec(lw['rwkv_w0']), lw['rwkv_w_up'], vec(lw['rwkv_a0']), lw['rwkv_a_up'],
      lw['rwkv_g_up'], vec(lw['rwkv_k_k']), vec(lw['rwkv_k_a']), vec(lw['rwkv_r_k']),
      vec(lw['rwkv_ln_w']), vec(lw['rwkv_ln_b']))
    return y, s_out, sh_out.reshape(batch, RWKV_PROJ)


def _gla_kernel(g_ref, s0_ref, aup_ref, abias_ref, onorm_ref,
                y_ref, sout_ref,
                s_ref, la_s,
                *, t_real, tt, chunk, sub):
    t = pl.program_id(1)
    nt = pl.num_programs(1)
    KW, VW = GLA_K_WIDTH, GLA_V_WIDTH
    dk, dv = GLA_KEY_DIM, GLA_VALUE_DIM
    C = chunk
    cp = GLA_CHUNK_PASSES

    @pl.when(t == 0)
    def _():
        s_ref[...] = s0_ref[0]

    gl = g_ref[:, 2 * KW + 2 * VW:GLA_SECT]
    x = _mm(gl, aup_ref[...], 'nn', GATE_PASSES) + abias_ref[...]
    la = -_softplus(-x) * (1.0 / GLA_TAU)
    padded = t_real % tt != 0
    if padded:
        rowid = lax.broadcasted_iota(jnp.int32, (tt, 1), 0)
        valid = (t * tt + rowid) < t_real
        la = jnp.where(valid, la, 0.0)
    la_s[...] = la

    tri = jnp.where(_tril_ones(C), 1.0, 0.0).astype(BF16)
    ones_cv = jnp.ones((C, dv), BF16)
    low_incl = _tril_ones(C)
    crow = lax.broadcasted_iota(jnp.int32, (C, 1), 0)
    nsub = C // sub
    heads = range(GLA_HEADS)

    def chunk_body(c, carry):
        rows = pl.ds(pl.multiple_of(c * C, C), C)
        la_c = la_s[rows, :]
        cum_all = _cumsum_rows(la_c, tri)
        if padded:
            vmask = (t * tt + c * C + crow) < t_real
        q, k, v, cums = [], [], [], []
        for h in heads:
            q.append(g_ref[rows, h * dk:(h + 1) * dk] * (dk ** -0.5))
            k_h = g_ref[rows, KW + h * dk:KW + (h + 1) * dk]
            v_h = g_ref[rows, 2 * KW + h * dv:2 * KW + (h + 1) * dv]
            if padded:
                k_h = jnp.where(vmask, k_h, 0.0)
                v_h = jnp.where(vmask, v_h, 0.0)
            k.append(k_h)
            v.append(v_h)
            cums.append(cum_all[:, h * dk:(h + 1) * dk])
        bounds = [[jnp.zeros((1, dk), F32)] + [cm[i * sub - 1:i * sub, :] for i in range(1, nsub)] for cm in cums]
        qt = [q[h] * jnp.exp(cums[h] - jnp.concatenate(
            [jnp.broadcast_to(b, (sub, dk)) for b in bounds[h]], axis=0)) for h in heads]
        att = []
        for h in heads:
            att_rows = []
            for i in range(nsub):
                kt = k[h] * jnp.exp(jnp.where(crow < (i + 1) * sub, bounds[h][i] - cums[h], NEG_INF))
                att_rows.append(_mm(qt[h][i * sub:(i + 1) * sub], kt, 'nt', cp))
            att.append(jnp.where(low_incl, jnp.concatenate(att_rows, axis=0), 0.0))
        intra = [_mm(att[h], v[h], 'nn', cp) for h in heads]
        kv = [_mm(k[h] * jnp.exp(cums[h][C - 1:C, :] - cums[h]), v[h], 'tn', STATE_PASSES) for h in heads]
        tot = []
        for h in heads:
            hi, md, lo = _split3(la_c[:, h * dk:(h + 1) * dk])
            tot.append(_dot_tn(hi, ones_cv) + _dot_tn(md, ones_cv) + _dot_tn(lo, ones_cv))
        for h in heads:
            S = s_ref[h]
            o = intra[h] + _mm(q[h] * jnp.exp(cums[h]), S, 'nn', STATE_PASSES)
            s_ref[h] = S * jnp.exp(tot[h]) + kv[h]
            r_h = g_ref[rows, 2 * KW + VW + h * dv:2 * KW + VW + (h + 1) * dv]
            ms = jnp.mean(o * o, axis=-1, keepdims=True)
            on = o * lax.rsqrt(ms + NORM_EPS) * onorm_ref[...]
            y_ref[rows, h * dv:(h + 1) * dv] = (on * (r_h * _sigmoid(r_h))).astype(y_ref.dtype)
        return carry

    lax.fori_loop(0, tt // C, chunk_body, 0)

    @pl.when(t == nt - 1)
    def _():
        sout_ref[0] = s_ref[...]


def gla_branch(G, S0, lw, *, batch, t_pad, t_real, tt, chunk, sub, col_block=0):
    nt = t_pad // tt
    full = lambda shape: pl.BlockSpec(shape, lambda b, t: (0,) * len(shape))
    aup = jnp.zeros((GLA_GATE_PAD, GLA_K_WIDTH), F32).at[:GLA_GATE_RANK].set(lw['gla_a_up'])
    kern = functools.partial(_gla_kernel, t_real=t_real, tt=tt, chunk=chunk, sub=sub)
    y, s_out = pl.pallas_call(
        kern,
        out_shape=(
            jax.ShapeDtypeStruct((batch * t_pad, GLA_V_WIDTH), _act_dtype(tt)),
            jax.ShapeDtypeStruct((batch, GLA_HEADS, GLA_KEY_DIM, GLA_VALUE_DIM), F32),
        ),
        grid=(batch, nt),
        in_specs=[
            pl.BlockSpec((tt, PROJ_SLOT), lambda b, t: (b * nt + t, col_block)),
            pl.BlockSpec((1, GLA_HEADS, GLA_KEY_DIM, GLA_VALUE_DIM), lambda b, t: (b, 0, 0, 0)),
            full((GLA_GATE_PAD, GLA_K_WIDTH)),
            full((1, GLA_K_WIDTH)),
            full((1, GLA_VALUE_DIM)),
        ],
        out_specs=(
            pl.BlockSpec((tt, GLA_V_WIDTH), lambda b, t: (b * nt + t, 0)),
            pl.BlockSpec((1, GLA_HEADS, GLA_KEY_DIM, GLA_VALUE_DIM), lambda b, t: (b, 0, 0, 0)),
        ),
        scratch_shapes=[
            pltpu.VMEM((GLA_HEADS, GLA_KEY_DIM, GLA_VALUE_DIM), F32),
            pltpu.VMEM((tt, GLA_K_WIDTH), F32),
        ],
        compiler_params=_cparams("parallel", "arbitrary"),
        name="gla_branch",
    )(G, S0, aup, lw['gla_a_bias'].reshape(1, -1), lw['gla_o_norm'].reshape(1, -1))
    return y, s_out


def _norm_rope(x, g, cos, sin):
    ms = jnp.mean(x * x, axis=-1, keepdims=True)
    y = x * lax.rsqrt(ms + NORM_EPS) * g
    return y * cos + pltpu.roll(y, MOBA_HEAD_DIM // 2, axis=1) * sin


def _moba_prep_prompt_kernel(q_ref, k_ref, v_ref, cos_ref, sin_ref, gq_ref, gk_ref,
                             qo_ref, ko_ref, vo_ref, kb_ref, vt_ref, km_ref):
    Dh = MOBA_HEAD_DIM
    cos = cos_ref[...]
    sin = sin_ref[...]
    rows = q_ref.shape[0]
    vo_ref[...] = v_ref[...]
    for h in range(MOBA_HEADS):
        sl = slice(h * Dh, (h + 1) * Dh)
        qo_ref[:, sl] = _norm_rope(q_ref[:, sl], gq_ref[...], cos, sin)
        kr = _norm_rope(k_ref[:, sl], gk_ref[...], cos, sin)
        ko_ref[:, sl] = kr
        kb_ref[:, sl] = kr.astype(BF16)
        km_ref[0, :, sl] = jnp.sum(kr, axis=0, keepdims=True) * (1.0 / rows)
        vt_ref[0, h, 0, 0:Dh, :] = v_ref[:, sl].T.astype(BF16)
        vt_ref[0, h, 0, Dh:VT_ROWS, :] = jnp.ones((VT_ROWS - Dh, rows), BF16)


def moba_prep_prompt(M, cos, sin, gq, gk, *, batch, seq, col0=0):
    m = M.shape[0]
    W = MOBA_WIDTH
    blk = MOBA_BLOCK
    Dh = MOBA_HEAD_DIM
    nq = seq // blk
    nblk = m // blk
    return pl.pallas_call(
        _moba_prep_prompt_kernel,
        out_shape=(
            jax.ShapeDtypeStruct((m, W), F32),
            jax.ShapeDtypeStruct((m, W), F32),
            jax.ShapeDtypeStruct((m, W), F32),
            jax.ShapeDtypeStruct((m, W), BF16),
            jax.ShapeDtypeStruct((batch, MOBA_HEADS, nq, VT_ROWS, blk), BF16),
            jax.ShapeDtypeStruct((nblk, 1, W), F32),
        ),
        grid=(nblk,),
        in_specs=[
            pl.BlockSpec((blk, W), lambda i: (i, col0)),
            pl.BlockSpec((blk, W), lambda i: (i, col0 + 1)),
            pl.BlockSpec((blk, W), lambda i: (i, col0 + 2)),
            pl.BlockSpec((blk, Dh), lambda i: (i % nq, 0)),
            pl.BlockSpec((blk, Dh), lambda i: (i % nq, 0)),
            pl.BlockSpec((1, Dh), lambda i: (0, 0)),
            pl.BlockSpec((1, Dh), lambda i: (0, 0)),
        ],
        out_specs=(
            pl.BlockSpec((blk, W), lambda i: (i, 0)),
            pl.BlockSpec((blk, W), lambda i: (i, 0)),
            pl.BlockSpec((blk, W), lambda i: (i, 0)),
            pl.BlockSpec((blk, W), lambda i: (i, 0)),
            pl.BlockSpec((1, MOBA_HEADS, 1, VT_ROWS, blk), lambda i: (i // nq, 0, i % nq, 0, 0)),
            pl.BlockSpec((1, 1, W), lambda i: (i, 0, 0)),
        ),
        compiler_params=_cparams("parallel"),
        name="moba_qkv_prep",
    )(M, M, M, cos, sin, gq.reshape(1, -1), gk.reshape(1, -1))


def _moba_prep_sample_kernel(q_ref, k_ref, cos_ref, sin_ref, gq_ref, gk_ref, qo_ref, ko_ref):
    Dh = MOBA_HEAD_DIM
    cos = cos_ref[...]
    sin = sin_ref[...]
    for h in range(MOBA_HEADS):
        sl = slice(h * Dh, (h + 1) * Dh)
        qo_ref[:, sl] = _norm_rope(q_ref[:, sl], gq_ref[...], cos, sin)
        ko_ref[:, sl] = _norm_rope(k_ref[:, sl], gk_ref[...], cos, sin)


def moba_prep_sample(M, cos, sin, gq, gk, *, col0=0):
    m = M.shape[0]
    W = MOBA_WIDTH
    Dh = MOBA_HEAD_DIM
    return pl.pallas_call(
        _moba_prep_sample_kernel,
        out_shape=(jax.ShapeDtypeStruct((m, W), F32), jax.ShapeDtypeStruct((m, W), F32)),
        grid=(1,),
        in_specs=[
            pl.BlockSpec((m, W), lambda i: (0, col0)),
            pl.BlockSpec((m, W), lambda i: (0, col0 + 1)),
            pl.BlockSpec((m, Dh), lambda i: (0, 0)),
            pl.BlockSpec((m, Dh), lambda i: (0, 0)),
            pl.BlockSpec((1, Dh), lambda i: (0, 0)),
            pl.BlockSpec((1, Dh), lambda i: (0, 0)),
        ],
        out_specs=(pl.BlockSpec((m, W), lambda i: (0, 0)), pl.BlockSpec((m, W), lambda i: (0, 0))),
        compiler_params=_cparams("arbitrary"),
        name="moba_qk_prep_sample",
    )(M, M, cos, sin, gq.reshape(1, -1), gk.reshape(1, -1))


def _block_rank(bs, n_valid, axis):
    nb = bs.shape[axis]
    idx = lax.broadcasted_iota(jnp.int32, bs.shape, axis)
    rank = jnp.zeros(bs.shape, jnp.int32)
    for mm in range(nb):
        one = bs[mm:mm + 1, :] if axis == 0 else bs[:, mm:mm + 1]
        beats = (one > bs) | ((one == bs) & (idx > mm))
        if n_valid is not None:
            beats = beats & (mm < n_valid)
        rank = rank + jnp.where(beats, 1, 0)
    return rank


def _moba_attn_kernel(q_ref, kb_ref, vt_ref, km_ref, o_ref, sel_ref, s_ref):
    i = pl.program_id(2)
    blk = MOBA_BLOCK
    scale = MOBA_HEAD_DIM ** -0.5
    qT = q_ref[...].T
    km = km_ref[0]
    nb = km.shape[0]
    grp = next(g for g in (4, 2, 1) if nb % g == 0)
    bsT = _dot(km, qT, HIGHEST)
    blk_id = lax.broadcasted_iota(jnp.int32, (nb, blk), 0)
    rank = _block_rank(bsT, i, 0)
    sel_ref[...] = jnp.where((blk_id < i) & (rank < MOBA_TOPK), 1.0, 0.0)
    qb = qT.astype(BF16)
    causal = (lax.broadcasted_iota(jnp.int32, (blk, blk), 0)
              <= lax.broadcasted_iota(jnp.int32, (blk, blk), 1))
    n_groups = (i + grp - 1) // grp
    c2 = scale * LOG2_E

    own = pl.ds(pl.multiple_of(i * blk, blk), blk)
    s_own = jnp.where(causal, _dot(kb_ref[0, own, :], qb) * c2, NEG_INF)
    s_ref[nb * blk:(nb + 1) * blk, :] = s_own

    def score_group(gi, m):
        for u in range(grp):
            j = gi * grp + u
            rows = pl.ds(pl.multiple_of(j * blk, blk), blk)
            picked = sel_ref[pl.ds(j, 1), :] > 0.0
            s = jnp.where(picked, _dot(kb_ref[0, rows, :], qb) * c2, NEG_INF)
            s_ref[rows, :] = s
            m = jnp.maximum(m, jnp.max(s, axis=0, keepdims=True))
        return m

    m = lax.fori_loop(0, n_groups, score_group, jnp.max(s_own, axis=0, keepdims=True))

    def weighted_values(j, rows):
        p = jnp.exp2(s_ref[rows, :] - m)
        return _dot(vt_ref[0, 0, j], p.astype(BF16))

    def value_group(gi, acc):
        for u in range(grp):
            j = gi * grp + u
            acc = acc + weighted_values(j, pl.ds(pl.multiple_of(j * blk, blk), blk))
        return acc

    acc = lax.fori_loop(0, n_groups, value_group, weighted_values(i, pl.ds(nb * blk, blk)))
    Dh = MOBA_HEAD_DIM
    o_ref[...] = (acc[0:Dh] / acc[Dh:Dh + 1]).T.astype(o_ref.dtype)


def moba_attention_prompt(q_rot, kb, vt, kmean, *, batch, seq):
    W = MOBA_WIDTH
    blk = MOBA_BLOCK
    Dh = MOBA_HEAD_DIM
    nq = seq // blk
    nb = kmean.shape[1]
    return pl.pallas_call(
        _moba_attn_kernel,
        out_shape=jax.ShapeDtypeStruct((batch * seq, W), BF16),
        grid=(batch, MOBA_HEADS, nq),
        in_specs=[
            pl.BlockSpec((blk, Dh), lambda b, h, i: (b * nq + i, h)),
            pl.BlockSpec((1, seq, Dh), lambda b, h, i: (b, 0, h)),
            pl.BlockSpec((1, 1, nb, VT_ROWS, blk), lambda b, h, i: (b, h, 0, 0, 0)),
            pl.BlockSpec((1, nb, Dh), lambda b, h, i: (b, 0, h)),
        ],
        out_specs=pl.BlockSpec((blk, Dh), lambda b, h, i: (b * nq + i, h)),
        scratch_shapes=[pltpu.VMEM((nb, blk), F32), pltpu.VMEM(((nb + 1) * blk, blk), F32)],
        compiler_params=_cparams("parallel", "parallel", "arbitrary"),
        name="moba_attention",
    )(q_rot, kb.reshape(batch, seq, W), vt, kmean)


PAGES_PER_STEP = 16
PAGES_PER_BLOCK = MOBA_BLOCK // PAGE_SIZE


def _moba_select_kernel(q_ref, km_ref, o_ref):
    Dh = MOBA_HEAD_DIM
    rows = []
    for h in range(MOBA_HEADS):
        rows.append(_dot_nt(q_ref[0, :, h * Dh:(h + 1) * Dh], km_ref[0, h], HIGHEST))
    bs = jnp.concatenate(rows, axis=0)
    nb = bs.shape[1]
    rank = _block_rank(bs, None, 1)
    lane = lax.broadcasted_iota(jnp.int32, bs.shape, 1)
    olane = lax.broadcasted_iota(jnp.int32, (MOBA_HEADS, 128), 1)
    out = jnp.zeros((MOBA_HEADS, 128), jnp.int32)
    for s in range(min(MOBA_TOPK, nb)):
        idx = jnp.sum(jnp.where(rank == s, lane, 0), axis=-1, keepdims=True)
        out = jnp.where(olane == s, idx, out)
    o_ref[0] = out


def moba_select_sample(q3, kmean_past):
    bs, _, W = q3.shape
    nb = kmean_past.shape[2]
    return pl.pallas_call(
        _moba_select_kernel,
        out_shape=jax.ShapeDtypeStruct((bs, MOBA_HEADS, 128), jnp.int32),
        grid=(bs,),
        in_specs=[
            pl.BlockSpec((1, 1, W), lambda b: (b, 0, 0)),
            pl.BlockSpec((1, MOBA_HEADS, nb, MOBA_HEAD_DIM), lambda b: (b, 0, 0, 0)),
        ],
        out_specs=pl.BlockSpec((1, MOBA_HEADS, 128), lambda b: (b, 0, 0)),
        compiler_params=_cparams("parallel"),
        name="moba_select_sample",
    )(q3, kmean_past)


def _moba_decode_kernel(sel_ref, pt_ref, q_ref, kn_ref, vn_ref, *refs, n_pg):
    del sel_ref, pt_ref
    kp_refs, vp_refs, o_ref = refs[:n_pg], refs[n_pg:2 * n_pg], refs[2 * n_pg]
    h = pl.program_id(1)
    scale = MOBA_HEAD_DIM ** -0.5
    rows = PAGE_SIZE * MOBA_HEADS
    q = q_ref[0]
    qb = q.astype(BF16)
    s_own = jnp.sum(q * kn_ref[0], axis=-1, keepdims=True) * scale
    mine = (lax.broadcasted_iota(jnp.int32, (1, rows), 1) % MOBA_HEADS) == h
    sc = [jnp.where(mine, _dot_nt(qb, kp[0, 0].reshape(rows, MOBA_HEAD_DIM).astype(BF16)) * scale, NEG_INF)
          for kp in kp_refs]
    m = s_own
    for x in sc:
        m = jnp.maximum(m, jnp.max(x, axis=-1, keepdims=True))
    p_own = jnp.exp(s_own - m)
    l = p_own
    acc = p_own * vn_ref[0]
    for x, vp in zip(sc, vp_refs):
        p = jnp.exp(x - m)
        l = l + jnp.sum(p, axis=-1, keepdims=True)
        acc = acc + _dot(p.astype(BF16), vp[0, 0].reshape(rows, MOBA_HEAD_DIM).astype(BF16))
    o_ref[0] = (acc / l).astype(o_ref.dtype)


def moba_decode(sel_flat, pt_flat, q3, k3, M3, cache_k, cache_v, *, layer, n_pages, n_sel, v_col):
    bs, _, W = q3.shape
    Dh = MOBA_HEAD_DIM
    n_pg = n_sel * PAGES_PER_BLOCK

    def page_spec(e):
        def page_map(b, h, sel, pt):
            blk = sel[(b * MOBA_HEADS + h) * n_sel + e // PAGES_PER_BLOCK]
            return (layer, pt[b * n_pages + blk * PAGES_PER_BLOCK + e % PAGES_PER_BLOCK], 0, 0, 0)
        return pl.BlockSpec((1, 1, PAGE_SIZE, MOBA_HEADS, Dh), page_map)

    vec = lambda col: pl.BlockSpec((1, 1, Dh), lambda b, h, sel, pt: (b, 0, col(h)))
    return pl.pallas_call(
        functools.partial(_moba_decode_kernel, n_pg=n_pg),
        out_shape=jax.ShapeDtypeStruct((bs, 1, W), F32),
        grid_spec=pltpu.PrefetchScalarGridSpec(
            num_scalar_prefetch=2,
            grid=(bs, MOBA_HEADS),
            in_specs=[vec(lambda h: h), vec(lambda h: h), vec(lambda h: v_col // Dh + h)]
            + [page_spec(e) for e in range(n_pg)] * 2,
            out_specs=vec(lambda h: h),
        ),
        compiler_params=_cparams("parallel", "arbitrary"),
        name="moba_decode",
    )(sel_flat, pt_flat, q3, k3, M3, *([cache_k] * n_pg), *([cache_v] * n_pg))


def _merge_kernel(ya_ref, yb_ref, yc_ref, wa_ref, wb_ref, wc_ref, z0_ref, z1_ref, z2_ref, o_ref):
    acc = _sigmoid(z0_ref[...]) * _dot(ya_ref[...].astype(BF16), wa_ref[...])
    acc = acc + _sigmoid(z1_ref[...]) * _dot(yb_ref[...].astype(BF16), wb_ref[...])
    acc = acc + _sigmoid(z2_ref[...]) * _dot(yc_ref[...].astype(BF16), wc_ref[...])
    o_ref[...] = acc.astype(o_ref.dtype)


def gated_merge(ya, yb, yc, wa, wb, wc, proj, layer, *, z_col, tn=PROJ_TILE, tm_pref=512):
    m = ya.shape[0]
    d = wa.shape[2]
    tm = _row_tile(m, tm_pref)
    nj = d // tn
    z0 = z_col // tn
    wspec = lambda w: pl.BlockSpec((None, w.shape[1], tn), lambda i, j: (layer, 0, j))
    return pl.pallas_call(
        _merge_kernel,
        out_shape=jax.ShapeDtypeStruct((m, d), BF16),
        grid=(m // tm, nj),
        in_specs=[
            pl.BlockSpec((tm, ya.shape[1]), lambda i, j: (i, 0)),
            pl.BlockSpec((tm, yb.shape[1]), lambda i, j: (i, 0)),
            pl.BlockSpec((tm, yc.shape[1]), lambda i, j: (i, 0)),
            wspec(wa), wspec(wb), wspec(wc),
            pl.BlockSpec((tm, tn), lambda i, j: (i, z0 + j)),
            pl.BlockSpec((tm, tn), lambda i, j: (i, z0 + nj + j)),
            pl.BlockSpec((tm, tn), lambda i, j: (i, z0 + 2 * nj + j)),
        ],
        out_specs=pl.BlockSpec((tm, tn), lambda i, j: (i, j)),
        compiler_params=_cparams("parallel", "arbitrary"),
        name="gated_merge",
    )(ya, yb, yc, wa, wb, wc, proj, proj, proj)


def _matmul_residual_kernel(a_ref, w_ref, x_ref, o_ref):
    o_ref[...] = x_ref[...] + _dot(a_ref[...], w_ref[...])


def matmul_residual(a, w, x, layer, *, tn, tm_pref=512):
    m, k = a.shape
    n = w.shape[2]
    tm = _row_tile(m, tm_pref)
    return pl.pallas_call(
        _matmul_residual_kernel,
        out_shape=jax.ShapeDtypeStruct((m, n), F32),
        grid=(m // tm, n // tn),
        in_specs=[
            pl.BlockSpec((tm, k), lambda i, j: (i, 0)),
            pl.BlockSpec((None, k, tn), lambda i, j: (layer, 0, j)),
            pl.BlockSpec((tm, tn), lambda i, j: (i, j)),
        ],
        out_specs=pl.BlockSpec((tm, tn), lambda i, j: (i, j)),
        compiler_params=_cparams("parallel", "arbitrary"),
        name="matmul_residual",
    )(a, w, x)


def _ffn_in_kernel(x_ref, g_ref, wg_ref, wv_ref, o_ref, h_ref):
    @pl.when(pl.program_id(1) == 0)
    def _():
        x = x_ref[...]
        ms = jnp.mean(x * x, axis=-1, keepdims=True)
        h_ref[...] = (x * lax.rsqrt(ms + NORM_EPS) * g_ref[...]).astype(BF16)

    h = h_ref[...]
    gate = _dot(h, wg_ref[...])
    val = _dot(h, wv_ref[...])
    o_ref[...] = (gate * _sigmoid(gate) * val).astype(o_ref.dtype)


def _ffn_in_key_means_kernel(pt_ref, x_ref, g_ref, wg_ref, wv_ref, *refs, n_pg, page_steps):
    del pt_ref
    page_refs, (o_ref, km_ref, h_ref) = refs[:n_pg], refs[n_pg:]
    _ffn_in_kernel(x_ref, g_ref, wg_ref, wv_ref, o_ref, h_ref)

    @pl.when(pl.program_id(0) * pl.num_programs(1) + pl.program_id(1) < page_steps)
    def _():
        for n in range(n_pg // PAGES_PER_BLOCK):
            tot = page_refs[PAGES_PER_BLOCK * n][0, 0].sum(axis=0)
            for e in range(1, PAGES_PER_BLOCK):
                tot = tot + page_refs[PAGES_PER_BLOCK * n + e][0, 0].sum(axis=0)
            tot = tot * (1.0 / MOBA_BLOCK)
            for hd in range(MOBA_HEADS):
                km_ref[0, hd, pl.ds(n, 1), :] = tot[hd:hd + 1, :]


def ffn_in(x, g, w, layer, *, tn=512, tm_pref=1024):
    m, d = x.shape
    hidden = w.shape[2] // 2
    tm = _row_tile(m, tm_pref)
    nj = hidden // tn
    return pl.pallas_call(
        _ffn_in_kernel,
        out_shape=jax.ShapeDtypeStruct((m, hidden), BF16),
        grid=(m // tm, nj),
        in_specs=[
            pl.BlockSpec((tm, d), lambda i, j: (i, 0)),
            pl.BlockSpec((1, d), lambda i, j: (0, 0)),
            pl.BlockSpec((None, d, tn), lambda i, j: (layer, 0, j)),
            pl.BlockSpec((None, d, tn), lambda i, j: (layer, 0, nj + j)),
        ],
        out_specs=pl.BlockSpec((tm, tn), lambda i, j: (i, j)),
        scratch_shapes=[pltpu.VMEM((tm, d), BF16)],
        compiler_params=_cparams("parallel", "arbitrary"),
        name="ffn_in_swiglu",
    )(x, g.reshape(1, d), w, w)


def ffn_in_key_means(x, g, w, layer, page_table, cache_k, *, tn=512, tm_pref=1024):
    m, d = x.shape
    hidden = w.shape[2] // 2
    tm = _row_tile(m, tm_pref)
    nj = hidden // tn
    bs, n_pages = page_table.shape
    pps = min(PAGES_PER_STEP, n_pages)
    per_seq = n_pages // pps
    page_steps = bs * per_seq
    assert n_pages % pps == 0 and page_steps <= (m // tm) * nj

    def page_pos(i, j):
        t = jnp.minimum(i * nj + j, page_steps - 1)
        return t // per_seq, t % per_seq

    def page_spec(e):
        def page_map(i, j, pt):
            b, s = page_pos(i, j)
            return (layer, pt[b * n_pages + s * pps + e], 0, 0, 0)
        return pl.BlockSpec((1, 1, PAGE_SIZE, MOBA_HEADS, MOBA_HEAD_DIM), page_map)

    def km_map(i, j, pt):
        b, s = page_pos(i, j)
        return (b, 0, s, 0)

    return pl.pallas_call(
        functools.partial(_ffn_in_key_means_kernel, n_pg=pps, page_steps=page_steps),
        out_shape=(
            jax.ShapeDtypeStruct((m, hidden), BF16),
            jax.ShapeDtypeStruct((bs, MOBA_HEADS, n_pages // PAGES_PER_BLOCK, MOBA_HEAD_DIM), F32),
        ),
        grid_spec=pltpu.PrefetchScalarGridSpec(
            num_scalar_prefetch=1,
            grid=(m // tm, nj),
            in_specs=[
                pl.BlockSpec((tm, d), lambda i, j, pt: (i, 0)),
                pl.BlockSpec((1, d), lambda i, j, pt: (0, 0)),
                pl.BlockSpec((None, d, tn), lambda i, j, pt: (layer, 0, j)),
                pl.BlockSpec((None, d, tn), lambda i, j, pt: (layer, 0, nj + j)),
            ] + [page_spec(e) for e in range(pps)],
            out_specs=(
                pl.BlockSpec((tm, tn), lambda i, j, pt: (i, j)),
                pl.BlockSpec((1, MOBA_HEADS, pps // PAGES_PER_BLOCK, MOBA_HEAD_DIM), km_map),
            ),
            scratch_shapes=[pltpu.VMEM((tm, d), BF16)],
        ),
        compiler_params=_cparams("arbitrary", "arbitrary"),
        name="ffn_in_swiglu_key_means",
    )(page_table.reshape(-1), x, g.reshape(1, d), w, w, *([cache_k] * pps))


def _rope_tables(pos):
    half = MOBA_HEAD_DIM // 2
    inv = ROPE_THETA ** (-jnp.arange(half, dtype=F32) / half)
    ang = pos.astype(F32)[:, None] * inv[None, :]
    cos, sin = jnp.cos(ang), jnp.sin(ang)
    return jnp.concatenate([cos, cos], axis=-1), jnp.concatenate([-sin, sin], axis=-1)


def _bf16_weights(w):
    o1 = RWKV_PROJ
    o2 = o1 + 3 * MOBA_WIDTH
    o3 = o2 + 2 * GLA_K_WIDTH + GLA_V_WIDTH
    o4 = o3 + GLA_GATE_RANK
    o5 = o4 + GLA_V_WIDTH
    w_in = w['w_in'].astype(BF16)
    total = COL_GATES + w_in.shape[-1] - o5

    def place(lo, hi, at):
        return jnp.pad(w_in[..., lo:hi], ((0, 0), (0, 0), (at, total - at - (hi - lo))))

    packed = (place(0, o1, COL_RWKV)
              + place(o2, o3, COL_GLA) + place(o4, o5, COL_GLA + o3 - o2)
              + place(o3, o4, COL_GLA + o3 - o2 + o5 - o4)
              + place(o1, o2, COL_MOBA) + place(o5, w_in.shape[-1], COL_GATES))
    out = {k: w[k].astype(BF16) for k in ('w_up_rwkv', 'w_up_moba', 'w_up_gla', 'w_out', 'w_ffn_in', 'w_ffn_out')}
    out['w_in'] = packed
    return out


def _mix_and_ffn(x, ya, yb, yc, proj, lw, wb, layer, paged_keys=None):
    merged = gated_merge(ya, yb, yc, wb['w_up_rwkv'], wb['w_up_moba'], wb['w_up_gla'], proj, layer,
                         z_col=COL_GATES)
    x = matmul_residual(merged, wb['w_out'], x, layer, tn=1024, tm_pref=1024)
    if paged_keys is None:
        act, kmean = ffn_in(x, lw['norm_ffn'], wb['w_ffn_in'], layer), None
    else:
        act, kmean = ffn_in_key_means(x, lw['norm_ffn'], wb['w_ffn_in'], layer, *paged_keys)
    return matmul_residual(act, wb['w_ffn_out'], x, layer, tn=512), kmean


def _prompt_layer(x, lw, wb, rope, layer, paged_keys, *, batch, seq):
    proj = norm_matmul(x, lw['norm_mix'], wb['w_in'], layer)

    tt = min(seq, 256)
    ya, r_S, shift = rwkv_branch(
        proj, jnp.zeros((batch, RWKV_PROJ), F32),
        jnp.zeros((batch, RWKV_HEADS, RWKV_HEAD_DIM, RWKV_HEAD_DIM), F32), lw,
        batch=batch, t_pad=seq, t_real=seq, tt=tt, chunk=min(tt, RWKV_CHUNK))
    yc, g_S = gla_branch(
        proj, jnp.zeros((batch, GLA_HEADS, GLA_KEY_DIM, GLA_VALUE_DIM), F32), lw,
        batch=batch, t_pad=seq, t_real=seq, tt=tt, chunk=min(tt, 128), sub=16, col_block=COL_GLA // PROJ_SLOT)

    nq = seq // MOBA_BLOCK
    q_rot, k_rot, v_out, kb, vt, kmean = moba_prep_prompt(
        proj, rope[0], rope[1], lw['moba_q_norm'], lw['moba_k_norm'],
        batch=batch, seq=seq, col0=COL_MOBA // MOBA_WIDTH)
    yb = moba_attention_prompt(q_rot, kb, vt, kmean.reshape(batch, nq, MOBA_WIDTH), batch=batch, seq=seq)

    x, kmean_past = _mix_and_ffn(x, ya, yb, yc, proj, lw, wb, layer, paged_keys)
    k_new = k_rot.reshape(batch, seq, MOBA_HEADS, MOBA_HEAD_DIM)
    v_new = v_out.reshape(batch, seq, MOBA_HEADS, MOBA_HEAD_DIM)
    return (x, k_new, v_new, r_S, shift, g_S), kmean_past


def _sample_layer(x, lw, wb, rope, shift0, rwkv_S0, gla_S0, kmean_past, pt_flat, cache_k, cache_v,
                  *, layer, n_pages):
    bs = x.shape[0]
    proj = norm_matmul(x, lw['norm_mix'], wb['w_in'], layer)

    pad_rows = lambda a: jnp.pad(a[:, None, :], ((0, 0), (0, SUBLANES - 1), (0, 0))).reshape(bs * SUBLANES, -1)
    ya, r_S, shift = rwkv_branch(pad_rows(proj[:, COL_RWKV:COL_RWKV + RWKV_PROJ]), shift0, rwkv_S0, lw,
                                 batch=bs, t_pad=SUBLANES, t_real=1, tt=SUBLANES, chunk=SUBLANES)
    yc, g_S = gla_branch(pad_rows(proj[:, COL_GLA:COL_GLA + PROJ_SLOT]), gla_S0, lw,
                         batch=bs, t_pad=SUBLANES, t_real=1, tt=SUBLANES, chunk=SUBLANES, sub=SUBLANES)
    ya = ya[::SUBLANES]
    yc = yc[::SUBLANES]

    q_rot, k_rot = moba_prep_sample(proj, rope[0], rope[1], lw['moba_q_norm'], lw['moba_k_norm'],
                                    col0=COL_MOBA // MOBA_WIDTH)
    q3 = q_rot.reshape(bs, 1, MOBA_WIDTH)
    n_sel = min(MOBA_TOPK, kmean_past.shape[2])
    sel = moba_select_sample(q3, kmean_past)[:, :, :n_sel].reshape(-1)
    v_col = COL_MOBA + 2 * MOBA_WIDTH
    yb = moba_decode(sel, pt_flat, q3, k_rot.reshape(bs, 1, MOBA_WIDTH), proj.reshape(bs, 1, -1),
                     cache_k, cache_v, layer=layer, n_pages=n_pages, n_sel=n_sel, v_col=v_col).reshape(bs, MOBA_WIDTH)

    x, _ = _mix_and_ffn(x, ya, yb, yc, proj, lw, wb, layer)
    k_new = k_rot.reshape(bs, 1, MOBA_HEADS, MOBA_HEAD_DIM)
    v_new = proj[:, v_col:COL_GATES].reshape(bs, 1, MOBA_HEADS, MOBA_HEAD_DIM)
    return x, k_new, v_new, r_S, shift, g_S


_LAYER_KEYS = ('norm_mix', 'w_in', 'rwkv_mu', 'rwkv_w0', 'rwkv_w_up', 'rwkv_a0', 'rwkv_a_up', 'rwkv_g_up',
               'rwkv_k_k', 'rwkv_k_a', 'rwkv_r_k', 'rwkv_ln_w', 'rwkv_ln_b', 'moba_q_norm', 'moba_k_norm',
               'gla_a_up', 'gla_a_bias', 'gla_o_norm', 'w_up_rwkv', 'w_up_moba', 'w_up_gla', 'w_out',
               'norm_ffn', 'w_ffn_in', 'w_ffn_out')


def kernel(x_prompt, x_sample, cache_k, cache_v, page_table, state_rwkv, state_rwkv_shift, state_gla, norm_mix, w_in, rwkv_mu, rwkv_w0, rwkv_w_up, rwkv_a0, rwkv_a_up, rwkv_g_up, rwkv_k_k, rwkv_k_a, rwkv_r_k, rwkv_ln_w, rwkv_ln_b, moba_q_norm, moba_k_norm, gla_a_up, gla_a_bias, gla_o_norm, w_up_rwkv, w_up_moba, w_up_gla, w_out, norm_ffn, w_ffn_in, w_ffn_out):
    stacked = dict(zip(_LAYER_KEYS, (
        norm_mix, w_in, rwkv_mu, rwkv_w0, rwkv_w_up, rwkv_a0, rwkv_a_up, rwkv_g_up, rwkv_k_k, rwkv_k_a,
        rwkv_r_k, rwkv_ln_w, rwkv_ln_b, moba_q_norm, moba_k_norm, gla_a_up, gla_a_bias, gla_o_norm,
        w_up_rwkv, w_up_moba, w_up_gla, w_out, norm_ffn, w_ffn_in, w_ffn_out)))
    depth = w_in.shape[0]
    bp, seq, d = x_prompt.shape
    bs, dec_seq, _ = x_sample.shape
    n_pages = page_table.shape[1]
    past_len = n_pages * cache_k.shape[2]
    assert dec_seq == 1 and cache_k.shape[2:] == (PAGE_SIZE, MOBA_HEADS, MOBA_HEAD_DIM)
    assert seq % MOBA_BLOCK == 0 and past_len % MOBA_BLOCK == 0 and past_len // MOBA_BLOCK >= MOBA_TOPK

    pt_flat = page_table.reshape(-1)

    rope_p = _rope_tables(jnp.arange(seq))
    rope_s = _rope_tables(jnp.full((bs,), past_len))

    yp = x_prompt.reshape(bp * seq, d)
    ys = x_sample.reshape(bs, d)
    wb = _bf16_weights(stacked)
    small = [k for k in _LAYER_KEYS if k not in wb]
    outs_p, outs_s = [], []
    for l in range(depth):
        lw = {k: stacked[k][l] for k in small}
        (yp, *rest_p), kmean_past = _prompt_layer(yp, lw, wb, rope_p, l, (page_table, cache_k), batch=bp, seq=seq)
        outs_p.append(rest_p)
        ys, *rest_s = _sample_layer(ys, lw, wb, rope_s, state_rwkv_shift[l], state_rwkv[l], state_gla[l],
                                    kmean_past, pt_flat, cache_k, cache_v, layer=l, n_pages=n_pages)
        outs_s.append(rest_s)
    stack = lambda outs, i: jnp.stack([o[i] for o in outs])
    return (yp.reshape(bp, seq, d), ys.reshape(bs, 1, d),
            stack(outs_p, 0), stack(outs_p, 1), stack(outs_p, 2), stack(outs_p, 3), stack(outs_p, 4),
            stack(outs_s, 0), stack(outs_s, 1), stack(outs_s, 2), stack(outs_s, 3), stack(outs_s, 4))
```

```python
import functools

import jax
import jax.numpy as jnp
from jax import lax
from jax.experimental import pallas as pl
from jax.experimental.pallas import tpu as pltpu

F32 = jnp.float32
BF16 = jnp.bfloat16
HIGHEST = lax.Precision.HIGHEST

PAGE_SIZE = 128
RWKV_HEADS = 8
RWKV_HEAD_DIM = 64
RWKV_WIDTH = RWKV_HEADS * RWKV_HEAD_DIM
RWKV_DECAY_RANK = 64
RWKV_ICLR_RANK = 64
RWKV_GATE_RANK = 128
RWKV_PROJ = 3 * RWKV_WIDTH + RWKV_DECAY_RANK + RWKV_ICLR_RANK + RWKV_GATE_RANK
RWKV_GN_EPS = 64e-5
MOBA_HEADS = 8
MOBA_HEAD_DIM = 128
MOBA_WIDTH = MOBA_HEADS * MOBA_HEAD_DIM
MOBA_BLOCK = 256
MOBA_TOPK = 3
ROPE_THETA = 10000.0
GLA_HEADS = 4
GLA_KEY_DIM = 64
GLA_VALUE_DIM = 128
GLA_K_WIDTH = GLA_HEADS * GLA_KEY_DIM
GLA_V_WIDTH = GLA_HEADS * GLA_VALUE_DIM
GLA_GATE_RANK = 16
GLA_GATE_PAD = 256
GLA_SECT = 2 * GLA_K_WIDTH + 2 * GLA_V_WIDTH + GLA_GATE_PAD
GLA_TAU = 16.0
PROJ_SLOT = 2048
PROJ_TILE = 1024
COL_RWKV = 0
COL_GLA = PROJ_SLOT
COL_MOBA = 2 * PROJ_SLOT
COL_GATES = COL_MOBA + 3 * MOBA_WIDTH
N_BRANCHES = 3
NORM_EPS = 1e-6
NEG_INF = -1e30
LOG2_E = 1.4426950408889634
VT_ROWS = MOBA_HEAD_DIM + 16

VMEM_LIMIT_BYTES = 56 * 1024 * 1024
SUBLANES = 8

GATE_PASSES = 3
STATE_PASSES = 3
RWKV_CHUNK = 128
RWKV_CHUNK_PASSES = 1
GLA_CHUNK_PASSES = 1


def _cparams(*sem):
    return pltpu.CompilerParams(dimension_semantics=sem, vmem_limit_bytes=VMEM_LIMIT_BYTES)


def _dot(a, b, precision=None):
    return jnp.dot(a, b, preferred_element_type=F32, precision=precision)


def _dot_nt(a, b, precision=None):
    return lax.dot_general(a, b, (((1,), (1,)), ((), ())), preferred_element_type=F32, precision=precision)


def _dot_tn(a, b, precision=None):
    return lax.dot_general(a, b, (((0,), (0,)), ((), ())), preferred_element_type=F32, precision=precision)


def _mm(a, b, kind, passes):
    f = {'nn': _dot, 'nt': _dot_nt, 'tn': _dot_tn}[kind]
    if passes == 6:
        return f(a, b, HIGHEST)
    ah = a.astype(BF16)
    bh = b.astype(BF16)
    if passes == 1:
        return f(ah, bh)
    al = (a - ah.astype(F32)).astype(BF16)
    bl = (b - bh.astype(F32)).astype(BF16)
    return f(ah, bh) + f(ah, bl) + f(al, bh)


def _split3(x):
    hi = x.astype(BF16)
    r1 = x - hi.astype(F32)
    mid = r1.astype(BF16)
    lo = (r1 - mid.astype(F32)).astype(BF16)
    return hi, mid, lo


def _tril_ones(n, strict=False):
    r = lax.broadcasted_iota(jnp.int32, (n, n), 0)
    c = lax.broadcasted_iota(jnp.int32, (n, n), 1)
    return (r > c) if strict else (r >= c)


def _cumsum_rows(x, tri_bf16):
    hi, mid, lo = _split3(x)
    return _dot(tri_bf16, hi) + _dot(tri_bf16, mid) + _dot(tri_bf16, lo)


def _sigmoid(x):
    return 1.0 / (1.0 + jnp.exp(-x))


def _softplus(x):
    return jnp.maximum(x, 0.0) + jnp.log(1.0 + jnp.exp(-jnp.abs(x)))


def _row_tile(m, pref):
    return pref if m % pref == 0 else m


def _act_dtype(tile_rows):
    return BF16 if tile_rows % (2 * SUBLANES) == 0 else F32


def _norm_matmul_kernel(x_ref, g_ref, w_ref, o_ref, h_ref):
    @pl.when(pl.program_id(1) == 0)
    def _():
        x = x_ref[...]
        ms = jnp.mean(x * x, axis=-1, keepdims=True)
        h_ref[...] = (x * lax.rsqrt(ms + NORM_EPS) * g_ref[...]).astype(BF16)

    o_ref[...] = _dot(h_ref[...], w_ref[...]).astype(o_ref.dtype)


def norm_matmul(x, g, w, layer, *, tn=1024, tm_pref=1024):
    m, d = x.shape
    n = w.shape[2]
    tm = _row_tile(m, tm_pref)
    return pl.pallas_call(
        _norm_matmul_kernel,
        out_shape=jax.ShapeDtypeStruct((m, n), F32),
        grid=(m // tm, n // tn),
        in_specs=[
            pl.BlockSpec((tm, d), lambda i, j: (i, 0)),
            pl.BlockSpec((1, d), lambda i, j: (0, 0)),
            pl.BlockSpec((None, d, tn), lambda i, j: (layer, 0, j)),
        ],
        out_specs=pl.BlockSpec((tm, tn), lambda i, j: (i, j)),
        scratch_shapes=[pltpu.VMEM((tm, d), BF16)],
        compiler_params=_cparams("parallel", "arbitrary"),
        name="norm_matmul",
    )(x, g.reshape(1, d), w)


def _rwkv_kernel(p_ref, shift0_ref, s0_ref, mu_ref, w0_ref, wup_ref, a0_ref, aup_ref, gup_ref,
                 kk_ref, ka_ref, rk_ref, lnw_ref, lnb_ref,
                 y_ref, sout_ref, shout_ref,
                 carry_ref, s_ref, r_s, k_s, v_s, kk_s, b_s, lw_s, y_s,
                 *, t_real, tt, chunk):
    t = pl.program_id(1)
    nt = pl.num_programs(1)
    W = RWKV_WIDTH
    N = RWKV_HEAD_DIM
    H = RWKV_HEADS
    C = chunk
    cp = RWKV_CHUNK_PASSES

    @pl.when(t == 0)
    def _():
        carry_ref[...] = shift0_ref[0]
        s_ref[...] = s0_ref[0]

    P = p_ref[...]
    rowid = lax.broadcasted_iota(jnp.int32, (tt, 1), 0)
    prev = jnp.where(rowid == 0, carry_ref[...], pltpu.roll(P, 1, axis=0))
    last_row = (t_real - 1) % tt
    carry_ref[...] = P[last_row:last_row + 1, :]

    @pl.when(t == nt - 1)
    def _():
        shout_ref[0] = P[last_row:last_row + 1, :]

    Pm = P + (prev - P) * mu_ref[...]
    r = Pm[:, 0:W]
    k = Pm[:, W:2 * W]
    v = Pm[:, 2 * W:3 * W]
    o = 3 * W
    wd = Pm[:, o:o + RWKV_DECAY_RANK]
    ad = Pm[:, o + RWKV_DECAY_RANK:o + RWKV_DECAY_RANK + RWKV_ICLR_RANK]
    gd = Pm[:, o + RWKV_DECAY_RANK + RWKV_ICLR_RANK:]
    w_val = -_softplus(-(w0_ref[...] + _mm(jnp.tanh(wd), wup_ref[...], 'nn', GATE_PASSES))) - 0.5
    lw = -jnp.exp(w_val)
    a = _sigmoid(a0_ref[...] + _mm(ad, aup_ref[...], 'nn', GATE_PASSES))
    g = _mm(_sigmoid(gd), gup_ref[...], 'nn', GATE_PASSES)
    kk = k * kk_ref[...]
    kmod = k * (1.0 + (a - 1.0) * ka_ref[...])
    if t_real % tt != 0:
        valid = (t * tt + rowid) < t_real
        lw = jnp.where(valid, lw, 0.0)
        kk = jnp.where(valid, kk, 0.0)
        kmod = jnp.where(valid, kmod, 0.0)
        v = jnp.where(valid, v, 0.0)
    seg_r = lax.broadcasted_iota(jnp.int32, (W, W), 0) // N
    seg_c = lax.broadcasted_iota(jnp.int32, (W, W), 1) // N
    seg = jnp.where(seg_r == seg_c, 1.0, 0.0).astype(BF16)
    hi, md, lo = _split3(kk * kk)
    ss = _dot(hi, seg) + _dot(md, seg) + _dot(lo, seg)
    kkn = kk / jnp.maximum(jnp.sqrt(ss), 1e-12)
    r_s[...] = r
    k_s[...] = kmod
    v_s[...] = v
    kk_s[...] = kkn
    b_s[...] = kkn * a
    lw_s[...] = lw

    tri = jnp.where(_tril_ones(C), 1.0, 0.0).astype(BF16)
    low_incl = _tril_ones(C)
    low_strict = _tril_ones(C, strict=True)
    eye = lax.broadcasted_iota(jnp.int32, (N, N), 0) == lax.broadcasted_iota(jnp.int32, (N, N), 1)
    mid = max(C // 2 - 1, 0)
    heads = range(H)
    hs = [slice(h * N, (h + 1) * N) for h in heads]

    def chunk_body(c, carry):
        rows = pl.ds(pl.multiple_of(c * C, C), C)
        lw_c = lw_s[rows, :]
        cum = _cumsum_rows(lw_c, tri)
        cume = cum - lw_c
        rho = cum[mid:mid + 1, :]
        cum_last = cum[C - 1:C, :]
        r_c = r_s[rows, :]
        k_c = k_s[rows, :]
        v_c = v_s[rows, :]
        kk_c = kk_s[rows, :]
        b_c = b_s[rows, :]
        e_out = jnp.exp(rho - cum)
        e_end = jnp.exp(cum_last - cum)
        kk_hat = kk_c * jnp.exp(cume - rho)
        r_hat = r_c * jnp.exp(cum - rho)
        k_til = k_c * e_out
        b_til = b_c * e_out
        kk_abs = kk_c * jnp.exp(cume)
        r_abs = r_c * jnp.exp(cum)
        k_end = k_c * e_end
        b_end = b_c * e_end
        gam = jnp.exp(cum_last)

        A = [_mm(jnp.concatenate([kk_hat[:, s], r_hat[:, s]], axis=0),
                 jnp.concatenate([k_til[:, s], b_til[:, s]], axis=0), 'nt', cp) for s in hs]
        Lk = [jnp.where(low_strict, x[0:C, 0:C], 0.0) for x in A]
        Lb = [jnp.where(low_strict, x[0:C, C:2 * C], 0.0) for x in A]
        Ark = [jnp.where(low_incl, x[C:2 * C, 0:C], 0.0) for x in A]
        Arb = [jnp.where(low_incl, x[C:2 * C, C:2 * C], 0.0) for x in A]
        X = [jnp.concatenate([kk_abs[:, s], _mm(Lk[h], v_c[:, s], 'nn', cp)], axis=1) for h, s in zip(heads, hs)]
        X = [X[h] - _mm(Lb[h], X[h], 'nn', cp) for h in heads]
        Lp = Lb
        p = 2
        while p < C:
            Lp = [_mm(x, x, 'nn', cp) for x in Lp]
            X = [X[h] + _mm(Lp[h], X[h], 'nn', cp) for h in heads]
            p *= 2
        RY = [jnp.concatenate([r_abs[:, s], _mm(Ark[h], v_c[:, s], 'nn', cp)], axis=1)
              - _mm(Arb[h], X[h], 'nn', cp) for h, s in zip(heads, hs)]
        MN = [_mm(X[h], b_end[:, s], 'tn', cp) for h, s in zip(heads, hs)]
        VK = [_mm(v_c[:, s], k_end[:, s], 'tn', cp) for s in hs]
        for h, s in zip(heads, hs):
            S = s_ref[h]
            Y = _mm(RY[h][:, 0:N], S, 'nt', STATE_PASSES) + RY[h][:, N:2 * N]
            trans = jnp.where(eye, jnp.broadcast_to(gam[:, s], (N, N)), 0.0) - MN[h][0:N]
            s_ref[h] = _mm(S, trans, 'nn', STATE_PASSES) + (VK[h] - MN[h][N:2 * N])
            mu = jnp.mean(Y, axis=-1, keepdims=True)
            var = jnp.mean(jnp.square(Y - mu), axis=-1, keepdims=True)
            yn = (Y - mu) * lax.rsqrt(var + RWKV_GN_EPS) * lnw_ref[:, s] + lnb_ref[:, s]
            bonus = jnp.sum(r_c[:, s] * k_c[:, s] * rk_ref[:, s], axis=-1, keepdims=True) * v_c[:, s]
            y_s[rows, s] = yn + bonus
        return carry

    lax.fori_loop(0, tt // C, chunk_body, 0)
    y_ref[...] = (y_s[...] * g).astype(y_ref.dtype)

    @pl.when(t == nt - 1)
    def _():
        sout_ref[0] = s_ref[...]


def rwkv_branch(P, shift0, S0, lw, *, batch, t_pad, t_real, tt, chunk):
    nt = t_pad // tt
    W = RWKV_WIDTH
    vec = lambda a: a.reshape(1, -1)
    full = lambda shape: pl.BlockSpec(shape, lambda b, t: (0,) * len(shape))
    kern = functools.partial(_rwkv_kernel, t_real=t_real, tt=tt, chunk=chunk)
    y, s_out, sh_out = pl.pallas_call(
        kern,
        out_shape=(
            jax.ShapeDtypeStruct((batch * t_pad, W), _act_dtype(tt)),
            jax.ShapeDtypeStruct((batch, RWKV_HEADS, RWKV_HEAD_DIM, RWKV_HEAD_DIM), F32),
            jax.ShapeDtypeStruct((batch, 1, RWKV_PROJ), F32),
        ),
        grid=(batch, nt),
        in_specs=[
            pl.BlockSpec((tt, RWKV_PROJ), lambda b, t: (b * nt + t, 0)),
            pl.BlockSpec((1, 1, RWKV_PROJ), lambda b, t: (b, 0, 0)),
            pl.BlockSpec((1, RWKV_HEADS, RWKV_HEAD_DIM, RWKV_HEAD_DIM), lambda b, t: (b, 0, 0, 0)),
            full((1, RWKV_PROJ)),
            full((1, W)),
            full((RWKV_DECAY_RANK, W)),
            full((1, W)),
            full((RWKV_ICLR_RANK, W)),
            full((RWKV_GATE_RANK, W)),
            full((1, W)), full((1, W)), full((1, W)), full((1, W)), full((1, W)),
        ],
        out_specs=(
            pl.BlockSpec((tt, W), lambda b, t: (b * nt + t, 0)),
            pl.BlockSpec((1, RWKV_HEADS, RWKV_HEAD_DIM, RWKV_HEAD_DIM), lambda b, t: (b, 0, 0, 0)),
            pl.BlockSpec((1, 1, RWKV_PROJ), lambda b, t: (b, 0, 0)),
        ),
        scratch_shapes=[
            pltpu.VMEM((1, RWKV_PROJ), F32),
            pltpu.VMEM((RWKV_HEADS, RWKV_HEAD_DIM, RWKV_HEAD_DIM), F32),
        ] + [pltpu.VMEM((tt, W), F32) for _ in range(7)],
        compiler_params=_cparams("parallel", "arbitrary"),
        name="rwkv7_branch",
    )(P, shift0.reshape(batch, 1, RWKV_PROJ), S0,
      vec(lw['rwkv_mu']), vec(lw['rwkv_w0']), lw['rwkv_w_up'], vec(lw['rwkv_a0']), lw['rwkv_a_up'],
      lw['rwkv_g_up'], vec(lw['rwkv_k_k']), vec(lw['rwkv_k_a']), vec(lw['rwkv_r_k']),
      vec(lw['rwkv_ln_w']), vec(lw['rwkv_ln_b']))
    return y, s_out, sh_out.reshape(batch, RWKV_PROJ)


def _gla_kernel(g_ref, s0_ref, aup_ref, abias_ref, onorm_ref,
                y_ref, sout_ref,
                s_ref, la_s,
                *, t_real, tt, chunk, sub):
    t = pl.program_id(1)
    nt = pl.num_programs(1)
    KW, VW = GLA_K_WIDTH, GLA_V_WIDTH
    dk, dv = GLA_KEY_DIM, GLA_VALUE_DIM
    C = chunk
    cp = GLA_CHUNK_PASSES

    @pl.when(t == 0)
    def _():
        s_ref[...] = s0_ref[0]

    gl = g_ref[:, 2 * KW + 2 * VW:GLA_SECT]
    x = _mm(gl, aup_ref[...], 'nn', GATE_PASSES) + abias_ref[...]
    la = -_softplus(-x) * (1.0 / GLA_TAU)
    padded = t_real % tt != 0
    if padded:
        rowid = lax.broadcasted_iota(jnp.int32, (tt, 1), 0)
        valid = (t * tt + rowid) < t_real
        la = jnp.where(valid, la, 0.0)
    la_s[...] = la

    tri = jnp.where(_tril_ones(C), 1.0, 0.0).astype(BF16)
    ones_cv = jnp.ones((C, dv), BF16)
    low_incl = _tril_ones(C)
    crow = lax.broadcasted_iota(jnp.int32, (C, 1), 0)
    nsub = C // sub
    heads = range(GLA_HEADS)

    def chunk_body(c, carry):
        rows = pl.ds(pl.multiple_of(c * C, C), C)
        la_c = la_s[rows, :]
        cum_all = _cumsum_rows(la_c, tri)
        if padded:
            vmask = (t * tt + c * C + crow) < t_real
        q, k, v, cums = [], [], [], []
        for h in heads:
            q.append(g_ref[rows, h * dk:(h + 1) * dk] * (dk ** -0.5))
            k_h = g_ref[rows, KW + h * dk:KW + (h + 1) * dk]
            v_h = g_ref[rows, 2 * KW + h * dv:2 * KW + (h + 1) * dv]
            if padded:
                k_h = jnp.where(vmask, k_h, 0.0)
                v_h = jnp.where(vmask, v_h, 0.0)
            k.append(k_h)
            v.append(v_h)
            cums.append(cum_all[:, h * dk:(h + 1) * dk])
        bounds = [[jnp.zeros((1, dk), F32)] + [cm[i * sub - 1:i * sub, :] for i in range(1, nsub)] for cm in cums]
        qt = [q[h] * jnp.exp(cums[h] - jnp.concatenate(
            [jnp.broadcast_to(b, (sub, dk)) for b in bounds[h]], axis=0)) for h in heads]
        att = []
        for h in heads:
            att_rows = []
            for i in range(nsub):
                kt = k[h] * jnp.exp(jnp.where(crow < (i + 1) * sub, bounds[h][i] - cums[h], NEG_INF))
                att_rows.append(_mm(qt[h][i * sub:(i + 1) * sub], kt, 'nt', cp))
            att.append(jnp.where(low_incl, jnp.concatenate(att_rows, axis=0), 0.0))
        intra = [_mm(att[h], v[h], 'nn', cp) for h in heads]
        kv = [_mm(k[h] * jnp.exp(cums[h][C - 1:C, :] - cums[h]), v[h], 'tn', STATE_PASSES) for h in heads]
        tot = []
        for h in heads:
            hi, md, lo = _split3(la_c[:, h * dk:(h + 1) * dk])
            tot.append(_dot_tn(hi, ones_cv) + _dot_tn(md, ones_cv) + _dot_tn(lo, ones_cv))
        for h in heads:
            S = s_ref[h]
            o = intra[h] + _mm(q[h] * jnp.exp(cums[h]), S, 'nn', STATE_PASSES)
            s_ref[h] = S * jnp.exp(tot[h]) + kv[h]
            r_h = g_ref[rows, 2 * KW + VW + h * dv:2 * KW + VW + (h + 1) * dv]
            ms = jnp.mean(o * o, axis=-1, keepdims=True)
            on = o * lax.rsqrt(ms + NORM_EPS) * onorm_ref[...]
            y_ref[rows, h * dv:(h + 1) * dv] = (on * (r_h * _sigmoid(r_h))).astype(y_ref.dtype)
        return carry

    lax.fori_loop(0, tt // C, chunk_body, 0)

    @pl.when(t == nt - 1)
    def _():
        sout_ref[0] = s_ref[...]


def gla_branch(G, S0, lw, *, batch, t_pad, t_real, tt, chunk, sub, col_block=0):
    nt = t_pad // tt
    full = lambda shape: pl.BlockSpec(shape, lambda b, t: (0,) * len(shape))
    aup = jnp.zeros((GLA_GATE_PAD, GLA_K_WIDTH), F32).at[:GLA_GATE_RANK].set(lw['gla_a_up'])
    kern = functools.partial(_gla_kernel, t_real=t_real, tt=tt, chunk=chunk, sub=sub)
    y, s_out = pl.pallas_call(
        kern,
        out_shape=(
            jax.ShapeDtypeStruct((batch * t_pad, GLA_V_WIDTH), _act_dtype(tt)),
            jax.ShapeDtypeStruct((batch, GLA_HEADS, GLA_KEY_DIM, GLA_VALUE_DIM), F32),
        ),
        grid=(batch, nt),
        in_specs=[
            pl.BlockSpec((tt, PROJ_SLOT), lambda b, t: (b * nt + t, col_block)),
            pl.BlockSpec((1, GLA_HEADS, GLA_KEY_DIM, GLA_VALUE_DIM), lambda b, t: (b, 0, 0, 0)),
            full((GLA_GATE_PAD, GLA_K_WIDTH)),
            full((1, GLA_K_WIDTH)),
            full((1, GLA_VALUE_DIM)),
        ],
        out_specs=(
            pl.BlockSpec((tt, GLA_V_WIDTH), lambda b, t: (b * nt + t, 0)),
            pl.BlockSpec((1, GLA_HEADS, GLA_KEY_DIM, GLA_VALUE_DIM), lambda b, t: (b, 0, 0, 0)),
        ),
        scratch_shapes=[
            pltpu.VMEM((GLA_HEADS, GLA_KEY_DIM, GLA_VALUE_DIM), F32),
            pltpu.VMEM((tt, GLA_K_WIDTH), F32),
        ],
        compiler_params=_cparams("parallel", "arbitrary"),
        name="gla_branch",
    )(G, S0, aup, lw['gla_a_bias'].reshape(1, -1), lw['gla_o_norm'].reshape(1, -1))
    return y, s_out


def _norm_rope(x, g, cos, sin):
    ms = jnp.mean(x * x, axis=-1, keepdims=True)
    y = x * lax.rsqrt(ms + NORM_EPS) * g
    return y * cos + pltpu.roll(y, MOBA_HEAD_DIM // 2, axis=1) * sin


def _moba_prep_prompt_kernel(q_ref, k_ref, v_ref, cos_ref, sin_ref, gq_ref, gk_ref,
                             qo_ref, ko_ref, vo_ref, kb_ref, vt_ref, km_ref):
    Dh = MOBA_HEAD_DIM
    cos = cos_ref[...]
    sin = sin_ref[...]
    rows = q_ref.shape[0]
    vo_ref[...] = v_ref[...]
    for h in range(MOBA_HEADS):
        sl = slice(h * Dh, (h + 1) * Dh)
        qo_ref[:, sl] = _norm_rope(q_ref[:, sl], gq_ref[...], cos, sin)
        kr = _norm_rope(k_ref[:, sl], gk_ref[...], cos, sin)
        ko_ref[:, sl] = kr
        kb_ref[:, sl] = kr.astype(BF16)
        km_ref[0, :, sl] = jnp.sum(kr, axis=0, keepdims=True) * (1.0 / rows)
        vt_ref[0, h, 0, 0:Dh, :] = v_ref[:, sl].T.astype(BF16)
        vt_ref[0, h, 0, Dh:VT_ROWS, :] = jnp.ones((VT_ROWS - Dh, rows), BF16)


def moba_prep_prompt(M, cos, sin, gq, gk, *, batch, seq, col0=0):
    m = M.shape[0]
    W = MOBA_WIDTH
    blk = MOBA_BLOCK
    Dh = MOBA_HEAD_DIM
    nq = seq // blk
    nblk = m // blk
    return pl.pallas_call(
        _moba_prep_prompt_kernel,
        out_shape=(
            jax.ShapeDtypeStruct((m, W), F32),
            jax.ShapeDtypeStruct((m, W), F32),
            jax.ShapeDtypeStruct((m, W), F32),
            jax.ShapeDtypeStruct((m, W), BF16),
            jax.ShapeDtypeStruct((batch, MOBA_HEADS, nq, VT_ROWS, blk), BF16),
            jax.ShapeDtypeStruct((nblk, 1, W), F32),
        ),
        grid=(nblk,),
        in_specs=[
            pl.BlockSpec((blk, W), lambda i: (i, col0)),
            pl.BlockSpec((blk, W), lambda i: (i, col0 + 1)),
            pl.BlockSpec((blk, W), lambda i: (i, col0 + 2)),
            pl.BlockSpec((blk, Dh), lambda i: (i % nq, 0)),
            pl.BlockSpec((blk, Dh), lambda i: (i % nq, 0)),
            pl.BlockSpec((1, Dh), lambda i: (0, 0)),
            pl.BlockSpec((1, Dh), lambda i: (0, 0)),
        ],
        out_specs=(
            pl.BlockSpec((blk, W), lambda i: (i, 0)),
            pl.BlockSpec((blk, W), lambda i: (i, 0)),
            pl.BlockSpec((blk, W), lambda i: (i, 0)),
            pl.BlockSpec((blk, W), lambda i: (i, 0)),
            pl.BlockSpec((1, MOBA_HEADS, 1, VT_ROWS, blk), lambda i: (i // nq, 0, i % nq, 0, 0)),
            pl.BlockSpec((1, 1, W), lambda i: (i, 0, 0)),
        ),
        compiler_params=_cparams("parallel"),
        name="moba_qkv_prep",
    )(M, M, M, cos, sin, gq.reshape(1, -1), gk.reshape(1, -1))


def _moba_prep_sample_kernel(q_ref, k_ref, cos_ref, sin_ref, gq_ref, gk_ref, qo_ref, ko_ref):
    Dh = MOBA_HEAD_DIM
    cos = cos_ref[...]
    sin = sin_ref[...]
    for h in range(MOBA_HEADS):
        sl = slice(h * Dh, (h + 1) * Dh)
        qo_ref[:, sl] = _norm_rope(q_ref[:, sl], gq_ref[...], cos, sin)
        ko_ref[:, sl] = _norm_rope(k_ref[:, sl], gk_ref[...], cos, sin)


def moba_prep_sample(M, cos, sin, gq, gk, *, col0=0):
    m = M.shape[0]
    W = MOBA_WIDTH
    Dh = MOBA_HEAD_DIM
    return pl.pallas_call(
        _moba_prep_sample_kernel,
        out_shape=(jax.ShapeDtypeStruct((m, W), F32), jax.ShapeDtypeStruct((m, W), F32)),
        grid=(1,),
        in_specs=[
            pl.BlockSpec((m, W), lambda i: (0, col0)),
            pl.BlockSpec((m, W), lambda i: (0, col0 + 1)),
            pl.BlockSpec((m, Dh), lambda i: (0, 0)),
            pl.BlockSpec((m, Dh), lambda i: (0, 0)),
            pl.BlockSpec((1, Dh), lambda i: (0, 0)),
            pl.BlockSpec((1, Dh), lambda i: (0, 0)),
        ],
        out_specs=(pl.BlockSpec((m, W), lambda i: (0, 0)), pl.BlockSpec((m, W), lambda i: (0, 0))),
        compiler_params=_cparams("arbitrary"),
        name="moba_qk_prep_sample",
    )(M, M, cos, sin, gq.reshape(1, -1), gk.reshape(1, -1))


def _block_rank(bs, n_valid, axis):
    nb = bs.shape[axis]
    idx = lax.broadcasted_iota(jnp.int32, bs.shape, axis)
    rank = jnp.zeros(bs.shape, jnp.int32)
    for mm in range(nb):
        one = bs[mm:mm + 1, :] if axis == 0 else bs[:, mm:mm + 1]
        beats = (one > bs) | ((one == bs) & (idx > mm))
        if n_valid is not None:
            beats = beats & (mm < n_valid)
        rank = rank + jnp.where(beats, 1, 0)
    return rank


def _moba_attn_kernel(q_ref, kb_ref, vt_ref, km_ref, o_ref, sel_ref, s_ref):
    i = pl.program_id(2)
    blk = MOBA_BLOCK
    scale = MOBA_HEAD_DIM ** -0.5
    qT = q_ref[...].T
    km = km_ref[0]
    nb = km.shape[0]
    grp = next(g for g in (4, 2, 1) if nb % g == 0)
    bsT = _dot(km, qT, HIGHEST)
    blk_id = lax.broadcasted_iota(jnp.int32, (nb, blk), 0)
    rank = _block_rank(bsT, i, 0)
    sel_ref[...] = jnp.where((blk_id < i) & (rank < MOBA_TOPK), 1.0, 0.0)
    qb = qT.astype(BF16)
    causal = (lax.broadcasted_iota(jnp.int32, (blk, blk), 0)
              <= lax.broadcasted_iota(jnp.int32, (blk, blk), 1))
    n_groups = (i + grp - 1) // grp
    c2 = scale * LOG2_E

    own = pl.ds(pl.multiple_of(i * blk, blk), blk)
    s_own = jnp.where(causal, _dot(kb_ref[0, own, :], qb) * c2, NEG_INF)
    s_ref[nb * blk:(nb + 1) * blk, :] = s_own

    def score_group(gi, m):
        for u in range(grp):
            j = gi * grp + u
            rows = pl.ds(pl.multiple_of(j * blk, blk), blk)
            picked = sel_ref[pl.ds(j, 1), :] > 0.0
            s = jnp.where(picked, _dot(kb_ref[0, rows, :], qb) * c2, NEG_INF)
            s_ref[rows, :] = s
            m = jnp.maximum(m, jnp.max(s, axis=0, keepdims=True))
        return m

    m = lax.fori_loop(0, n_groups, score_group, jnp.max(s_own, axis=0, keepdims=True))

    def weighted_values(j, rows):
        p = jnp.exp2(s_ref[rows, :] - m)
        return _dot(vt_ref[0, 0, j], p.astype(BF16))

    def value_group(gi, acc):
        for u in range(grp):
            j = gi * grp + u
            acc = acc + weighted_values(j, pl.ds(pl.multiple_of(j * blk, blk), blk))
        return acc

    acc = lax.fori_loop(0, n_groups, value_group, weighted_values(i, pl.ds(nb * blk, blk)))
    Dh = MOBA_HEAD_DIM
    o_ref[...] = (acc[0:Dh] / acc[Dh:Dh + 1]).T.astype(o_ref.dtype)


def moba_attention_prompt(q_rot, kb, vt, kmean, *, batch, seq):
    W = MOBA_WIDTH
    blk = MOBA_BLOCK
    Dh = MOBA_HEAD_DIM
    nq = seq // blk
    nb = kmean.shape[1]
    return pl.pallas_call(
        _moba_attn_kernel,
        out_shape=jax.ShapeDtypeStruct((batch * seq, W), BF16),
        grid=(batch, MOBA_HEADS, nq),
        in_specs=[
            pl.BlockSpec((blk, Dh), lambda b, h, i: (b * nq + i, h)),
            pl.BlockSpec((1, seq, Dh), lambda b, h, i: (b, 0, h)),
            pl.BlockSpec((1, 1, nb, VT_ROWS, blk), lambda b, h, i: (b, h, 0, 0, 0)),
            pl.BlockSpec((1, nb, Dh), lambda b, h, i: (b, 0, h)),
        ],
        out_specs=pl.BlockSpec((blk, Dh), lambda b, h, i: (b * nq + i, h)),
        scratch_shapes=[pltpu.VMEM((nb, blk), F32), pltpu.VMEM(((nb + 1) * blk, blk), F32)],
        compiler_params=_cparams("parallel", "parallel", "arbitrary"),
        name="moba_attention",
    )(q_rot, kb.reshape(batch, seq, W), vt, kmean)


PAGES_PER_STEP = 16
PAGES_PER_BLOCK = MOBA_BLOCK // PAGE_SIZE


def _moba_select_kernel(q_ref, km_ref, o_ref):
    Dh = MOBA_HEAD_DIM
    rows = []
    for h in range(MOBA_HEADS):
        rows.append(_dot_nt(q_ref[0, :, h * Dh:(h + 1) * Dh], km_ref[0, h], HIGHEST))
    bs = jnp.concatenate(rows, axis=0)
    nb = bs.shape[1]
    rank = _block_rank(bs, None, 1)
    lane = lax.broadcasted_iota(jnp.int32, bs.shape, 1)
    olane = lax.broadcasted_iota(jnp.int32, (MOBA_HEADS, 128), 1)
    out = jnp.zeros((MOBA_HEADS, 128), jnp.int32)
    for s in range(min(MOBA_TOPK, nb)):
        idx = jnp.sum(jnp.where(rank == s, lane, 0), axis=-1, keepdims=True)
        out = jnp.where(olane == s, idx, out)
    o_ref[0] = out


def moba_select_sample(q3, kmean_past):
    bs, _, W = q3.shape
    nb = kmean_past.shape[2]
    return pl.pallas_call(
        _moba_select_kernel,
        out_shape=jax.ShapeDtypeStruct((bs, MOBA_HEADS, 128), jnp.int32),
        grid=(bs,),
        in_specs=[
            pl.BlockSpec((1, 1, W), lambda b: (b, 0, 0)),
            pl.BlockSpec((1, MOBA_HEADS, nb, MOBA_HEAD_DIM), lambda b: (b, 0, 0, 0)),
        ],
        out_specs=pl.BlockSpec((1, MOBA_HEADS, 128), lambda b: (b, 0, 0)),
        compiler_params=_cparams("parallel"),
        name="moba_select_sample",
    )(q3, kmean_past)


def _moba_decode_kernel(sel_ref, pt_ref, q_ref, kn_ref, vn_ref, ck_ref, cv_ref, o_ref, kbuf, vbuf, sem,
                        *, layer, n_pg, n_sel, n_pages):
    step = pl.program_id(0)
    scale = MOBA_HEAD_DIM ** -0.5

    def page_copies(st, slot):
        b = st // MOBA_HEADS
        h = st % MOBA_HEADS
        copies = []
        for e in range(n_pg):
            blk = sel_ref[st * n_sel + e // PAGES_PER_BLOCK]
            page = pt_ref[b * n_pages + blk * PAGES_PER_BLOCK + e % PAGES_PER_BLOCK]
            copies.append(pltpu.make_async_copy(ck_ref.at[layer, page, :, h, :], kbuf.at[slot, e], sem.at[slot, 0, e]))
            copies.append(pltpu.make_async_copy(cv_ref.at[layer, page, :, h, :], vbuf.at[slot, e], sem.at[slot, 1, e]))
        return copies

    @pl.when(step == 0)
    def _():
        for c in page_copies(0, 0):
            c.start()

    @pl.when(step + 1 < pl.num_programs(0))
    def _():
        for c in page_copies(step + 1, (step + 1) % 2):
            c.start()

    slot = step % 2
    for c in page_copies(step, slot):
        c.wait()

    rows = n_pg * PAGE_SIZE
    q = q_ref[0]
    s_own = jnp.sum(q * kn_ref[0], axis=-1, keepdims=True) * scale
    kp = kbuf[slot].reshape(rows, MOBA_HEAD_DIM).astype(BF16)
    vp = vbuf[slot].reshape(rows, MOBA_HEAD_DIM).astype(BF16)
    sc = _dot_nt(q.astype(BF16), kp) * scale
    m = jnp.maximum(s_own, jnp.max(sc, axis=-1, keepdims=True))
    p_own = jnp.exp(s_own - m)
    p = jnp.exp(sc - m)
    l = p_own + jnp.sum(p, axis=-1, keepdims=True)
    acc = p_own * vn_ref[0] + _dot(p.astype(BF16), vp)
    o_ref[0] = (acc / l).astype(o_ref.dtype)


def moba_decode(sel_flat, pt_flat, q3, k3, M3, cache_k, cache_v, *, layer, n_pages, n_sel, v_col):
    bs, _, W = q3.shape
    Dh = MOBA_HEAD_DIM
    H = MOBA_HEADS
    n_pg = n_sel * PAGES_PER_BLOCK
    vec = lambda col: pl.BlockSpec((1, 1, Dh), lambda st, sel, pt: (st // H, 0, col(st % H)))
    hbm = pl.BlockSpec(memory_space=pl.ANY)
    return pl.pallas_call(
        functools.partial(_moba_decode_kernel, layer=layer, n_pg=n_pg, n_sel=n_sel, n_pages=n_pages),
        out_shape=jax.ShapeDtypeStruct((bs, 1, W), F32),
        grid_spec=pltpu.PrefetchScalarGridSpec(
            num_scalar_prefetch=2,
            grid=(bs * H,),
            in_specs=[vec(lambda h: h), vec(lambda h: h), vec(lambda h: v_col // Dh + h), hbm, hbm],
            out_specs=vec(lambda h: h),
            scratch_shapes=[
                pltpu.VMEM((2, n_pg, PAGE_SIZE, Dh), F32),
                pltpu.VMEM((2, n_pg, PAGE_SIZE, Dh), F32),
                pltpu.SemaphoreType.DMA((2, 2, n_pg)),
            ],
        ),
        compiler_params=_cparams("arbitrary"),
        name="moba_decode",
    )(sel_flat, pt_flat, q3, k3, M3, cache_k, cache_v)


def _merge_kernel(ya_ref, yb_ref, yc_ref, wa_ref, wb_ref, wc_ref, z0_ref, z1_ref, z2_ref, o_ref):
    acc = _sigmoid(z0_ref[...]) * _dot(ya_ref[...].astype(BF16), wa_ref[...])
    acc = acc + _sigmoid(z1_ref[...]) * _dot(yb_ref[...].astype(BF16), wb_ref[...])
    acc = acc + _sigmoid(z2_ref[...]) * _dot(yc_ref[...].astype(BF16), wc_ref[...])
    o_ref[...] = acc.astype(o_ref.dtype)


def gated_merge(ya, yb, yc, wa, wb, wc, proj, layer, *, z_col, tn=PROJ_TILE, tm_pref=512):
    m = ya.shape[0]
    d = wa.shape[2]
    tm = _row_tile(m, tm_pref)
    nj = d // tn
    z0 = z_col // tn
    wspec = lambda w: pl.BlockSpec((None, w.shape[1], tn), lambda i, j: (layer, 0, j))
    return pl.pallas_call(
        _merge_kernel,
        out_shape=jax.ShapeDtypeStruct((m, d), BF16),
        grid=(m // tm, nj),
        in_specs=[
            pl.BlockSpec((tm, ya.shape[1]), lambda i, j: (i, 0)),
            pl.BlockSpec((tm, yb.shape[1]), lambda i, j: (i, 0)),
            pl.BlockSpec((tm, yc.shape[1]), lambda i, j: (i, 0)),
            wspec(wa), wspec(wb), wspec(wc),
            pl.BlockSpec((tm, tn), lambda i, j: (i, z0 + j)),
            pl.BlockSpec((tm, tn), lambda i, j: (i, z0 + nj + j)),
            pl.BlockSpec((tm, tn), lambda i, j: (i, z0 + 2 * nj + j)),
        ],
        out_specs=pl.BlockSpec((tm, tn), lambda i, j: (i, j)),
        compiler_params=_cparams("parallel", "arbitrary"),
        name="gated_merge",
    )(ya, yb, yc, wa, wb, wc, proj, proj, proj)


def _matmul_residual_kernel(a_ref, w_ref, x_ref, o_ref):
    o_ref[...] = x_ref[...] + _dot(a_ref[...], w_ref[...])


def matmul_residual(a, w, x, layer, *, tn, tm_pref=512):
    m, k = a.shape
    n = w.shape[2]
    tm = _row_tile(m, tm_pref)
    return pl.pallas_call(
        _matmul_residual_kernel,
        out_shape=jax.ShapeDtypeStruct((m, n), F32),
        grid=(m // tm, n // tn),
        in_specs=[
            pl.BlockSpec((tm, k), lambda i, j: (i, 0)),
            pl.BlockSpec((None, k, tn), lambda i, j: (layer, 0, j)),
            pl.BlockSpec((tm, tn), lambda i, j: (i, j)),
        ],
        out_specs=pl.BlockSpec((tm, tn), lambda i, j: (i, j)),
        compiler_params=_cparams("parallel", "arbitrary"),
        name="matmul_residual",
    )(a, w, x)


def _ffn_in_kernel(x_ref, g_ref, wg_ref, wv_ref, o_ref, h_ref):
    @pl.when(pl.program_id(1) == 0)
    def _():
        x = x_ref[...]
        ms = jnp.mean(x * x, axis=-1, keepdims=True)
        h_ref[...] = (x * lax.rsqrt(ms + NORM_EPS) * g_ref[...]).astype(BF16)

    h = h_ref[...]
    gate = _dot(h, wg_ref[...])
    val = _dot(h, wv_ref[...])
    o_ref[...] = (gate * _sigmoid(gate) * val).astype(o_ref.dtype)


def _ffn_in_key_means_kernel(pt_ref, x_ref, g_ref, wg_ref, wv_ref, *refs, n_pg, page_steps):
    del pt_ref
    page_refs, (o_ref, km_ref, h_ref) = refs[:n_pg], refs[n_pg:]
    _ffn_in_kernel(x_ref, g_ref, wg_ref, wv_ref, o_ref, h_ref)

    @pl.when(pl.program_id(0) * pl.num_programs(1) + pl.program_id(1) < page_steps)
    def _():
        for n in range(n_pg // PAGES_PER_BLOCK):
            tot = page_refs[PAGES_PER_BLOCK * n][0, 0].sum(axis=0)
            for e in range(1, PAGES_PER_BLOCK):
                tot = tot + page_refs[PAGES_PER_BLOCK * n + e][0, 0].sum(axis=0)
            tot = tot * (1.0 / MOBA_BLOCK)
            for hd in range(MOBA_HEADS):
                km_ref[0, hd, pl.ds(n, 1), :] = tot[hd:hd + 1, :]


def ffn_in(x, g, w, layer, *, tn=512, tm_pref=1024):
    m, d = x.shape
    hidden = w.shape[2] // 2
    tm = _row_tile(m, tm_pref)
    nj = hidden // tn
    return pl.pallas_call(
        _ffn_in_kernel,
        out_shape=jax.ShapeDtypeStruct((m, hidden), BF16),
        grid=(m // tm, nj),
        in_specs=[
            pl.BlockSpec((tm, d), lambda i, j: (i, 0)),
            pl.BlockSpec((1, d), lambda i, j: (0, 0)),
            pl.BlockSpec((None, d, tn), lambda i, j: (layer, 0, j)),
            pl.BlockSpec((None, d, tn), lambda i, j: (layer, 0, nj + j)),
        ],
        out_specs=pl.BlockSpec((tm, tn), lambda i, j: (i, j)),
        scratch_shapes=[pltpu.VMEM((tm, d), BF16)],
        compiler_params=_cparams("parallel", "arbitrary"),
        name="ffn_in_swiglu",
    )(x, g.reshape(1, d), w, w)


def ffn_in_key_means(x, g, w, layer, page_table, cache_k, *, tn=512, tm_pref=1024):
    m, d = x.shape
    hidden = w.shape[2] // 2
    tm = _row_tile(m, tm_pref)
    nj = hidden // tn
    bs, n_pages = page_table.shape
    pps = min(PAGES_PER_STEP, n_pages)
    per_seq = n_pages // pps
    page_steps = bs * per_seq
    assert n_pages % pps == 0 and page_steps <= (m // tm) * nj

    def page_pos(i, j):
        t = jnp.minimum(i * nj + j, page_steps - 1)
        return t // per_seq, t % per_seq

    def page_spec(e):
        def page_map(i, j, pt):
            b, s = page_pos(i, j)
            return (layer, pt[b * n_pages + s * pps + e], 0, 0, 0)
        return pl.BlockSpec((1, 1, PAGE_SIZE, MOBA_HEADS, MOBA_HEAD_DIM), page_map)

    def km_map(i, j, pt):
        b, s = page_pos(i, j)
        return (b, 0, s, 0)

    return pl.pallas_call(
        functools.partial(_ffn_in_key_means_kernel, n_pg=pps, page_steps=page_steps),
        out_shape=(
            jax.ShapeDtypeStruct((m, hidden), BF16),
            jax.ShapeDtypeStruct((bs, MOBA_HEADS, n_pages // PAGES_PER_BLOCK, MOBA_HEAD_DIM), F32),
        ),
        grid_spec=pltpu.PrefetchScalarGridSpec(
            num_scalar_prefetch=1,
            grid=(m // tm, nj),
            in_specs=[
                pl.BlockSpec((tm, d), lambda i, j, pt: (i, 0)),
                pl.BlockSpec((1, d), lambda i, j, pt: (0, 0)),
                pl.BlockSpec((None, d, tn), lambda i, j, pt: (layer, 0, j)),
                pl.BlockSpec((None, d, tn), lambda i, j, pt: (layer, 0, nj + j)),
            ] + [page_spec(e) for e in range(pps)],
            out_specs=(
                pl.BlockSpec((tm, tn), lambda i, j, pt: (i, j)),
                pl.BlockSpec((1, MOBA_HEADS, pps // PAGES_PER_BLOCK, MOBA_HEAD_DIM), km_map),
            ),
            scratch_shapes=[pltpu.VMEM((tm, d), BF16)],
        ),
        compiler_params=_cparams("arbitrary", "arbitrary"),
        name="ffn_in_swiglu_key_means",
    )(page_table.reshape(-1), x, g.reshape(1, d), w, w, *([cache_k] * pps))


def _rope_tables(pos):
    half = MOBA_HEAD_DIM // 2
    inv = ROPE_THETA ** (-jnp.arange(half, dtype=F32) / half)
    ang = pos.astype(F32)[:, None] * inv[None, :]
    cos, sin = jnp.cos(ang), jnp.sin(ang)
    return jnp.concatenate([cos, cos], axis=-1), jnp.concatenate([-sin, sin], axis=-1)


def _bf16_weights(w):
    o1 = RWKV_PROJ
    o2 = o1 + 3 * MOBA_WIDTH
    o3 = o2 + 2 * GLA_K_WIDTH + GLA_V_WIDTH
    o4 = o3 + GLA_GATE_RANK
    o5 = o4 + GLA_V_WIDTH
    w_in = w['w_in'].astype(BF16)
    total = COL_GATES + w_in.shape[-1] - o5

    def place(lo, hi, at):
        return jnp.pad(w_in[..., lo:hi], ((0, 0), (0, 0), (at, total - at - (hi - lo))))

    packed = (place(0, o1, COL_RWKV)
              + place(o2, o3, COL_GLA) + place(o4, o5, COL_GLA + o3 - o2)
              + place(o3, o4, COL_GLA + o3 - o2 + o5 - o4)
              + place(o1, o2, COL_MOBA) + place(o5, w_in.shape[-1], COL_GATES))
    out = {k: w[k].astype(BF16) for k in ('w_up_rwkv', 'w_up_moba', 'w_up_gla', 'w_out', 'w_ffn_in', 'w_ffn_out')}
    out['w_in'] = packed
    return out


def _mix_and_ffn(x, ya, yb, yc, proj, lw, wb, layer, paged_keys=None):
    merged = gated_merge(ya, yb, yc, wb['w_up_rwkv'], wb['w_up_moba'], wb['w_up_gla'], proj, layer,
                         z_col=COL_GATES, tn=512, tm_pref=1024)
    x = matmul_residual(merged, wb['w_out'], x, layer, tn=1024, tm_pref=1024)
    if paged_keys is None:
        act, kmean = ffn_in(x, lw['norm_ffn'], wb['w_ffn_in'], layer), None
    else:
        act, kmean = ffn_in_key_means(x, lw['norm_ffn'], wb['w_ffn_in'], layer, *paged_keys)
    return matmul_residual(act, wb['w_ffn_out'], x, layer, tn=512, tm_pref=1024), kmean


def _prompt_layer(x, lw, wb, rope, layer, paged_keys, *, batch, seq):
    proj = norm_matmul(x, lw['norm_mix'], wb['w_in'], layer)

    tt = min(seq, 256)
    ya, r_S, shift = rwkv_branch(
        proj, jnp.zeros((batch, RWKV_PROJ), F32),
        jnp.zeros((batch, RWKV_HEADS, RWKV_HEAD_DIM, RWKV_HEAD_DIM), F32), lw,
        batch=batch, t_pad=seq, t_real=seq, tt=tt, chunk=min(tt, RWKV_CHUNK))
    yc, g_S = gla_branch(
        proj, jnp.zeros((batch, GLA_HEADS, GLA_KEY_DIM, GLA_VALUE_DIM), F32), lw,
        batch=batch, t_pad=seq, t_real=seq, tt=tt, chunk=min(tt, 128), sub=16, col_block=COL_GLA // PROJ_SLOT)

    nq = seq // MOBA_BLOCK
    q_rot, k_rot, v_out, kb, vt, kmean = moba_prep_prompt(
        proj, rope[0], rope[1], lw['moba_q_norm'], lw['moba_k_norm'],
        batch=batch, seq=seq, col0=COL_MOBA // MOBA_WIDTH)
    yb = moba_attention_prompt(q_rot, kb, vt, kmean.reshape(batch, nq, MOBA_WIDTH), batch=batch, seq=seq)

    x, kmean_past = _mix_and_ffn(x, ya, yb, yc, proj, lw, wb, layer, paged_keys)
    k_new = k_rot.reshape(batch, seq, MOBA_HEADS, MOBA_HEAD_DIM)
    v_new = v_out.reshape(batch, seq, MOBA_HEADS, MOBA_HEAD_DIM)
    return (x, k_new, v_new, r_S, shift, g_S), kmean_past


def _sample_layer(x, lw, wb, rope, shift0, rwkv_S0, gla_S0, kmean_past, pt_flat, cache_k, cache_v,
                  *, layer, n_pages):
    bs = x.shape[0]
    proj = norm_matmul(x, lw['norm_mix'], wb['w_in'], layer)

    pad_rows = lambda a: jnp.pad(a[:, None, :], ((0, 0), (0, SUBLANES - 1), (0, 0))).reshape(bs * SUBLANES, -1)
    ya, r_S, shift = rwkv_branch(pad_rows(proj[:, COL_RWKV:COL_RWKV + RWKV_PROJ]), shift0, rwkv_S0, lw,
                                 batch=bs, t_pad=SUBLANES, t_real=1, tt=SUBLANES, chunk=SUBLANES)
    yc, g_S = gla_branch(pad_rows(proj[:, COL_GLA:COL_GLA + PROJ_SLOT]), gla_S0, lw,
                         batch=bs, t_pad=SUBLANES, t_real=1, tt=SUBLANES, chunk=SUBLANES, sub=SUBLANES)
    ya = ya[::SUBLANES]
    yc = yc[::SUBLANES]

    q_rot, k_rot = moba_prep_sample(proj, rope[0], rope[1], lw['moba_q_norm'], lw['moba_k_norm'],
                                    col0=COL_MOBA // MOBA_WIDTH)
    q3 = q_rot.reshape(bs, 1, MOBA_WIDTH)
    n_sel = min(MOBA_TOPK, kmean_past.shape[2])
    sel = moba_select_sample(q3, kmean_past)[:, :, :n_sel].reshape(-1)
    v_col = COL_MOBA + 2 * MOBA_WIDTH
    yb = moba_decode(sel, pt_flat, q3, k_rot.reshape(bs, 1, MOBA_WIDTH), proj.reshape(bs, 1, -1),
                     cache_k, cache_v, layer=layer, n_pages=n_pages, n_sel=n_sel, v_col=v_col).reshape(bs, MOBA_WIDTH)

    x, _ = _mix_and_ffn(x, ya, yb, yc, proj, lw, wb, layer)
    k_new = k_rot.reshape(bs, 1, MOBA_HEADS, MOBA_HEAD_DIM)
    v_new = proj[:, v_col:COL_GATES].reshape(bs, 1, MOBA_HEADS, MOBA_HEAD_DIM)
    return x, k_new, v_new, r_S, shift, g_S


_LAYER_KEYS = ('norm_mix', 'w_in', 'rwkv_mu', 'rwkv_w0', 'rwkv_w_up', 'rwkv_a0', 'rwkv_a_up', 'rwkv_g_up',
               'rwkv_k_k', 'rwkv_k_a', 'rwkv_r_k', 'rwkv_ln_w', 'rwkv_ln_b', 'moba_q_norm', 'moba_k_norm',
               'gla_a_up', 'gla_a_bias', 'gla_o_norm', 'w_up_rwkv', 'w_up_moba', 'w_up_gla', 'w_out',
               'norm_ffn', 'w_ffn_in', 'w_ffn_out')


def kernel(x_prompt, x_sample, cache_k, cache_v, page_table, state_rwkv, state_rwkv_shift, state_gla, norm_mix, w_in, rwkv_mu, rwkv_w0, rwkv_w_up, rwkv_a0, rwkv_a_up, rwkv_g_up, rwkv_k_k, rwkv_k_a, rwkv_r_k, rwkv_ln_w, rwkv_ln_b, moba_q_norm, moba_k_norm, gla_a_up, gla_a_bias, gla_o_norm, w_up_rwkv, w_up_moba, w_up_gla, w_out, norm_ffn, w_ffn_in, w_ffn_out):
    stacked = dict(zip(_LAYER_KEYS, (
        norm_mix, w_in, rwkv_mu, rwkv_w0, rwkv_w_up, rwkv_a0, rwkv_a_up, rwkv_g_up, rwkv_k_k, rwkv_k_a,
        rwkv_r_k, rwkv_ln_w, rwkv_ln_b, moba_q_norm, moba_k_norm, gla_a_up, gla_a_bias, gla_o_norm,
        w_up_rwkv, w_up_moba, w_up_gla, w_out, norm_ffn, w_ffn_in, w_ffn_out)))
    depth = w_in.shape[0]
    bp, seq, d = x_prompt.shape
    bs, dec_seq, _ = x_sample.shape
    n_pages = page_table.shape[1]
    past_len = n_pages * cache_k.shape[2]
    assert dec_seq == 1 and cache_k.shape[2:] == (PAGE_SIZE, MOBA_HEADS, MOBA_HEAD_DIM)
    assert seq % MOBA_BLOCK == 0 and past_len % MOBA_BLOCK == 0 and past_len // MOBA_BLOCK >= MOBA_TOPK

    pt_flat = page_table.reshape(-1)

    rope_p = _rope_tables(jnp.arange(seq))
    rope_s = _rope_tables(jnp.full((bs,), past_len))

    yp = x_prompt.reshape(bp * seq, d)
    ys = x_sample.reshape(bs, d)
    wb = _bf16_weights(stacked)
    small = [k for k in _LAYER_KEYS if k not in wb]
    outs_p, outs_s = [], []
    for l in range(depth):
        lw = {k: stacked[k][l] for k in small}
        (yp, *rest_p), kmean_past = _prompt_layer(yp, lw, wb, rope_p, l, (page_table, cache_k), batch=bp, seq=seq)
        outs_p.append(rest_p)
        ys, *rest_s = _sample_layer(ys, lw, wb, rope_s, state_rwkv_shift[l], state_rwkv[l], state_gla[l],
                                    kmean_past, pt_flat, cache_k, cache_v, layer=l, n_pages=n_pages)
        outs_s.append(rest_s)
    stack = lambda outs, i: jnp.stack([o[i] for o in outs])
    return (yp.reshape(bp, seq, d), ys.reshape(bs, 1, d),
            stack(outs_p, 0), stack(outs_p, 1), stack(outs_p, 2), stack(outs_p, 3), stack(outs_p, 4),
            stack(outs_s, 0), stack(outs_s, 1), stack(outs_s, 2), stack(outs_s, 3), stack(outs_s, 4))
```

```python
import functools

import jax
import jax.numpy as jnp
from jax import lax
from jax.experimental import pallas as pl
from jax.experimental.pallas import tpu as pltpu

F32 = jnp.float32
BF16 = jnp.bfloat16
HIGHEST = lax.Precision.HIGHEST

PAGE_SIZE = 128
RWKV_HEADS = 8
RWKV_HEAD_DIM = 64
RWKV_WIDTH = RWKV_HEADS * RWKV_HEAD_DIM
RWKV_DECAY_RANK = 64
RWKV_ICLR_RANK = 64
RWKV_GATE_RANK = 128
RWKV_PROJ = 3 * RWKV_WIDTH + RWKV_DECAY_RANK + RWKV_ICLR_RANK + RWKV_GATE_RANK
RWKV_GN_EPS = 64e-5
MOBA_HEADS = 8
MOBA_HEAD_DIM = 128
MOBA_WIDTH = MOBA_HEADS * MOBA_HEAD_DIM
MOBA_BLOCK = 256
MOBA_TOPK = 3
ROPE_THETA = 10000.0
GLA_HEADS = 4
GLA_KEY_DIM = 64
GLA_VALUE_DIM = 128
GLA_K_WIDTH = GLA_HEADS * GLA_KEY_DIM
GLA_V_WIDTH = GLA_HEADS * GLA_VALUE_DIM
GLA_GATE_RANK = 16
GLA_GATE_PAD = 256
GLA_SECT = 2 * GLA_K_WIDTH + 2 * GLA_V_WIDTH + GLA_GATE_PAD
GLA_TAU = 16.0
PROJ_SLOT = 2048
PROJ_TILE = 1024
COL_RWKV = 0
COL_GLA = PROJ_SLOT
COL_MOBA = 2 * PROJ_SLOT
COL_GATES = COL_MOBA + 3 * MOBA_WIDTH
N_BRANCHES = 3
NORM_EPS = 1e-6
NEG_INF = -1e30
LOG2_E = 1.4426950408889634
VT_ROWS = MOBA_HEAD_DIM + 16
MOBA_HEADS_PER_STEP = 4

VMEM_LIMIT_BYTES = 56 * 1024 * 1024
SUBLANES = 8

GATE_PASSES = 3
STATE_PASSES = 3
RWKV_CHUNK = 128
RWKV_CHUNK_PASSES = 1
GLA_CHUNK_PASSES = 1


def _cparams(*sem):
    return pltpu.CompilerParams(dimension_semantics=sem, vmem_limit_bytes=VMEM_LIMIT_BYTES)


def _dot(a, b, precision=None):
    return jnp.dot(a, b, preferred_element_type=F32, precision=precision)


def _dot_nt(a, b, precision=None):
    return lax.dot_general(a, b, (((1,), (1,)), ((), ())), preferred_element_type=F32, precision=precision)


def _dot_tn(a, b, precision=None):
    return lax.dot_general(a, b, (((0,), (0,)), ((), ())), preferred_element_type=F32, precision=precision)


def _mm(a, b, kind, passes):
    f = {'nn': _dot, 'nt': _dot_nt, 'tn': _dot_tn}[kind]
    if passes == 6:
        return f(a, b, HIGHEST)
    ah = a.astype(BF16)
    bh = b.astype(BF16)
    if passes == 1:
        return f(ah, bh)
    al = (a - ah.astype(F32)).astype(BF16)
    bl = (b - bh.astype(F32)).astype(BF16)
    return f(ah, bh) + f(ah, bl) + f(al, bh)


def _split3(x):
    hi = x.astype(BF16)
    r1 = x - hi.astype(F32)
    mid = r1.astype(BF16)
    lo = (r1 - mid.astype(F32)).astype(BF16)
    return hi, mid, lo


def _tril_ones(n, strict=False):
    r = lax.broadcasted_iota(jnp.int32, (n, n), 0)
    c = lax.broadcasted_iota(jnp.int32, (n, n), 1)
    return (r > c) if strict else (r >= c)


def _cumsum_rows(x, tri_bf16):
    hi, mid, lo = _split3(x)
    return _dot(tri_bf16, hi) + _dot(tri_bf16, mid) + _dot(tri_bf16, lo)


def _sigmoid(x):
    return 1.0 / (1.0 + jnp.exp(-x))


def _softplus(x):
    return jnp.maximum(x, 0.0) + jnp.log(1.0 + jnp.exp(-jnp.abs(x)))


def _row_tile(m, pref):
    return pref if m % pref == 0 else m


def _act_dtype(tile_rows):
    return BF16 if tile_rows % (2 * SUBLANES) == 0 else F32


def _norm_matmul_kernel(x_ref, g_ref, w_ref, o_ref, h_ref):
    @pl.when(pl.program_id(1) == 0)
    def _():
        x = x_ref[...]
        ms = jnp.mean(x * x, axis=-1, keepdims=True)
        h_ref[...] = (x * lax.rsqrt(ms + NORM_EPS) * g_ref[...]).astype(BF16)

    o_ref[...] = _dot(h_ref[...], w_ref[...]).astype(o_ref.dtype)


def norm_matmul(x, g, w, layer, *, tn=1024, tm_pref=1024):
    m, d = x.shape
    n = w.shape[2]
    tm = _row_tile(m, tm_pref)
    return pl.pallas_call(
        _norm_matmul_kernel,
        out_shape=jax.ShapeDtypeStruct((m, n), F32),
        grid=(m // tm, n // tn),
        in_specs=[
            pl.BlockSpec((tm, d), lambda i, j: (i, 0)),
            pl.BlockSpec((1, d), lambda i, j: (0, 0)),
            pl.BlockSpec((None, d, tn), lambda i, j: (layer, 0, j)),
        ],
        out_specs=pl.BlockSpec((tm, tn), lambda i, j: (i, j)),
        scratch_shapes=[pltpu.VMEM((tm, d), BF16)],
        compiler_params=_cparams("parallel", "arbitrary"),
        name="norm_matmul",
    )(x, g.reshape(1, d), w)


def _rwkv_kernel(p_ref, shift0_ref, s0_ref, mu_ref, w0_ref, wup_ref, a0_ref, aup_ref, gup_ref,
                 kk_ref, ka_ref, rk_ref, lnw_ref, lnb_ref,
                 y_ref, sout_ref, shout_ref,
                 carry_ref, s_ref, r_s, k_s, v_s, kk_s, b_s, lw_s, y_s,
                 *, t_real, tt, chunk):
    t = pl.program_id(1)
    nt = pl.num_programs(1)
    W = RWKV_WIDTH
    N = RWKV_HEAD_DIM
    H = RWKV_HEADS
    C = chunk
    cp = RWKV_CHUNK_PASSES

    @pl.when(t == 0)
    def _():
        carry_ref[...] = shift0_ref[0]
        s_ref[...] = s0_ref[0]

    P = p_ref[...]
    rowid = lax.broadcasted_iota(jnp.int32, (tt, 1), 0)
    prev = jnp.where(rowid == 0, carry_ref[...], pltpu.roll(P, 1, axis=0))
    last_row = (t_real - 1) % tt
    carry_ref[...] = P[last_row:last_row + 1, :]

    @pl.when(t == nt - 1)
    def _():
        shout_ref[0] = P[last_row:last_row + 1, :]

    Pm = P + (prev - P) * mu_ref[...]
    r = Pm[:, 0:W]
    k = Pm[:, W:2 * W]
    v = Pm[:, 2 * W:3 * W]
    o = 3 * W
    wd = Pm[:, o:o + RWKV_DECAY_RANK]
    ad = Pm[:, o + RWKV_DECAY_RANK:o + RWKV_DECAY_RANK + RWKV_ICLR_RANK]
    gd = Pm[:, o + RWKV_DECAY_RANK + RWKV_ICLR_RANK:]
    w_val = -_softplus(-(w0_ref[...] + _mm(jnp.tanh(wd), wup_ref[...], 'nn', GATE_PASSES))) - 0.5
    lw = -jnp.exp(w_val)
    a = _sigmoid(a0_ref[...] + _mm(ad, aup_ref[...], 'nn', GATE_PASSES))
    g = _mm(_sigmoid(gd), gup_ref[...], 'nn', GATE_PASSES)
    kk = k * kk_ref[...]
    kmod = k * (1.0 + (a - 1.0) * ka_ref[...])
    if t_real % tt != 0:
        valid = (t * tt + rowid) < t_real
        lw = jnp.where(valid, lw, 0.0)
        kk = jnp.where(valid, kk, 0.0)
        kmod = jnp.where(valid, kmod, 0.0)
        v = jnp.where(valid, v, 0.0)
    seg_r = lax.broadcasted_iota(jnp.int32, (W, W), 0) // N
    seg_c = lax.broadcasted_iota(jnp.int32, (W, W), 1) // N
    seg = jnp.where(seg_r == seg_c, 1.0, 0.0).astype(BF16)
    hi, md, lo = _split3(kk * kk)
    ss = _dot(hi, seg) + _dot(md, seg) + _dot(lo, seg)
    kkn = kk / jnp.maximum(jnp.sqrt(ss), 1e-12)
    r_s[...] = r
    k_s[...] = kmod
    v_s[...] = v
    kk_s[...] = kkn
    b_s[...] = kkn * a
    lw_s[...] = lw

    tri = jnp.where(_tril_ones(C), 1.0, 0.0).astype(BF16)
    low_incl = _tril_ones(C)
    low_strict = _tril_ones(C, strict=True)
    eye = lax.broadcasted_iota(jnp.int32, (N, N), 0) == lax.broadcasted_iota(jnp.int32, (N, N), 1)
    mid = max(C // 2 - 1, 0)
    heads = range(H)
    hs = [slice(h * N, (h + 1) * N) for h in heads]

    def chunk_body(c, carry):
        rows = pl.ds(pl.multiple_of(c * C, C), C)
        lw_c = lw_s[rows, :]
        cum = _cumsum_rows(lw_c, tri)
        cume = cum - lw_c
        rho = cum[mid:mid + 1, :]
        cum_last = cum[C - 1:C, :]
        r_c = r_s[rows, :]
        k_c = k_s[rows, :]
        v_c = v_s[rows, :]
        kk_c = kk_s[rows, :]
        b_c = b_s[rows, :]
        e_out = jnp.exp(rho - cum)
        e_end = jnp.exp(cum_last - cum)
        kk_hat = kk_c * jnp.exp(cume - rho)
        r_hat = r_c * jnp.exp(cum - rho)
        k_til = k_c * e_out
        b_til = b_c * e_out
        kk_abs = kk_c * jnp.exp(cume)
        r_abs = r_c * jnp.exp(cum)
        k_end = k_c * e_end
        b_end = b_c * e_end
        gam = jnp.exp(cum_last)

        A = [_mm(jnp.concatenate([kk_hat[:, s], r_hat[:, s]], axis=0),
                 jnp.concatenate([k_til[:, s], b_til[:, s]], axis=0), 'nt', cp) for s in hs]
        Lk = [jnp.where(low_strict, x[0:C, 0:C], 0.0) for x in A]
        Lb = [jnp.where(low_strict, x[0:C, C:2 * C], 0.0) for x in A]
        Ark = [jnp.where(low_incl, x[C:2 * C, 0:C], 0.0) for x in A]
        Arb = [jnp.where(low_incl, x[C:2 * C, C:2 * C], 0.0) for x in A]
        X = [jnp.concatenate([kk_abs[:, s], _mm(Lk[h], v_c[:, s], 'nn', cp)], axis=1) for h, s in zip(heads, hs)]
        X = [X[h] - _mm(Lb[h], X[h], 'nn', cp) for h in heads]
        Lp = Lb
        p = 2
        while p < C:
            Lp = [_mm(x, x, 'nn', cp) for x in Lp]
            X = [X[h] + _mm(Lp[h], X[h], 'nn', cp) for h in heads]
            p *= 2
        RY = [jnp.concatenate([r_abs[:, s], _mm(Ark[h], v_c[:, s], 'nn', cp)], axis=1)
              - _mm(Arb[h], X[h], 'nn', cp) for h, s in zip(heads, hs)]
        MN = [_mm(X[h], b_end[:, s], 'tn', cp) for h, s in zip(heads, hs)]
        VK = [_mm(v_c[:, s], k_end[:, s], 'tn', cp) for s in hs]
        for h, s in zip(heads, hs):
            S = s_ref[h]
            Y = _mm(RY[h][:, 0:N], S, 'nt', STATE_PASSES) + RY[h][:, N:2 * N]
            trans = jnp.where(eye, jnp.broadcast_to(gam[:, s], (N, N)), 0.0) - MN[h][0:N]
            s_ref[h] = _mm(S, trans, 'nn', STATE_PASSES) + (VK[h] - MN[h][N:2 * N])
            mu = jnp.mean(Y, axis=-1, keepdims=True)
            var = jnp.mean(jnp.square(Y - mu), axis=-1, keepdims=True)
            yn = (Y - mu) * lax.rsqrt(var + RWKV_GN_EPS) * lnw_ref[:, s] + lnb_ref[:, s]
            bonus = jnp.sum(r_c[:, s] * k_c[:, s] * rk_ref[:, s], axis=-1, keepdims=True) * v_c[:, s]
            y_s[rows, s] = yn + bonus
        return carry

    lax.fori_loop(0, tt // C, chunk_body, 0)
    y_ref[...] = (y_s[...] * g).astype(y_ref.dtype)

    @pl.when(t == nt - 1)
    def _():
        sout_ref[0] = s_ref[...]


def rwkv_branch(P, shift0, S0, lw, *, batch, t_pad, t_real, tt, chunk):
    nt = t_pad // tt
    W = RWKV_WIDTH
    vec = lambda a: a.reshape(1, -1)
    full = lambda shape: pl.BlockSpec(shape, lambda b, t: (0,) * len(shape))
    kern = functools.partial(_rwkv_kernel, t_real=t_real, tt=tt, chunk=chunk)
    y, s_out, sh_out = pl.pallas_call(
        kern,
        out_shape=(
            jax.ShapeDtypeStruct((batch * t_pad, W), _act_dtype(tt)),
            jax.ShapeDtypeStruct((batch, RWKV_HEADS, RWKV_HEAD_DIM, RWKV_HEAD_DIM), F32),
            jax.ShapeDtypeStruct((batch, 1, RWKV_PROJ), F32),
        ),
        grid=(batch, nt),
        in_specs=[
            pl.BlockSpec((tt, RWKV_PROJ), lambda b, t: (b * nt + t, 0)),
            pl.BlockSpec((1, 1, RWKV_PROJ), lambda b, t: (b, 0, 0)),
            pl.BlockSpec((1, RWKV_HEADS, RWKV_HEAD_DIM, RWKV_HEAD_DIM), lambda b, t: (b, 0, 0, 0)),
            full((1, RWKV_PROJ)),
            full((1, W)),
            full((RWKV_DECAY_RANK, W)),
            full((1, W)),
            full((RWKV_ICLR_RANK, W)),
            full((RWKV_GATE_RANK, W)),
            full((1, W)), full((1, W)), full((1, W)), full((1, W)), full((1, W)),
        ],
        out_specs=(
            pl.BlockSpec((tt, W), lambda b, t: (b * nt + t, 0)),
            pl.BlockSpec((1, RWKV_HEADS, RWKV_HEAD_DIM, RWKV_HEAD_DIM), lambda b, t: (b, 0, 0, 0)),
            pl.BlockSpec((1, 1, RWKV_PROJ), lambda b, t: (b, 0, 0)),
        ),
        scratch_shapes=[
            pltpu.VMEM((1, RWKV_PROJ), F32),
            pltpu.VMEM((RWKV_HEADS, RWKV_HEAD_DIM, RWKV_HEAD_DIM), F32),
        ] + [pltpu.VMEM((tt, W), F32) for _ in range(7)],
        compiler_params=_cparams("parallel", "arbitrary"),
        name="rwkv7_branch",
    )(P, shift0.reshape(batch, 1, RWKV_PROJ), S0,
      vec(lw['rwkv_mu']), vec(lw['rwkv_w0']), lw['rwkv_w_up'], vec(lw['rwkv_a0']), lw['rwkv_a_up'],
      lw['rwkv_g_up'], vec(lw['rwkv_k_k']), vec(lw['rwkv_k_a']), vec(lw['rwkv_r_k']),
      vec(lw['rwkv_ln_w']), vec(lw['rwkv_ln_b']))
    return y, s_out, sh_out.reshape(batch, RWKV_PROJ)


def _gla_kernel(g_ref, s0_ref, aup_ref, abias_ref, onorm_ref,
                y_ref, sout_ref,
                s_ref, la_s,
                *, t_real, tt, chunk, sub):
    t = pl.program_id(1)
    nt = pl.num_programs(1)
    KW, VW = GLA_K_WIDTH, GLA_V_WIDTH
    dk, dv = GLA_KEY_DIM, GLA_VALUE_DIM
    C = chunk
    cp = GLA_CHUNK_PASSES

    @pl.when(t == 0)
    def _():
        s_ref[...] = s0_ref[0]

    gl = g_ref[:, 2 * KW + 2 * VW:GLA_SECT]
    x = _mm(gl, aup_ref[...], 'nn', GATE_PASSES) + abias_ref[...]
    la = -_softplus(-x) * (1.0 / GLA_TAU)
    padded = t_real % tt != 0
    if padded:
        rowid = lax.broadcasted_iota(jnp.int32, (tt, 1), 0)
        valid = (t * tt + rowid) < t_real
        la = jnp.where(valid, la, 0.0)
    la_s[...] = la

    tri = jnp.where(_tril_ones(C), 1.0, 0.0).astype(BF16)
    ones_cv = jnp.ones((C, dv), BF16)
    low_incl = _tril_ones(C)
    crow = lax.broadcasted_iota(jnp.int32, (C, 1), 0)
    nsub = C // sub
    heads = range(GLA_HEADS)

    def chunk_body(c, carry):
        rows = pl.ds(pl.multiple_of(c * C, C), C)
        la_c = la_s[rows, :]
        cum_all = _cumsum_rows(la_c, tri)
        if padded:
            vmask = (t * tt + c * C + crow) < t_real
        q, k, v, cums = [], [], [], []
        for h in heads:
            q.append(g_ref[rows, h * dk:(h + 1) * dk] * (dk ** -0.5))
            k_h = g_ref[rows, KW + h * dk:KW + (h + 1) * dk]
            v_h = g_ref[rows, 2 * KW + h * dv:2 * KW + (h + 1) * dv]
            if padded:
                k_h = jnp.where(vmask, k_h, 0.0)
                v_h = jnp.where(vmask, v_h, 0.0)
            k.append(k_h)
            v.append(v_h)
            cums.append(cum_all[:, h * dk:(h + 1) * dk])
        bounds = [[jnp.zeros((1, dk), F32)] + [cm[i * sub - 1:i * sub, :] for i in range(1, nsub)] for cm in cums]
        qt = [q[h] * jnp.exp(cums[h] - jnp.concatenate(
            [jnp.broadcast_to(b, (sub, dk)) for b in bounds[h]], axis=0)) for h in heads]
        att = []
        for h in heads:
            att_rows = []
            for i in range(nsub):
                kt = k[h] * jnp.exp(jnp.where(crow < (i + 1) * sub, bounds[h][i] - cums[h], NEG_INF))
                att_rows.append(_mm(qt[h][i * sub:(i + 1) * sub], kt, 'nt', cp))
            att.append(jnp.where(low_incl, jnp.concatenate(att_rows, axis=0), 0.0))
        intra = [_mm(att[h], v[h], 'nn', cp) for h in heads]
        kv = [_mm(k[h] * jnp.exp(cums[h][C - 1:C, :] - cums[h]), v[h], 'tn', STATE_PASSES) for h in heads]
        tot = []
        for h in heads:
            hi, md, lo = _split3(la_c[:, h * dk:(h + 1) * dk])
            tot.append(_dot_tn(hi, ones_cv) + _dot_tn(md, ones_cv) + _dot_tn(lo, ones_cv))
        for h in heads:
            S = s_ref[h]
            o = intra[h] + _mm(q[h] * jnp.exp(cums[h]), S, 'nn', STATE_PASSES)
            s_ref[h] = S * jnp.exp(tot[h]) + kv[h]
            r_h = g_ref[rows, 2 * KW + VW + h * dv:2 * KW + VW + (h + 1) * dv]
            ms = jnp.mean(o * o, axis=-1, keepdims=True)
            on = o * lax.rsqrt(ms + NORM_EPS) * onorm_ref[...]
            y_ref[rows, h * dv:(h + 1) * dv] = (on * (r_h * _sigmoid(r_h))).astype(y_ref.dtype)
        return carry

    lax.fori_loop(0, tt // C, chunk_body, 0)

    @pl.when(t == nt - 1)
    def _():
        sout_ref[0] = s_ref[...]


def gla_branch(G, S0, lw, *, batch, t_pad, t_real, tt, chunk, sub, col_block=0):
    nt = t_pad // tt
    full = lambda shape: pl.BlockSpec(shape, lambda b, t: (0,) * len(shape))
    aup = jnp.zeros((GLA_GATE_PAD, GLA_K_WIDTH), F32).at[:GLA_GATE_RANK].set(lw['gla_a_up'])
    kern = functools.partial(_gla_kernel, t_real=t_real, tt=tt, chunk=chunk, sub=sub)
    y, s_out = pl.pallas_call(
        kern,
        out_shape=(
            jax.ShapeDtypeStruct((batch * t_pad, GLA_V_WIDTH), _act_dtype(tt)),
            jax.ShapeDtypeStruct((batch, GLA_HEADS, GLA_KEY_DIM, GLA_VALUE_DIM), F32),
        ),
        grid=(batch, nt),
        in_specs=[
            pl.BlockSpec((tt, PROJ_SLOT), lambda b, t: (b * nt + t, col_block)),
            pl.BlockSpec((1, GLA_HEADS, GLA_KEY_DIM, GLA_VALUE_DIM), lambda b, t: (b, 0, 0, 0)),
            full((GLA_GATE_PAD, GLA_K_WIDTH)),
            full((1, GLA_K_WIDTH)),
            full((1, GLA_VALUE_DIM)),
        ],
        out_specs=(
            pl.BlockSpec((tt, GLA_V_WIDTH), lambda b, t: (b * nt + t, 0)),
            pl.BlockSpec((1, GLA_HEADS, GLA_KEY_DIM, GLA_VALUE_DIM), lambda b, t: (b, 0, 0, 0)),
        ),
        scratch_shapes=[
            pltpu.VMEM((GLA_HEADS, GLA_KEY_DIM, GLA_VALUE_DIM), F32),
            pltpu.VMEM((tt, GLA_K_WIDTH), F32),
        ],
        compiler_params=_cparams("parallel", "arbitrary"),
        name="gla_branch",
    )(G, S0, aup, lw['gla_a_bias'].reshape(1, -1), lw['gla_o_norm'].reshape(1, -1))
    return y, s_out


def _norm_rope(x, g, cos, sin):
    ms = jnp.mean(x * x, axis=-1, keepdims=True)
    y = x * lax.rsqrt(ms + NORM_EPS) * g
    return y * cos + pltpu.roll(y, MOBA_HEAD_DIM // 2, axis=1) * sin


def _moba_prep_prompt_kernel(q_ref, k_ref, v_ref, cos_ref, sin_ref, gq_ref, gk_ref,
                             qo_ref, ko_ref, vo_ref, kb_ref, vt_ref, km_ref):
    Dh = MOBA_HEAD_DIM
    cos = cos_ref[...]
    sin = sin_ref[...]
    rows = q_ref.shape[0]
    vo_ref[...] = v_ref[...]
    for h in range(MOBA_HEADS):
        sl = slice(h * Dh, (h + 1) * Dh)
        qo_ref[:, sl] = _norm_rope(q_ref[:, sl], gq_ref[...], cos, sin)
        kr = _norm_rope(k_ref[:, sl], gk_ref[...], cos, sin)
        ko_ref[:, sl] = kr
        kb_ref[:, sl] = kr.astype(BF16)
        km_ref[0, :, sl] = jnp.sum(kr, axis=0, keepdims=True) * (1.0 / rows)
        vt_ref[0, h, 0, 0:Dh, :] = v_ref[:, sl].T.astype(BF16)
        vt_ref[0, h, 0, Dh:VT_ROWS, :] = jnp.ones((VT_ROWS - Dh, rows), BF16)


def moba_prep_prompt(M, cos, sin, gq, gk, *, batch, seq, col0=0):
    m = M.shape[0]
    W = MOBA_WIDTH
    blk = MOBA_BLOCK
    Dh = MOBA_HEAD_DIM
    nq = seq // blk
    nblk = m // blk
    return pl.pallas_call(
        _moba_prep_prompt_kernel,
        out_shape=(
            jax.ShapeDtypeStruct((m, W), F32),
            jax.ShapeDtypeStruct((m, W), F32),
            jax.ShapeDtypeStruct((m, W), F32),
            jax.ShapeDtypeStruct((m, W), BF16),
            jax.ShapeDtypeStruct((batch, MOBA_HEADS, nq, VT_ROWS, blk), BF16),
            jax.ShapeDtypeStruct((nblk, 1, W), F32),
        ),
        grid=(nblk,),
        in_specs=[
            pl.BlockSpec((blk, W), lambda i: (i, col0)),
            pl.BlockSpec((blk, W), lambda i: (i, col0 + 1)),
            pl.BlockSpec((blk, W), lambda i: (i, col0 + 2)),
            pl.BlockSpec((blk, Dh), lambda i: (i % nq, 0)),
            pl.BlockSpec((blk, Dh), lambda i: (i % nq, 0)),
            pl.BlockSpec((1, Dh), lambda i: (0, 0)),
            pl.BlockSpec((1, Dh), lambda i: (0, 0)),
        ],
        out_specs=(
            pl.BlockSpec((blk, W), lambda i: (i, 0)),
            pl.BlockSpec((blk, W), lambda i: (i, 0)),
            pl.BlockSpec((blk, W), lambda i: (i, 0)),
            pl.BlockSpec((blk, W), lambda i: (i, 0)),
            pl.BlockSpec((1, MOBA_HEADS, 1, VT_ROWS, blk), lambda i: (i // nq, 0, i % nq, 0, 0)),
            pl.BlockSpec((1, 1, W), lambda i: (i, 0, 0)),
        ),
        compiler_params=_cparams("parallel"),
        name="moba_qkv_prep",
    )(M, M, M, cos, sin, gq.reshape(1, -1), gk.reshape(1, -1))


def _moba_prep_sample_kernel(q_ref, k_ref, cos_ref, sin_ref, gq_ref, gk_ref, qo_ref, ko_ref):
    Dh = MOBA_HEAD_DIM
    cos = cos_ref[...]
    sin = sin_ref[...]
    for h in range(MOBA_HEADS):
        sl = slice(h * Dh, (h + 1) * Dh)
        qo_ref[:, sl] = _norm_rope(q_ref[:, sl], gq_ref[...], cos, sin)
        ko_ref[:, sl] = _norm_rope(k_ref[:, sl], gk_ref[...], cos, sin)


def moba_prep_sample(M, cos, sin, gq, gk, *, col0=0):
    m = M.shape[0]
    W = MOBA_WIDTH
    Dh = MOBA_HEAD_DIM
    return pl.pallas_call(
        _moba_prep_sample_kernel,
        out_shape=(jax.ShapeDtypeStruct((m, W), F32), jax.ShapeDtypeStruct((m, W), F32)),
        grid=(1,),
        in_specs=[
            pl.BlockSpec((m, W), lambda i: (0, col0)),
            pl.BlockSpec((m, W), lambda i: (0, col0 + 1)),
            pl.BlockSpec((m, Dh), lambda i: (0, 0)),
            pl.BlockSpec((m, Dh), lambda i: (0, 0)),
            pl.BlockSpec((1, Dh), lambda i: (0, 0)),
            pl.BlockSpec((1, Dh), lambda i: (0, 0)),
        ],
        out_specs=(pl.BlockSpec((m, W), lambda i: (0, 0)), pl.BlockSpec((m, W), lambda i: (0, 0))),
        compiler_params=_cparams("arbitrary"),
        name="moba_qk_prep_sample",
    )(M, M, cos, sin, gq.reshape(1, -1), gk.reshape(1, -1))


def _block_rank(bs, n_valid, axis):
    nb = bs.shape[axis]
    idx = lax.broadcasted_iota(jnp.int32, bs.shape, axis)
    rank = jnp.zeros(bs.shape, jnp.int32)
    for mm in range(nb):
        one = bs[mm:mm + 1, :] if axis == 0 else bs[:, mm:mm + 1]
        beats = (one > bs) | ((one == bs) & (idx > mm))
        if n_valid is not None:
            beats = beats & (mm < n_valid)
        rank = rank + jnp.where(beats, 1, 0)
    return rank


def _moba_attn_kernel(q_ref, kb_ref, vt_ref, km_ref, o_ref, sel_ref, s_ref):
    i = pl.program_id(2)
    blk = MOBA_BLOCK
    Dh = MOBA_HEAD_DIM
    heads = range(MOBA_HEADS_PER_STEP)
    hs = [slice(h * Dh, (h + 1) * Dh) for h in heads]
    nb = km_ref.shape[1]
    grp = next(g for g in (4, 2, 1) if nb % g == 0)
    c2 = (Dh ** -0.5) * LOG2_E
    blk_id = lax.broadcasted_iota(jnp.int32, (nb, blk), 0)
    causal = (lax.broadcasted_iota(jnp.int32, (blk, blk), 0)
              <= lax.broadcasted_iota(jnp.int32, (blk, blk), 1))
    n_groups = (i + grp - 1) // grp

    qT = [q_ref[:, s].T for s in hs]
    bsT = [_dot(km_ref[0, :, s], qT[h], HIGHEST) for h, s in zip(heads, hs)]
    for h in heads:
        rank = _block_rank(bsT[h], i, 0)
        sel_ref[h] = jnp.where((blk_id < i) & (rank < MOBA_TOPK), 1.0, 0.0)
    qb = [x.astype(BF16) for x in qT]

    own = pl.ds(pl.multiple_of(i * blk, blk), blk)
    s_own = [jnp.where(causal, _dot(kb_ref[0, own, s], qb[h]) * c2, NEG_INF) for h, s in zip(heads, hs)]
    for h in heads:
        s_ref[h, nb * blk:(nb + 1) * blk, :] = s_own[h]

    def score_group(gi, m):
        m = list(m)
        for u in range(grp):
            j = gi * grp + u
            rows = pl.ds(pl.multiple_of(j * blk, blk), blk)
            raw = [_dot(kb_ref[0, rows, s], qb[h]) for h, s in zip(heads, hs)]
            for h in heads:
                picked = sel_ref[h, pl.ds(j, 1), :] > 0.0
                sc = jnp.where(picked, raw[h] * c2, NEG_INF)
                s_ref[h, rows, :] = sc
                m[h] = jnp.maximum(m[h], jnp.max(sc, axis=0, keepdims=True))
        return tuple(m)

    m = lax.fori_loop(0, n_groups, score_group, tuple(jnp.max(x, axis=0, keepdims=True) for x in s_own))

    def weighted_values(j, rows):
        p = [jnp.exp2(s_ref[h, rows, :] - m[h]).astype(BF16) for h in heads]
        return [_dot(vt_ref[0, h, j], p[h]) for h in heads]

    def value_group(gi, acc):
        acc = list(acc)
        for u in range(grp):
            j = gi * grp + u
            new = weighted_values(j, pl.ds(pl.multiple_of(j * blk, blk), blk))
            acc = [a + n for a, n in zip(acc, new)]
        return tuple(acc)

    acc = lax.fori_loop(0, n_groups, value_group, tuple(weighted_values(i, pl.ds(nb * blk, blk))))
    for h, s in zip(heads, hs):
        o_ref[:, s] = (acc[h][0:Dh] / acc[h][Dh:Dh + 1]).T.astype(o_ref.dtype)


def moba_attention_prompt(q_rot, kb, vt, kmean, *, batch, seq):
    W = MOBA_WIDTH
    blk = MOBA_BLOCK
    hp = MOBA_HEADS_PER_STEP
    wide = hp * MOBA_HEAD_DIM
    nq = seq // blk
    nb = kmean.shape[1]
    return pl.pallas_call(
        _moba_attn_kernel,
        out_shape=jax.ShapeDtypeStruct((batch * seq, W), BF16),
        grid=(batch, MOBA_HEADS // hp, nq),
        in_specs=[
            pl.BlockSpec((blk, wide), lambda b, h, i: (b * nq + i, h)),
            pl.BlockSpec((1, seq, wide), lambda b, h, i: (b, 0, h)),
            pl.BlockSpec((1, hp, nb, VT_ROWS, blk), lambda b, h, i: (b, h, 0, 0, 0)),
            pl.BlockSpec((1, nb, wide), lambda b, h, i: (b, 0, h)),
        ],
        out_specs=pl.BlockSpec((blk, wide), lambda b, h, i: (b * nq + i, h)),
        scratch_shapes=[pltpu.VMEM((hp, nb, blk), F32), pltpu.VMEM((hp, (nb + 1) * blk, blk), F32)],
        compiler_params=_cparams("parallel", "parallel", "arbitrary"),
        name="moba_attention",
    )(q_rot, kb.reshape(batch, seq, W), vt, kmean)


PAGES_PER_STEP = 16
PAGES_PER_BLOCK = MOBA_BLOCK // PAGE_SIZE


def _moba_select_kernel(q_ref, km_ref, o_ref):
    Dh = MOBA_HEAD_DIM
    rows = []
    for h in range(MOBA_HEADS):
        rows.append(_dot_nt(q_ref[0, :, h * Dh:(h + 1) * Dh], km_ref[0, h], HIGHEST))
    bs = jnp.concatenate(rows, axis=0)
    nb = bs.shape[1]
    rank = _block_rank(bs, None, 1)
    lane = lax.broadcasted_iota(jnp.int32, bs.shape, 1)
    olane = lax.broadcasted_iota(jnp.int32, (MOBA_HEADS, 128), 1)
    out = jnp.zeros((MOBA_HEADS, 128), jnp.int32)
    for s in range(min(MOBA_TOPK, nb)):
        idx = jnp.sum(jnp.where(rank == s, lane, 0), axis=-1, keepdims=True)
        out = jnp.where(olane == s, idx, out)
    o_ref[0] = out


def moba_select_sample(q3, kmean_past):
    bs, _, W = q3.shape
    nb = kmean_past.shape[2]
    return pl.pallas_call(
        _moba_select_kernel,
        out_shape=jax.ShapeDtypeStruct((bs, MOBA_HEADS, 128), jnp.int32),
        grid=(bs,),
        in_specs=[
            pl.BlockSpec((1, 1, W), lambda b: (b, 0, 0)),
            pl.BlockSpec((1, MOBA_HEADS, nb, MOBA_HEAD_DIM), lambda b: (b, 0, 0, 0)),
        ],
        out_specs=pl.BlockSpec((1, MOBA_HEADS, 128), lambda b: (b, 0, 0)),
        compiler_params=_cparams("parallel"),
        name="moba_select_sample",
    )(q3, kmean_past)


def _moba_decode_kernel(sel_ref, pt_ref, q_ref, kn_ref, vn_ref, ck_ref, cv_ref, o_ref, kbuf, vbuf, sem,
                        *, layer, n_pg, n_sel, n_pages):
    step = pl.program_id(0)
    scale = MOBA_HEAD_DIM ** -0.5

    def page_copies(st, slot):
        b = st // MOBA_HEADS
        h = st % MOBA_HEADS
        copies = []
        for e in range(n_pg):
            blk = sel_ref[st * n_sel + e // PAGES_PER_BLOCK]
            page = pt_ref[b * n_pages + blk * PAGES_PER_BLOCK + e % PAGES_PER_BLOCK]
            copies.append(pltpu.make_async_copy(ck_ref.at[layer, page, :, h, :], kbuf.at[slot, e], sem.at[slot, 0, e]))
            copies.append(pltpu.make_async_copy(cv_ref.at[layer, page, :, h, :], vbuf.at[slot, e], sem.at[slot, 1, e]))
        return copies

    @pl.when(step == 0)
    def _():
        for c in page_copies(0, 0):
            c.start()

    @pl.when(step + 1 < pl.num_programs(0))
    def _():
        for c in page_copies(step + 1, (step + 1) % 2):
            c.start()

    slot = step % 2
    for c in page_copies(step, slot):
        c.wait()

    rows = n_pg * PAGE_SIZE
    q = q_ref[0]
    s_own = jnp.sum(q * kn_ref[0], axis=-1, keepdims=True) * scale
    kp = kbuf[slot].reshape(rows, MOBA_HEAD_DIM).astype(BF16)
    vp = vbuf[slot].reshape(rows, MOBA_HEAD_DIM).astype(BF16)
    sc = _dot_nt(q.astype(BF16), kp) * scale
    m = jnp.maximum(s_own, jnp.max(sc, axis=-1, keepdims=True))
    p_own = jnp.exp(s_own - m)
    p = jnp.exp(sc - m)
    l = p_own + jnp.sum(p, axis=-1, keepdims=True)
    acc = p_own * vn_ref[0] + _dot(p.astype(BF16), vp)
    o_ref[0] = (acc / l).astype(o_ref.dtype)


def moba_decode(sel_flat, pt_flat, q3, k3, M3, cache_k, cache_v, *, layer, n_pages, n_sel, v_col):
    bs, _, W = q3.shape
    Dh = MOBA_HEAD_DIM
    H = MOBA_HEADS
    n_pg = n_sel * PAGES_PER_BLOCK
    vec = lambda col: pl.BlockSpec((1, 1, Dh), lambda st, sel, pt: (st // H, 0, col(st % H)))
    hbm = pl.BlockSpec(memory_space=pl.ANY)
    return pl.pallas_call(
        functools.partial(_moba_decode_kernel, layer=layer, n_pg=n_pg, n_sel=n_sel, n_pages=n_pages),
        out_shape=jax.ShapeDtypeStruct((bs, 1, W), F32),
        grid_spec=pltpu.PrefetchScalarGridSpec(
            num_scalar_prefetch=2,
            grid=(bs * H,),
            in_specs=[vec(lambda h: h), vec(lambda h: h), vec(lambda h: v_col // Dh + h), hbm, hbm],
            out_specs=vec(lambda h: h),
            scratch_shapes=[
                pltpu.VMEM((2, n_pg, PAGE_SIZE, Dh), F32),
                pltpu.VMEM((2, n_pg, PAGE_SIZE, Dh), F32),
                pltpu.SemaphoreType.DMA((2, 2, n_pg)),
            ],
        ),
        compiler_params=_cparams("arbitrary"),
        name="moba_decode",
    )(sel_flat, pt_flat, q3, k3, M3, cache_k, cache_v)


def _merge_kernel(ya_ref, yb_ref, yc_ref, wa_ref, wb_ref, wc_ref, z0_ref, z1_ref, z2_ref, o_ref):
    acc = _sigmoid(z0_ref[...]) * _dot(ya_ref[...].astype(BF16), wa_ref[...])
    acc = acc + _sigmoid(z1_ref[...]) * _dot(yb_ref[...].astype(BF16), wb_ref[...])
    acc = acc + _sigmoid(z2_ref[...]) * _dot(yc_ref[...].astype(BF16), wc_ref[...])
    o_ref[...] = acc.astype(o_ref.dtype)


def gated_merge(ya, yb, yc, wa, wb, wc, proj, layer, *, z_col, tn=PROJ_TILE, tm_pref=512):
    m = ya.shape[0]
    d = wa.shape[2]
    tm = _row_tile(m, tm_pref)
    nj = d // tn
    z0 = z_col // tn
    wspec = lambda w: pl.BlockSpec((None, w.shape[1], tn), lambda i, j: (layer, 0, j))
    return pl.pallas_call(
        _merge_kernel,
        out_shape=jax.ShapeDtypeStruct((m, d), BF16),
        grid=(m // tm, nj),
        in_specs=[
            pl.BlockSpec((tm, ya.shape[1]), lambda i, j: (i, 0)),
            pl.BlockSpec((tm, yb.shape[1]), lambda i, j: (i, 0)),
            pl.BlockSpec((tm, yc.shape[1]), lambda i, j: (i, 0)),
            wspec(wa), wspec(wb), wspec(wc),
            pl.BlockSpec((tm, tn), lambda i, j: (i, z0 + j)),
            pl.BlockSpec((tm, tn), lambda i, j: (i, z0 + nj + j)),
            pl.BlockSpec((tm, tn), lambda i, j: (i, z0 + 2 * nj + j)),
        ],
        out_specs=pl.BlockSpec((tm, tn), lambda i, j: (i, j)),
        compiler_params=_cparams("parallel", "arbitrary"),
        name="gated_merge",
    )(ya, yb, yc, wa, wb, wc, proj, proj, proj)


def _matmul_residual_kernel(a_ref, w_ref, x_ref, o_ref):
    o_ref[...] = x_ref[...] + _dot(a_ref[...], w_ref[...])


def matmul_residual(a, w, x, layer, *, tn, tm_pref=512):
    m, k = a.shape
    n = w.shape[2]
    tm = _row_tile(m, tm_pref)
    return pl.pallas_call(
        _matmul_residual_kernel,
        out_shape=jax.ShapeDtypeStruct((m, n), F32),
        grid=(m // tm, n // tn),
        in_specs=[
            pl.BlockSpec((tm, k), lambda i, j: (i, 0)),
            pl.BlockSpec((None, k, tn), lambda i, j: (layer, 0, j)),
            pl.BlockSpec((tm, tn), lambda i, j: (i, j)),
        ],
        out_specs=pl.BlockSpec((tm, tn), lambda i, j: (i, j)),
        compiler_params=_cparams("parallel", "arbitrary"),
        name="matmul_residual",
    )(a, w, x)


def _ffn_in_kernel(x_ref, g_ref, wg_ref, wv_ref, o_ref, h_ref):
    @pl.when(pl.program_id(1) == 0)
    def _():
        x = x_ref[...]
        ms = jnp.mean(x * x, axis=-1, keepdims=True)
        h_ref[...] = (x * lax.rsqrt(ms + NORM_EPS) * g_ref[...]).astype(BF16)

    h = h_ref[...]
    gate = _dot(h, wg_ref[...])
    val = _dot(h, wv_ref[...])
    o_ref[...] = (gate * _sigmoid(gate) * val).astype(o_ref.dtype)


def _ffn_in_key_means_kernel(pt_ref, x_ref, g_ref, wg_ref, wv_ref, *refs, n_pg, page_steps):
    del pt_ref
    page_refs, (o_ref, km_ref, h_ref) = refs[:n_pg], refs[n_pg:]
    _ffn_in_kernel(x_ref, g_ref, wg_ref, wv_ref, o_ref, h_ref)

    @pl.when(pl.program_id(0) * pl.num_programs(1) + pl.program_id(1) < page_steps)
    def _():
        for n in range(n_pg // PAGES_PER_BLOCK):
            tot = page_refs[PAGES_PER_BLOCK * n][0, 0].sum(axis=0)
            for e in range(1, PAGES_PER_BLOCK):
                tot = tot + page_refs[PAGES_PER_BLOCK * n + e][0, 0].sum(axis=0)
            tot = tot * (1.0 / MOBA_BLOCK)
            for hd in range(MOBA_HEADS):
                km_ref[0, hd, pl.ds(n, 1), :] = tot[hd:hd + 1, :]


def ffn_in(x, g, w, layer, *, tn=512, tm_pref=1024):
    m, d = x.shape
    hidden = w.shape[2] // 2
    tm = _row_tile(m, tm_pref)
    nj = hidden // tn
    return pl.pallas_call(
        _ffn_in_kernel,
        out_shape=jax.ShapeDtypeStruct((m, hidden), BF16),
        grid=(m // tm, nj),
        in_specs=[
            pl.BlockSpec((tm, d), lambda i, j: (i, 0)),
            pl.BlockSpec((1, d), lambda i, j: (0, 0)),
            pl.BlockSpec((None, d, tn), lambda i, j: (layer, 0, j)),
            pl.BlockSpec((None, d, tn), lambda i, j: (layer, 0, nj + j)),
        ],
        out_specs=pl.BlockSpec((tm, tn), lambda i, j: (i, j)),
        scratch_shapes=[pltpu.VMEM((tm, d), BF16)],
        compiler_params=_cparams("parallel", "arbitrary"),
        name="ffn_in_swiglu",
    )(x, g.reshape(1, d), w, w)


def ffn_in_key_means(x, g, w, layer, page_table, cache_k, *, tn=512, tm_pref=1024):
    m, d = x.shape
    hidden = w.shape[2] // 2
    tm = _row_tile(m, tm_pref)
    nj = hidden // tn
    bs, n_pages = page_table.shape
    pps = min(PAGES_PER_STEP, n_pages)
    per_seq = n_pages // pps
    page_steps = bs * per_seq
    assert n_pages % pps == 0 and page_steps <= (m // tm) * nj

    def page_pos(i, j):
        t = jnp.minimum(i * nj + j, page_steps - 1)
        return t // per_seq, t % per_seq

    def page_spec(e):
        def page_map(i, j, pt):
            b, s = page_pos(i, j)
            return (layer, pt[b * n_pages + s * pps + e], 0, 0, 0)
        return pl.BlockSpec((1, 1, PAGE_SIZE, MOBA_HEADS, MOBA_HEAD_DIM), page_map)

    def km_map(i, j, pt):
        b, s = page_pos(i, j)
        return (b, 0, s, 0)

    return pl.pallas_call(
        functools.partial(_ffn_in_key_means_kernel, n_pg=pps, page_steps=page_steps),
        out_shape=(
            jax.ShapeDtypeStruct((m, hidden), BF16),
            jax.ShapeDtypeStruct((bs, MOBA_HEADS, n_pages // PAGES_PER_BLOCK, MOBA_HEAD_DIM), F32),
        ),
        grid_spec=pltpu.PrefetchScalarGridSpec(
            num_scalar_prefetch=1,
            grid=(m // tm, nj),
            in_specs=[
                pl.BlockSpec((tm, d), lambda i, j, pt: (i, 0)),
                pl.BlockSpec((1, d), lambda i, j, pt: (0, 0)),
                pl.BlockSpec((None, d, tn), lambda i, j, pt: (layer, 0, j)),
                pl.BlockSpec((None, d, tn), lambda i, j, pt: (layer, 0, nj + j)),
            ] + [page_spec(e) for e in range(pps)],
            out_specs=(
                pl.BlockSpec((tm, tn), lambda i, j, pt: (i, j)),
                pl.BlockSpec((1, MOBA_HEADS, pps // PAGES_PER_BLOCK, MOBA_HEAD_DIM), km_map),
            ),
            scratch_shapes=[pltpu.VMEM((tm, d), BF16)],
        ),
        compiler_params=_cparams("arbitrary", "arbitrary"),
        name="ffn_in_swiglu_key_means",
    )(page_table.reshape(-1), x, g.reshape(1, d), w, w, *([cache_k] * pps))


def _rope_tables(pos):
    half = MOBA_HEAD_DIM // 2
    inv = ROPE_THETA ** (-jnp.arange(half, dtype=F32) / half)
    ang = pos.astype(F32)[:, None] * inv[None, :]
    cos, sin = jnp.cos(ang), jnp.sin(ang)
    return jnp.concatenate([cos, cos], axis=-1), jnp.concatenate([-sin, sin], axis=-1)


def _bf16_weights(w):
    o1 = RWKV_PROJ
    o2 = o1 + 3 * MOBA_WIDTH
    o3 = o2 + 2 * GLA_K_WIDTH + GLA_V_WIDTH
    o4 = o3 + GLA_GATE_RANK
    o5 = o4 + GLA_V_WIDTH
    w_in = w['w_in'].astype(BF16)
    total = COL_GATES + w_in.shape[-1] - o5

    def place(lo, hi, at):
        return jnp.pad(w_in[..., lo:hi], ((0, 0), (0, 0), (at, total - at - (hi - lo))))

    packed = (place(0, o1, COL_RWKV)
              + place(o2, o3, COL_GLA) + place(o4, o5, COL_GLA + o3 - o2)
              + place(o3, o4, COL_GLA + o3 - o2 + o5 - o4)
              + place(o1, o2, COL_MOBA) + place(o5, w_in.shape[-1], COL_GATES))
    out = {k: w[k].astype(BF16) for k in ('w_up_rwkv', 'w_up_moba', 'w_up_gla', 'w_out', 'w_ffn_in', 'w_ffn_out')}
    out['w_in'] = packed
    return out


def _mix_and_ffn(x, ya, yb, yc, proj, lw, wb, layer, paged_keys=None):
    merged = gated_merge(ya, yb, yc, wb['w_up_rwkv'], wb['w_up_moba'], wb['w_up_gla'], proj, layer,
                         z_col=COL_GATES, tn=512, tm_pref=1024)
    x = matmul_residual(merged, wb['w_out'], x, layer, tn=1024, tm_pref=1024)
    if paged_keys is None:
        act, kmean = ffn_in(x, lw['norm_ffn'], wb['w_ffn_in'], layer), None
    else:
        act, kmean = ffn_in_key_means(x, lw['norm_ffn'], wb['w_ffn_in'], layer, *paged_keys)
    return matmul_residual(act, wb['w_ffn_out'], x, layer, tn=512, tm_pref=1024), kmean


def _prompt_layer(x, lw, wb, rope, layer, paged_keys, *, batch, seq):
    proj = norm_matmul(x, lw['norm_mix'], wb['w_in'], layer)

    tt = min(seq, 256)
    ya, r_S, shift = rwkv_branch(
        proj, jnp.zeros((batch, RWKV_PROJ), F32),
        jnp.zeros((batch, RWKV_HEADS, RWKV_HEAD_DIM, RWKV_HEAD_DIM), F32), lw,
        batch=batch, t_pad=seq, t_real=seq, tt=tt, chunk=min(tt, RWKV_CHUNK))
    yc, g_S = gla_branch(
        proj, jnp.zeros((batch, GLA_HEADS, GLA_KEY_DIM, GLA_VALUE_DIM), F32), lw,
        batch=batch, t_pad=seq, t_real=seq, tt=tt, chunk=min(tt, 128), sub=16, col_block=COL_GLA // PROJ_SLOT)

    nq = seq // MOBA_BLOCK
    q_rot, k_rot, v_out, kb, vt, kmean = moba_prep_prompt(
        proj, rope[0], rope[1], lw['moba_q_norm'], lw['moba_k_norm'],
        batch=batch, seq=seq, col0=COL_MOBA // MOBA_WIDTH)
    yb = moba_attention_prompt(q_rot, kb, vt, kmean.reshape(batch, nq, MOBA_WIDTH), batch=batch, seq=seq)

    x, kmean_past = _mix_and_ffn(x, ya, yb, yc, proj, lw, wb, layer, paged_keys)
    k_new = k_rot.reshape(batch, seq, MOBA_HEADS, MOBA_HEAD_DIM)
    v_new = v_out.reshape(batch, seq, MOBA_HEADS, MOBA_HEAD_DIM)
    return (x, k_new, v_new, r_S, shift, g_S), kmean_past


def _sample_layer(x, lw, wb, rope, shift0, rwkv_S0, gla_S0, kmean_past, pt_flat, cache_k, cache_v,
                  *, layer, n_pages):
    bs = x.shape[0]
    proj = norm_matmul(x, lw['norm_mix'], wb['w_in'], layer)

    pad_rows = lambda a: jnp.pad(a[:, None, :], ((0, 0), (0, SUBLANES - 1), (0, 0))).reshape(bs * SUBLANES, -1)
    ya, r_S, shift = rwkv_branch(pad_rows(proj[:, COL_RWKV:COL_RWKV + RWKV_PROJ]), shift0, rwkv_S0, lw,
                                 batch=bs, t_pad=SUBLANES, t_real=1, tt=SUBLANES, chunk=SUBLANES)
    yc, g_S = gla_branch(pad_rows(proj[:, COL_GLA:COL_GLA + PROJ_SLOT]), gla_S0, lw,
                         batch=bs, t_pad=SUBLANES, t_real=1, tt=SUBLANES, chunk=SUBLANES, sub=SUBLANES)
    ya = ya[::SUBLANES]
    yc = yc[::SUBLANES]

    q_rot, k_rot = moba_prep_sample(proj, rope[0], rope[1], lw['moba_q_norm'], lw['moba_k_norm'],
                                    col0=COL_MOBA // MOBA_WIDTH)
    q3 = q_rot.reshape(bs, 1, MOBA_WIDTH)
    n_sel = min(MOBA_TOPK, kmean_past.shape[2])
    sel = moba_select_sample(q3, kmean_past)[:, :, :n_sel].reshape(-1)
    v_col = COL_MOBA + 2 * MOBA_WIDTH
    yb = moba_decode(sel, pt_flat, q3, k_rot.reshape(bs, 1, MOBA_WIDTH), proj.reshape(bs, 1, -1),
                     cache_k, cache_v, layer=layer, n_pages=n_pages, n_sel=n_sel, v_col=v_col).reshape(bs, MOBA_WIDTH)

    x, _ = _mix_and_ffn(x, ya, yb, yc, proj, lw, wb, layer)
    k_new = k_rot.reshape(bs, 1, MOBA_HEADS, MOBA_HEAD_DIM)
    v_new = proj[:, v_col:COL_GATES].reshape(bs, 1, MOBA_HEADS, MOBA_HEAD_DIM)
    return x, k_new, v_new, r_S, shift, g_S


_LAYER_KEYS = ('norm_mix', 'w_in', 'rwkv_mu', 'rwkv_w0', 'rwkv_w_up', 'rwkv_a0', 'rwkv_a_up', 'rwkv_g_up',
               'rwkv_k_k', 'rwkv_k_a', 'rwkv_r_k', 'rwkv_ln_w', 'rwkv_ln_b', 'moba_q_norm', 'moba_k_norm',
               'gla_a_up', 'gla_a_bias', 'gla_o_norm', 'w_up_rwkv', 'w_up_moba', 'w_up_gla', 'w_out',
               'norm_ffn', 'w_ffn_in', 'w_ffn_out')


def kernel(x_prompt, x_sample, cache_k, cache_v, page_table, state_rwkv, state_rwkv_shift, state_gla, norm_mix, w_in, rwkv_mu, rwkv_w0, rwkv_w_up, rwkv_a0, rwkv_a_up, rwkv_g_up, rwkv_k_k, rwkv_k_a, rwkv_r_k, rwkv_ln_w, rwkv_ln_b, moba_q_norm, moba_k_norm, gla_a_up, gla_a_bias, gla_o_norm, w_up_rwkv, w_up_moba, w_up_gla, w_out, norm_ffn, w_ffn_in, w_ffn_out):
    stacked = dict(zip(_LAYER_KEYS, (
        norm_mix, w_in, rwkv_mu, rwkv_w0, rwkv_w_up, rwkv_a0, rwkv_a_up, rwkv_g_up, rwkv_k_k, rwkv_k_a,
        rwkv_r_k, rwkv_ln_w, rwkv_ln_b, moba_q_norm, moba_k_norm, gla_a_up, gla_a_bias, gla_o_norm,
        w_up_rwkv, w_up_moba, w_up_gla, w_out, norm_ffn, w_ffn_in, w_ffn_out)))
    depth = w_in.shape[0]
    bp, seq, d = x_prompt.shape
    bs, dec_seq, _ = x_sample.shape
    n_pages = page_table.shape[1]
    past_len = n_pages * cache_k.shape[2]
    assert dec_seq == 1 and cache_k.shape[2:] == (PAGE_SIZE, MOBA_HEADS, MOBA_HEAD_DIM)
    assert seq % MOBA_BLOCK == 0 and past_len % MOBA_BLOCK == 0 and past_len // MOBA_BLOCK >= MOBA_TOPK

    pt_flat = page_table.reshape(-1)

    rope_p = _rope_tables(jnp.arange(seq))
    rope_s = _rope_tables(jnp.full((bs,), past_len))

    yp = x_prompt.reshape(bp * seq, d)
    ys = x_sample.reshape(bs, d)
    wb = _bf16_weights(stacked)
    small = [k for k in _LAYER_KEYS if k not in wb]
    outs_p, outs_s = [], []
    for l in range(depth):
        lw = {k: stacked[k][l] for k in small}
        (yp, *rest_p), kmean_past = _prompt_layer(yp, lw, wb, rope_p, l, (page_table, cache_k), batch=bp, seq=seq)
        outs_p.append(rest_p)
        ys, *rest_s = _sample_layer(ys, lw, wb, rope_s, state_rwkv_shift[l], state_rwkv[l], state_gla[l],
                                    kmean_past, pt_flat, cache_k, cache_v, layer=l, n_pages=n_pages)
        outs_s.append(rest_s)
    stack = lambda outs, i: jnp.stack([o[i] for o in outs])
    return (yp.reshape(bp, seq, d), ys.reshape(bs, 1, d),
            stack(outs_p, 0), stack(outs_p, 1), stack(outs_p, 2), stack(outs_p, 3), stack(outs_p, 4),
            stack(outs_s, 0), stack(outs_s, 1), stack(outs_s, 2), stack(outs_s, 3), stack(outs_s, 4))
```

```python
import functools

import jax
import jax.numpy as jnp
from jax import lax
from jax.experimental import pallas as pl
from jax.experimental.pallas import tpu as pltpu

F32 = jnp.float32
BF16 = jnp.bfloat16
HIGHEST = lax.Precision.HIGHEST

PAGE_SIZE = 128
RWKV_HEADS = 8
RWKV_HEAD_DIM = 64
RWKV_WIDTH = RWKV_HEADS * RWKV_HEAD_DIM
RWKV_DECAY_RANK = 64
RWKV_ICLR_RANK = 64
RWKV_GATE_RANK = 128
RWKV_PROJ = 3 * RWKV_WIDTH + RWKV_DECAY_RANK + RWKV_ICLR_RANK + RWKV_GATE_RANK
RWKV_GN_EPS = 64e-5
MOBA_HEADS = 8
MOBA_HEAD_DIM = 128
MOBA_WIDTH = MOBA_HEADS * MOBA_HEAD_DIM
MOBA_BLOCK = 256
MOBA_TOPK = 3
ROPE_THETA = 10000.0
GLA_HEADS = 4
GLA_KEY_DIM = 64
GLA_VALUE_DIM = 128
GLA_K_WIDTH = GLA_HEADS * GLA_KEY_DIM
GLA_V_WIDTH = GLA_HEADS * GLA_VALUE_DIM
GLA_GATE_RANK = 16
GLA_GATE_PAD = 256
GLA_SECT = 2 * GLA_K_WIDTH + 2 * GLA_V_WIDTH + GLA_GATE_PAD
GLA_TAU = 16.0
PROJ_SLOT = 2048
PROJ_TILE = 1024
COL_RWKV = 0
COL_GLA = PROJ_SLOT
COL_MOBA = 2 * PROJ_SLOT
COL_GATES = COL_MOBA + 3 * MOBA_WIDTH
N_BRANCHES = 3
NORM_EPS = 1e-6
NEG_INF = -1e30
LOG2_E = 1.4426950408889634
VT_ROWS = MOBA_HEAD_DIM + 16
MOBA_HEADS_PER_STEP = 4

VMEM_LIMIT_BYTES = 56 * 1024 * 1024
SUBLANES = 8

GATE_PASSES = 3
STATE_PASSES = 3
RWKV_CHUNK = 128
RWKV_CHUNK_PASSES = 1
GLA_CHUNK_PASSES = 1


def _cparams(*sem):
    return pltpu.CompilerParams(dimension_semantics=sem, vmem_limit_bytes=VMEM_LIMIT_BYTES)


def _dot(a, b, precision=None):
    return jnp.dot(a, b, preferred_element_type=F32, precision=precision)


def _dot_nt(a, b, precision=None):
    return lax.dot_general(a, b, (((1,), (1,)), ((), ())), preferred_element_type=F32, precision=precision)


def _dot_tn(a, b, precision=None):
    return lax.dot_general(a, b, (((0,), (0,)), ((), ())), preferred_element_type=F32, precision=precision)


def _mm(a, b, kind, passes):
    f = {'nn': _dot, 'nt': _dot_nt, 'tn': _dot_tn}[kind]
    if passes == 6:
        return f(a, b, HIGHEST)
    ah = a.astype(BF16)
    bh = b.astype(BF16)
    if passes == 1:
        return f(ah, bh)
    al = (a - ah.astype(F32)).astype(BF16)
    bl = (b - bh.astype(F32)).astype(BF16)
    return f(ah, bh) + f(ah, bl) + f(al, bh)


def _split3(x):
    hi = x.astype(BF16)
    r1 = x - hi.astype(F32)
    mid = r1.astype(BF16)
    lo = (r1 - mid.astype(F32)).astype(BF16)
    return hi, mid, lo


def _tril_ones(n, strict=False):
    r = lax.broadcasted_iota(jnp.int32, (n, n), 0)
    c = lax.broadcasted_iota(jnp.int32, (n, n), 1)
    return (r > c) if strict else (r >= c)


def _cumsum_rows(x, tri_bf16):
    hi, mid, lo = _split3(x)
    return _dot(tri_bf16, hi) + _dot(tri_bf16, mid) + _dot(tri_bf16, lo)


def _sigmoid(x):
    return 1.0 / (1.0 + jnp.exp(-x))


def _softplus(x):
    return jnp.maximum(x, 0.0) + jnp.log(1.0 + jnp.exp(-jnp.abs(x)))


def _row_tile(m, pref):
    return pref if m % pref == 0 else m


def _act_dtype(tile_rows):
    return BF16 if tile_rows % (2 * SUBLANES) == 0 else F32


def _norm_matmul_kernel(x_ref, g_ref, w_ref, o_ref, h_ref):
    @pl.when(pl.program_id(1) == 0)
    def _():
        x = x_ref[...]
        ms = jnp.mean(x * x, axis=-1, keepdims=True)
        h_ref[...] = (x * lax.rsqrt(ms + NORM_EPS) * g_ref[...]).astype(BF16)

    o_ref[...] = _dot(h_ref[...], w_ref[...]).astype(o_ref.dtype)


def norm_matmul(x, g, w, layer, *, tn=1024, tm_pref=1024):
    m, d = x.shape
    n = w.shape[2]
    tm = _row_tile(m, tm_pref)
    return pl.pallas_call(
        _norm_matmul_kernel,
        out_shape=jax.ShapeDtypeStruct((m, n), F32),
        grid=(m // tm, n // tn),
        in_specs=[
            pl.BlockSpec((tm, d), lambda i, j: (i, 0)),
            pl.BlockSpec((1, d), lambda i, j: (0, 0)),
            pl.BlockSpec((None, d, tn), lambda i, j: (layer, 0, j)),
        ],
        out_specs=pl.BlockSpec((tm, tn), lambda i, j: (i, j)),
        scratch_shapes=[pltpu.VMEM((tm, d), BF16)],
        compiler_params=_cparams("parallel", "arbitrary"),
        name="norm_matmul",
    )(x, g.reshape(1, d), w)


def _rwkv_kernel(p_ref, shift0_ref, s0_ref, mu_ref, w0_ref, wup_ref, a0_ref, aup_ref, gup_ref,
                 kk_ref, ka_ref, rk_ref, lnw_ref, lnb_ref,
                 y_ref, sout_ref, shout_ref,
                 carry_ref, s_ref, r_s, k_s, v_s, kk_s, b_s, lw_s, y_s,
                 *, t_real, tt, chunk, bb):
    t = pl.program_id(1)
    nt = pl.num_programs(1)
    W = RWKV_WIDTH
    N = RWKV_HEAD_DIM
    H = RWKV_HEADS
    C = chunk
    cp = RWKV_CHUNK_PASSES
    seqs = range(bb)

    @pl.when(t == 0)
    def _():
        carry_ref[...] = shift0_ref[...]
        s_ref[...] = s0_ref[...]

    rowid = lax.broadcasted_iota(jnp.int32, (tt, 1), 0)
    last_row = (t_real - 1) % tt
    seg_r = lax.broadcasted_iota(jnp.int32, (W, W), 0) // N
    seg_c = lax.broadcasted_iota(jnp.int32, (W, W), 1) // N
    seg = jnp.where(seg_r == seg_c, 1.0, 0.0).astype(BF16)
    gate = []
    for b in seqs:
        P = p_ref[b]
        prev = jnp.where(rowid == 0, carry_ref[b], pltpu.roll(P, 1, axis=0))
        carry_ref[b] = P[last_row:last_row + 1, :]

        @pl.when(t == nt - 1)
        def _():
            shout_ref[b] = P[last_row:last_row + 1, :]

        Pm = P + (prev - P) * mu_ref[...]
        r = Pm[:, 0:W]
        k = Pm[:, W:2 * W]
        v = Pm[:, 2 * W:3 * W]
        o = 3 * W
        wd = Pm[:, o:o + RWKV_DECAY_RANK]
        ad = Pm[:, o + RWKV_DECAY_RANK:o + RWKV_DECAY_RANK + RWKV_ICLR_RANK]
        gd = Pm[:, o + RWKV_DECAY_RANK + RWKV_ICLR_RANK:]
        w_val = -_softplus(-(w0_ref[...] + _mm(jnp.tanh(wd), wup_ref[...], 'nn', GATE_PASSES))) - 0.5
        lw = -jnp.exp(w_val)
        a = _sigmoid(a0_ref[...] + _mm(ad, aup_ref[...], 'nn', GATE_PASSES))
        gate.append(_mm(_sigmoid(gd), gup_ref[...], 'nn', GATE_PASSES))
        kk = k * kk_ref[...]
        kmod = k * (1.0 + (a - 1.0) * ka_ref[...])
        if t_real % tt != 0:
            valid = (t * tt + rowid) < t_real
            lw = jnp.where(valid, lw, 0.0)
            kk = jnp.where(valid, kk, 0.0)
            kmod = jnp.where(valid, kmod, 0.0)
            v = jnp.where(valid, v, 0.0)
        hi, md, lo = _split3(kk * kk)
        ss = _dot(hi, seg) + _dot(md, seg) + _dot(lo, seg)
        kkn = kk / jnp.maximum(jnp.sqrt(ss), 1e-12)
        r_s[b] = r
        k_s[b] = kmod
        v_s[b] = v
        kk_s[b] = kkn
        b_s[b] = kkn * a
        lw_s[b] = lw

    tri = jnp.where(_tril_ones(C), 1.0, 0.0).astype(BF16)
    low_incl = _tril_ones(C)
    low_strict = _tril_ones(C, strict=True)
    eye = lax.broadcasted_iota(jnp.int32, (N, N), 0) == lax.broadcasted_iota(jnp.int32, (N, N), 1)
    mid = max(C // 2 - 1, 0)
    pairs = [(b, slice(h * N, (h + 1) * N), h) for b in seqs for h in range(H)]
    idx = range(len(pairs))

    def chunk_body(c, carry):
        rows = pl.ds(pl.multiple_of(c * C, C), C)
        kk_hat, r_hat, k_til, b_til, kk_abs, r_abs, k_end, b_end, gam, r_c, k_c, v_c = ([] for _ in range(12))
        for b in seqs:
            lw_c = lw_s[b, rows, :]
            cum = _cumsum_rows(lw_c, tri)
            cume = cum - lw_c
            rho = cum[mid:mid + 1, :]
            cum_last = cum[C - 1:C, :]
            r_c.append(r_s[b, rows, :])
            k_c.append(k_s[b, rows, :])
            v_c.append(v_s[b, rows, :])
            kk_c = kk_s[b, rows, :]
            b_c = b_s[b, rows, :]
            e_out = jnp.exp(rho - cum)
            e_end = jnp.exp(cum_last - cum)
            kk_hat.append(kk_c * jnp.exp(cume - rho))
            r_hat.append(r_c[b] * jnp.exp(cum - rho))
            k_til.append(k_c[b] * e_out)
            b_til.append(b_c * e_out)
            kk_abs.append(kk_c * jnp.exp(cume))
            r_abs.append(r_c[b] * jnp.exp(cum))
            k_end.append(k_c[b] * e_end)
            b_end.append(b_c * e_end)
            gam.append(jnp.exp(cum_last))

        A = [_mm(jnp.concatenate([kk_hat[b][:, s], r_hat[b][:, s]], axis=0),
                 jnp.concatenate([k_til[b][:, s], b_til[b][:, s]], axis=0), 'nt', cp) for b, s, _ in pairs]
        Lk = [jnp.where(low_strict, x[0:C, 0:C], 0.0) for x in A]
        Lb = [jnp.where(low_strict, x[0:C, C:2 * C], 0.0) for x in A]
        Ark = [jnp.where(low_incl, x[C:2 * C, 0:C], 0.0) for x in A]
        Arb = [jnp.where(low_incl, x[C:2 * C, C:2 * C], 0.0) for x in A]
        X = [jnp.concatenate([kk_abs[b][:, s], _mm(Lk[i], v_c[b][:, s], 'nn', cp)], axis=1)
             for i, (b, s, _) in zip(idx, pairs)]
        X = [X[i] - _mm(Lb[i], X[i], 'nn', cp) for i in idx]
        Lp = Lb
        p = 2
        while p < C:
            Lp = [_mm(x, x, 'nn', cp) for x in Lp]
            X = [X[i] + _mm(Lp[i], X[i], 'nn', cp) for i in idx]
            p *= 2
        RY = [jnp.concatenate([r_abs[b][:, s], _mm(Ark[i], v_c[b][:, s], 'nn', cp)], axis=1)
              - _mm(Arb[i], X[i], 'nn', cp) for i, (b, s, _) in zip(idx, pairs)]
        MN = [_mm(X[i], b_end[b][:, s], 'tn', cp) for i, (b, s, _) in zip(idx, pairs)]
        VK = [_mm(v_c[b][:, s], k_end[b][:, s], 'tn', cp) for b, s, _ in pairs]
        for i, (b, s, h) in zip(idx, pairs):
            S = s_ref[b, h]
            Y = _mm(RY[i][:, 0:N], S, 'nt', STATE_PASSES) + RY[i][:, N:2 * N]
            trans = jnp.where(eye, jnp.broadcast_to(gam[b][:, s], (N, N)), 0.0) - MN[i][0:N]
            s_ref[b, h] = _mm(S, trans, 'nn', STATE_PASSES) + (VK[i] - MN[i][N:2 * N])
            mu = jnp.mean(Y, axis=-1, keepdims=True)
            var = jnp.mean(jnp.square(Y - mu), axis=-1, keepdims=True)
            yn = (Y - mu) * lax.rsqrt(var + RWKV_GN_EPS) * lnw_ref[:, s] + lnb_ref[:, s]
            bonus = jnp.sum(r_c[b][:, s] * k_c[b][:, s] * rk_ref[:, s], axis=-1, keepdims=True) * v_c[b][:, s]
            y_s[b, rows, s] = yn + bonus
        return carry

    lax.fori_loop(0, tt // C, chunk_body, 0)
    for b in seqs:
        y_ref[b] = (y_s[b] * gate[b]).astype(y_ref.dtype)

    @pl.when(t == nt - 1)
    def _():
        sout_ref[...] = s_ref[...]


def rwkv_branch(P, shift0, S0, lw, *, batch, t_pad, t_real, tt, chunk, bb=1):
    nt = t_pad // tt
    W = RWKV_WIDTH
    vec = lambda a: a.reshape(1, -1)
    full = lambda shape: pl.BlockSpec(shape, lambda b, t: (0,) * len(shape))
    state = (bb, RWKV_HEADS, RWKV_HEAD_DIM, RWKV_HEAD_DIM)
    kern = functools.partial(_rwkv_kernel, t_real=t_real, tt=tt, chunk=chunk, bb=bb)
    y, s_out, sh_out = pl.pallas_call(
        kern,
        out_shape=(
            jax.ShapeDtypeStruct((batch, t_pad, W), _act_dtype(tt)),
            jax.ShapeDtypeStruct((batch, RWKV_HEADS, RWKV_HEAD_DIM, RWKV_HEAD_DIM), F32),
            jax.ShapeDtypeStruct((batch, 1, RWKV_PROJ), F32),
        ),
        grid=(batch // bb, nt),
        in_specs=[
            pl.BlockSpec((bb, tt, RWKV_PROJ), lambda b, t: (b, t, 0)),
            pl.BlockSpec((bb, 1, RWKV_PROJ), lambda b, t: (b, 0, 0)),
            pl.BlockSpec(state, lambda b, t: (b, 0, 0, 0)),
            full((1, RWKV_PROJ)),
            full((1, W)),
            full((RWKV_DECAY_RANK, W)),
            full((1, W)),
            full((RWKV_ICLR_RANK, W)),
            full((RWKV_GATE_RANK, W)),
            full((1, W)), full((1, W)), full((1, W)), full((1, W)), full((1, W)),
        ],
        out_specs=(
            pl.BlockSpec((bb, tt, W), lambda b, t: (b, t, 0)),
            pl.BlockSpec(state, lambda b, t: (b, 0, 0, 0)),
            pl.BlockSpec((bb, 1, RWKV_PROJ), lambda b, t: (b, 0, 0)),
        ),
        scratch_shapes=[
            pltpu.VMEM((bb, 1, RWKV_PROJ), F32),
            pltpu.VMEM(state, F32),
        ] + [pltpu.VMEM((bb, tt, W), F32) for _ in range(7)],
        compiler_params=_cparams("parallel", "arbitrary"),
        name="rwkv7_branch",
    )(P, shift0.reshape(batch, 1, RWKV_PROJ), S0,
      vec(lw['rwkv_mu']), vec(lw['rwkv_w0']), lw['rwkv_w_up'], vec(lw['rwkv_a0']), lw['rwkv_a_up'],
      lw['rwkv_g_up'], vec(lw['rwkv_k_k']), vec(lw['rwkv_k_a']), vec(lw['rwkv_r_k']),
      vec(lw['rwkv_ln_w']), vec(lw['rwkv_ln_b']))
    return y.reshape(batch * t_pad, W), s_out, sh_out.reshape(batch, RWKV_PROJ)


def _gla_kernel(g_ref, s0_ref, aup_ref, abias_ref, onorm_ref,
                y_ref, sout_ref,
                s_ref, la_s,
                *, t_real, tt, chunk, sub, bb):
    t = pl.program_id(1)
    nt = pl.num_programs(1)
    KW, VW = GLA_K_WIDTH, GLA_V_WIDTH
    dk, dv = GLA_KEY_DIM, GLA_VALUE_DIM
    C = chunk
    cp = GLA_CHUNK_PASSES
    seqs = range(bb)

    @pl.when(t == 0)
    def _():
        s_ref[...] = s0_ref[...]

    padded = t_real % tt != 0
    for b in seqs:
        gl = g_ref[b, :, 2 * KW + 2 * VW:GLA_SECT]
        x = _mm(gl, aup_ref[...], 'nn', GATE_PASSES) + abias_ref[...]
        la = -_softplus(-x) * (1.0 / GLA_TAU)
        if padded:
            rowid = lax.broadcasted_iota(jnp.int32, (tt, 1), 0)
            valid = (t * tt + rowid) < t_real
            la = jnp.where(valid, la, 0.0)
        la_s[b] = la

    tri = jnp.where(_tril_ones(C), 1.0, 0.0).astype(BF16)
    ones_cv = jnp.ones((C, dv), BF16)
    low_incl = _tril_ones(C)
    crow = lax.broadcasted_iota(jnp.int32, (C, 1), 0)
    nsub = C // sub
    pairs = [(b, h) for b in seqs for h in range(GLA_HEADS)]
    idx = range(len(pairs))

    def chunk_body(c, carry):
        rows = pl.ds(pl.multiple_of(c * C, C), C)
        la_c = [la_s[b, rows, :] for b in seqs]
        cum_all = [_cumsum_rows(x, tri) for x in la_c]
        if padded:
            vmask = (t * tt + c * C + crow) < t_real
        q, k, v, cums = [], [], [], []
        for b, h in pairs:
            q.append(g_ref[b, rows, h * dk:(h + 1) * dk] * (dk ** -0.5))
            k_h = g_ref[b, rows, KW + h * dk:KW + (h + 1) * dk]
            v_h = g_ref[b, rows, 2 * KW + h * dv:2 * KW + (h + 1) * dv]
            if padded:
                k_h = jnp.where(vmask, k_h, 0.0)
                v_h = jnp.where(vmask, v_h, 0.0)
            k.append(k_h)
            v.append(v_h)
            cums.append(cum_all[b][:, h * dk:(h + 1) * dk])
        bounds = [[jnp.zeros((1, dk), F32)] + [cm[i * sub - 1:i * sub, :] for i in range(1, nsub)] for cm in cums]
        qt = [q[p] * jnp.exp(cums[p] - jnp.concatenate(
            [jnp.broadcast_to(x, (sub, dk)) for x in bounds[p]], axis=0)) for p in idx]
        att = []
        for p in idx:
            att_rows = []
            for i in range(nsub):
                kt = k[p] * jnp.exp(jnp.where(crow < (i + 1) * sub, bounds[p][i] - cums[p], NEG_INF))
                att_rows.append(_mm(qt[p][i * sub:(i + 1) * sub], kt, 'nt', cp))
            att.append(jnp.where(low_incl, jnp.concatenate(att_rows, axis=0), 0.0))
        intra = [_mm(att[p], v[p], 'nn', cp) for p in idx]
        kv = [_mm(k[p] * jnp.exp(cums[p][C - 1:C, :] - cums[p]), v[p], 'tn', STATE_PASSES) for p in idx]
        tot = []
        for b, h in pairs:
            hi, md, lo = _split3(la_c[b][:, h * dk:(h + 1) * dk])
            tot.append(_dot_tn(hi, ones_cv) + _dot_tn(md, ones_cv) + _dot_tn(lo, ones_cv))
        for p, (b, h) in zip(idx, pairs):
            S = s_ref[b, h]
            o = intra[p] + _mm(q[p] * jnp.exp(cums[p]), S, 'nn', STATE_PASSES)
            s_ref[b, h] = S * jnp.exp(tot[p]) + kv[p]
            r_h = g_ref[b, rows, 2 * KW + VW + h * dv:2 * KW + VW + (h + 1) * dv]
            ms = jnp.mean(o * o, axis=-1, keepdims=True)
            on = o * lax.rsqrt(ms + NORM_EPS) * onorm_ref[...]
            y_ref[b, rows, h * dv:(h + 1) * dv] = (on * (r_h * _sigmoid(r_h))).astype(y_ref.dtype)
        return carry

    lax.fori_loop(0, tt // C, chunk_body, 0)

    @pl.when(t == nt - 1)
    def _():
        sout_ref[...] = s_ref[...]


def gla_branch(G, S0, lw, *, batch, t_pad, t_real, tt, chunk, sub, col_block=0, bb=1):
    nt = t_pad // tt
    full = lambda shape: pl.BlockSpec(shape, lambda b, t: (0,) * len(shape))
    aup = jnp.zeros((GLA_GATE_PAD, GLA_K_WIDTH), F32).at[:GLA_GATE_RANK].set(lw['gla_a_up'])
    state = (bb, GLA_HEADS, GLA_KEY_DIM, GLA_VALUE_DIM)
    kern = functools.partial(_gla_kernel, t_real=t_real, tt=tt, chunk=chunk, sub=sub, bb=bb)
    y, s_out = pl.pallas_call(
        kern,
        out_shape=(
            jax.ShapeDtypeStruct((batch, t_pad, GLA_V_WIDTH), _act_dtype(tt)),
            jax.ShapeDtypeStruct((batch, GLA_HEADS, GLA_KEY_DIM, GLA_VALUE_DIM), F32),
        ),
        grid=(batch // bb, nt),
        in_specs=[
            pl.BlockSpec((bb, tt, PROJ_SLOT), lambda b, t: (b, t, col_block)),
            pl.BlockSpec(state, lambda b, t: (b, 0, 0, 0)),
            full((GLA_GATE_PAD, GLA_K_WIDTH)),
            full((1, GLA_K_WIDTH)),
            full((1, GLA_VALUE_DIM)),
        ],
        out_specs=(
            pl.BlockSpec((bb, tt, GLA_V_WIDTH), lambda b, t: (b, t, 0)),
            pl.BlockSpec(state, lambda b, t: (b, 0, 0, 0)),
        ),
        scratch_shapes=[
            pltpu.VMEM(state, F32),
            pltpu.VMEM((bb, tt, GLA_K_WIDTH), F32),
        ],
        compiler_params=_cparams("parallel", "arbitrary"),
        name="gla_branch",
    )(G, S0, aup, lw['gla_a_bias'].reshape(1, -1), lw['gla_o_norm'].reshape(1, -1))
    return y.reshape(batch * t_pad, GLA_V_WIDTH), s_out


def _norm_rope(x, g, cos, sin):
    ms = jnp.mean(x * x, axis=-1, keepdims=True)
    y = x * lax.rsqrt(ms + NORM_EPS) * g
    return y * cos + pltpu.roll(y, MOBA_HEAD_DIM // 2, axis=1) * sin


def _moba_prep_prompt_kernel(q_ref, k_ref, v_ref, cos_ref, sin_ref, gq_ref, gk_ref, *rest):
    qo_ref, ko_ref, vo_ref, kb_ref, vt_ref, km_ref = rest[-6:]
    Dh = MOBA_HEAD_DIM
    cos = cos_ref[...]
    sin = sin_ref[...]
    rows = q_ref.shape[0]
    for h in range(MOBA_HEADS):
        sl = slice(h * Dh, (h + 1) * Dh)
        qo_ref[:, sl] = _norm_rope(q_ref[:, sl], gq_ref[...], cos, sin)
        kr = _norm_rope(k_ref[:, sl], gk_ref[...], cos, sin)
        ko_ref[:, h, :] = kr
        vo_ref[:, h, :] = v_ref[:, sl]
        kb_ref[:, sl] = kr.astype(BF16)
        km_ref[0, :, sl] = jnp.sum(kr, axis=0, keepdims=True) * (1.0 / rows)
        vt_ref[0, h, 0, 0:Dh, :] = v_ref[:, sl].T.astype(BF16)
        vt_ref[0, h, 0, Dh:VT_ROWS, :] = jnp.ones((VT_ROWS - Dh, rows), BF16)


def moba_prep_prompt(M, cos, sin, gq, gk, kv_all, *, layer, depth, batch, seq, col0=0):
    m = M.shape[0]
    W = MOBA_WIDTH
    blk = MOBA_BLOCK
    Dh = MOBA_HEAD_DIM
    H = MOBA_HEADS
    nq = seq // blk
    nblk = m // blk
    prev = () if kv_all is None else tuple(kv_all)
    slab = pl.BlockSpec((None, blk, H, Dh), lambda i: (layer, i, 0, 0))
    return pl.pallas_call(
        _moba_prep_prompt_kernel,
        out_shape=(
            jax.ShapeDtypeStruct((m, W), F32),
            jax.ShapeDtypeStruct((depth, m, H, Dh), F32),
            jax.ShapeDtypeStruct((depth, m, H, Dh), F32),
            jax.ShapeDtypeStruct((m, W), BF16),
            jax.ShapeDtypeStruct((batch, MOBA_HEADS, nq, VT_ROWS, blk), BF16),
            jax.ShapeDtypeStruct((nblk, 1, W), F32),
        ),
        grid=(nblk,),
        in_specs=[
            pl.BlockSpec((blk, W), lambda i: (i, col0)),
            pl.BlockSpec((blk, W), lambda i: (i, col0 + 1)),
            pl.BlockSpec((blk, W), lambda i: (i, col0 + 2)),
            pl.BlockSpec((blk, Dh), lambda i: (i % nq, 0)),
            pl.BlockSpec((blk, Dh), lambda i: (i % nq, 0)),
            pl.BlockSpec((1, Dh), lambda i: (0, 0)),
            pl.BlockSpec((1, Dh), lambda i: (0, 0)),
        ] + [pl.BlockSpec(memory_space=pl.ANY)] * len(prev),
        out_specs=(
            pl.BlockSpec((blk, W), lambda i: (i, 0)),
            slab,
            slab,
            pl.BlockSpec((blk, W), lambda i: (i, 0)),
            pl.BlockSpec((1, MOBA_HEADS, 1, VT_ROWS, blk), lambda i: (i // nq, 0, i % nq, 0, 0)),
            pl.BlockSpec((1, 1, W), lambda i: (i, 0, 0)),
        ),
        input_output_aliases={7 + j: 1 + j for j in range(len(prev))},
        compiler_params=_cparams("parallel"),
        name="moba_qkv_prep",
    )(M, M, M, cos, sin, gq.reshape(1, -1), gk.reshape(1, -1), *prev)


def _moba_prep_sample_kernel(q_ref, k_ref, cos_ref, sin_ref, gq_ref, gk_ref, qo_ref, ko_ref):
    Dh = MOBA_HEAD_DIM
    cos = cos_ref[...]
    sin = sin_ref[...]
    for h in range(MOBA_HEADS):
        sl = slice(h * Dh, (h + 1) * Dh)
        qo_ref[:, sl] = _norm_rope(q_ref[:, sl], gq_ref[...], cos, sin)
        ko_ref[:, sl] = _norm_rope(k_ref[:, sl], gk_ref[...], cos, sin)


def moba_prep_sample(M, cos, sin, gq, gk, *, col0=0):
    m = M.shape[0]
    W = MOBA_WIDTH
    Dh = MOBA_HEAD_DIM
    return pl.pallas_call(
        _moba_prep_sample_kernel,
        out_shape=(jax.ShapeDtypeStruct((m, W), F32), jax.ShapeDtypeStruct((m, W), F32)),
        grid=(1,),
        in_specs=[
            pl.BlockSpec((m, W), lambda i: (0, col0)),
            pl.BlockSpec((m, W), lambda i: (0, col0 + 1)),
            pl.BlockSpec((m, Dh), lambda i: (0, 0)),
            pl.BlockSpec((m, Dh), lambda i: (0, 0)),
            pl.BlockSpec((1, Dh), lambda i: (0, 0)),
            pl.BlockSpec((1, Dh), lambda i: (0, 0)),
        ],
        out_specs=(pl.BlockSpec((m, W), lambda i: (0, 0)), pl.BlockSpec((m, W), lambda i: (0, 0))),
        compiler_params=_cparams("arbitrary"),
        name="moba_qk_prep_sample",
    )(M, M, cos, sin, gq.reshape(1, -1), gk.reshape(1, -1))


def _block_rank(bs, n_valid, axis):
    nb = bs.shape[axis]
    idx = lax.broadcasted_iota(jnp.int32, bs.shape, axis)
    rank = jnp.zeros(bs.shape, jnp.int32)
    for mm in range(nb):
        one = bs[mm:mm + 1, :] if axis == 0 else bs[:, mm:mm + 1]
        beats = (one > bs) | ((one == bs) & (idx > mm))
        if n_valid is not None:
            beats = beats & (mm < n_valid)
        rank = rank + jnp.where(beats, 1, 0)
    return rank


def _moba_attn_kernel(q_ref, kb_ref, vt_ref, km_ref, o_ref, sel_ref, s_ref):
    i = pl.program_id(2)
    blk = MOBA_BLOCK
    Dh = MOBA_HEAD_DIM
    heads = range(MOBA_HEADS_PER_STEP)
    hs = [slice(h * Dh, (h + 1) * Dh) for h in heads]
    nb = km_ref.shape[1]
    grp = next(g for g in (4, 2, 1) if nb % g == 0)
    c2 = (Dh ** -0.5) * LOG2_E
    blk_id = lax.broadcasted_iota(jnp.int32, (nb, blk), 0)
    causal = (lax.broadcasted_iota(jnp.int32, (blk, blk), 0)
              <= lax.broadcasted_iota(jnp.int32, (blk, blk), 1))
    n_groups = (i + grp - 1) // grp

    qT = [q_ref[:, s].T for s in hs]
    bsT = [_dot(km_ref[0, :, s], qT[h], HIGHEST) for h, s in zip(heads, hs)]
    for h in heads:
        rank = _block_rank(bsT[h], i, 0)
        sel_ref[h] = jnp.where((blk_id < i) & (rank < MOBA_TOPK), 1.0, 0.0)
    qb = [x.astype(BF16) for x in qT]

    own = pl.ds(pl.multiple_of(i * blk, blk), blk)
    s_own = [jnp.where(causal, _dot(kb_ref[0, own, s], qb[h]) * c2, NEG_INF) for h, s in zip(heads, hs)]
    for h in heads:
        s_ref[h, nb * blk:(nb + 1) * blk, :] = s_own[h]

    def score_group(gi, m):
        m = list(m)
        for u in range(grp):
            j = gi * grp + u
            rows = pl.ds(pl.multiple_of(j * blk, blk), blk)
            raw = [_dot(kb_ref[0, rows, s], qb[h]) for h, s in zip(heads, hs)]
            for h in heads:
                picked = sel_ref[h, pl.ds(j, 1), :] > 0.0
                sc = jnp.where(picked, raw[h] * c2, NEG_INF)
                s_ref[h, rows, :] = sc
                m[h] = jnp.maximum(m[h], jnp.max(sc, axis=0, keepdims=True))
        return tuple(m)

    m = lax.fori_loop(0, n_groups, score_group, tuple(jnp.max(x, axis=0, keepdims=True) for x in s_own))

    def weighted_values(j, rows):
        p = [jnp.exp2(s_ref[h, rows, :] - m[h]).astype(BF16) for h in heads]
        return [_dot(vt_ref[0, h, j], p[h]) for h in heads]

    def value_group(gi, acc):
        acc = list(acc)
        for u in range(grp):
            j = gi * grp + u
            new = weighted_values(j, pl.ds(pl.multiple_of(j * blk, blk), blk))
            acc = [a + n for a, n in zip(acc, new)]
        return tuple(acc)

    acc = lax.fori_loop(0, n_groups, value_group, tuple(weighted_values(i, pl.ds(nb * blk, blk))))
    for h, s in zip(heads, hs):
        o_ref[:, s] = (acc[h][0:Dh] / acc[h][Dh:Dh + 1]).T.astype(o_ref.dtype)


def moba_attention_prompt(q_rot, kb, vt, kmean, *, batch, seq):
    W = MOBA_WIDTH
    blk = MOBA_BLOCK
    hp = MOBA_HEADS_PER_STEP
    wide = hp * MOBA_HEAD_DIM
    nq = seq // blk
    nb = kmean.shape[1]
    return pl.pallas_call(
        _moba_attn_kernel,
        out_shape=jax.ShapeDtypeStruct((batch * seq, W), BF16),
        grid=(batch, MOBA_HEADS // hp, nq),
        in_specs=[
            pl.BlockSpec((blk, wide), lambda b, h, i: (b * nq + i, h)),
            pl.BlockSpec((1, seq, wide), lambda b, h, i: (b, 0, h)),
            pl.BlockSpec((1, hp, nb, VT_ROWS, blk), lambda b, h, i: (b, h, 0, 0, 0)),
            pl.BlockSpec((1, nb, wide), lambda b, h, i: (b, 0, h)),
        ],
        out_specs=pl.BlockSpec((blk, wide), lambda b, h, i: (b * nq + i, h)),
        scratch_shapes=[pltpu.VMEM((hp, nb, blk), F32), pltpu.VMEM((hp, (nb + 1) * blk, blk), F32)],
        compiler_params=_cparams("parallel", "parallel", "arbitrary"),
        name="moba_attention",
    )(q_rot, kb.reshape(batch, seq, W), vt, kmean)


PAGES_PER_STEP = 16
PAGES_PER_BLOCK = MOBA_BLOCK // PAGE_SIZE


def _moba_select_kernel(q_ref, km_ref, o_ref):
    Dh = MOBA_HEAD_DIM
    rows = []
    for h in range(MOBA_HEADS):
        rows.append(_dot_nt(q_ref[0, :, h * Dh:(h + 1) * Dh], km_ref[0, h], HIGHEST))
    bs = jnp.concatenate(rows, axis=0)
    nb = bs.shape[1]
    rank = _block_rank(bs, None, 1)
    lane = lax.broadcasted_iota(jnp.int32, bs.shape, 1)
    olane = lax.broadcasted_iota(jnp.int32, (MOBA_HEADS, 128), 1)
    out = jnp.zeros((MOBA_HEADS, 128), jnp.int32)
    for s in range(min(MOBA_TOPK, nb)):
        idx = jnp.sum(jnp.where(rank == s, lane, 0), axis=-1, keepdims=True)
        out = jnp.where(olane == s, idx, out)
    o_ref[0] = out


def moba_select_sample(q3, kmean_past):
    bs, _, W = q3.shape
    nb = kmean_past.shape[2]
    return pl.pallas_call(
        _moba_select_kernel,
        out_shape=jax.ShapeDtypeStruct((bs, MOBA_HEADS, 128), jnp.int32),
        grid=(bs,),
        in_specs=[
            pl.BlockSpec((1, 1, W), lambda b: (b, 0, 0)),
            pl.BlockSpec((1, MOBA_HEADS, nb, MOBA_HEAD_DIM), lambda b: (b, 0, 0, 0)),
        ],
        out_specs=pl.BlockSpec((1, MOBA_HEADS, 128), lambda b: (b, 0, 0)),
        compiler_params=_cparams("parallel"),
        name="moba_select_sample",
    )(q3, kmean_past)


def _moba_decode_kernel(sel_ref, pt_ref, q_ref, kn_ref, vn_ref, ck_ref, cv_ref, o_ref, kbuf, vbuf, sem,
                        *, layer, n_pg, n_sel, n_pages):
    step = pl.program_id(0)
    scale = MOBA_HEAD_DIM ** -0.5

    def page_copies(st, slot):
        b = st // MOBA_HEADS
        h = st % MOBA_HEADS
        copies = []
        for e in range(n_pg):
            blk = sel_ref[st * n_sel + e // PAGES_PER_BLOCK]
            page = pt_ref[b * n_pages + blk * PAGES_PER_BLOCK + e % PAGES_PER_BLOCK]
            copies.append(pltpu.make_async_copy(ck_ref.at[layer, page, :, h, :], kbuf.at[slot, e], sem.at[slot, 0, e]))
            copies.append(pltpu.make_async_copy(cv_ref.at[layer, page, :, h, :], vbuf.at[slot, e], sem.at[slot, 1, e]))
        return copies

    @pl.when(step == 0)
    def _():
        for c in page_copies(0, 0):
            c.start()

    @pl.when(step + 1 < pl.num_programs(0))
    def _():
        for c in page_copies(step + 1, (step + 1) % 2):
            c.start()

    slot = step % 2
    for c in page_copies(step, slot):
        c.wait()

    rows = n_pg * PAGE_SIZE
    q = q_ref[0]
    s_own = jnp.sum(q * kn_ref[0], axis=-1, keepdims=True) * scale
    kp = kbuf[slot].reshape(rows, MOBA_HEAD_DIM).astype(BF16)
    vp = vbuf[slot].reshape(rows, MOBA_HEAD_DIM).astype(BF16)
    sc = _dot_nt(q.astype(BF16), kp) * scale
    m = jnp.maximum(s_own, jnp.max(sc, axis=-1, keepdims=True))
    p_own = jnp.exp(s_own - m)
    p = jnp.exp(sc - m)
    l = p_own + jnp.sum(p, axis=-1, keepdims=True)
    acc = p_own * vn_ref[0] + _dot(p.astype(BF16), vp)
    o_ref[0] = (acc / l).astype(o_ref.dtype)


def moba_decode(sel_flat, pt_flat, q3, k3, M3, cache_k, cache_v, *, layer, n_pages, n_sel, v_col):
    bs, _, W = q3.shape
    Dh = MOBA_HEAD_DIM
    H = MOBA_HEADS
    n_pg = n_sel * PAGES_PER_BLOCK
    vec = lambda col: pl.BlockSpec((1, 1, Dh), lambda st, sel, pt: (st // H, 0, col(st % H)))
    hbm = pl.BlockSpec(memory_space=pl.ANY)
    return pl.pallas_call(
        functools.partial(_moba_decode_kernel, layer=layer, n_pg=n_pg, n_sel=n_sel, n_pages=n_pages),
        out_shape=jax.ShapeDtypeStruct((bs, 1, W), F32),
        grid_spec=pltpu.PrefetchScalarGridSpec(
            num_scalar_prefetch=2,
            grid=(bs * H,),
            in_specs=[vec(lambda h: h), vec(lambda h: h), vec(lambda h: v_col // Dh + h), hbm, hbm],
            out_specs=vec(lambda h: h),
            scratch_shapes=[
                pltpu.VMEM((2, n_pg, PAGE_SIZE, Dh), F32),
                pltpu.VMEM((2, n_pg, PAGE_SIZE, Dh), F32),
                pltpu.SemaphoreType.DMA((2, 2, n_pg)),
            ],
        ),
        compiler_params=_cparams("arbitrary"),
        name="moba_decode",
    )(sel_flat, pt_flat, q3, k3, M3, cache_k, cache_v)


def _merge_kernel(ya_ref, yb_ref, yc_ref, wa_ref, wb_ref, wc_ref, z0_ref, z1_ref, z2_ref, o_ref):
    acc = _sigmoid(z0_ref[...]) * _dot(ya_ref[...].astype(BF16), wa_ref[...])
    acc = acc + _sigmoid(z1_ref[...]) * _dot(yb_ref[...].astype(BF16), wb_ref[...])
    acc = acc + _sigmoid(z2_ref[...]) * _dot(yc_ref[...].astype(BF16), wc_ref[...])
    o_ref[...] = acc.astype(o_ref.dtype)


def gated_merge(ya, yb, yc, wa, wb, wc, proj, layer, *, z_col, tn=PROJ_TILE, tm_pref=512):
    m = ya.shape[0]
    d = wa.shape[2]
    tm = _row_tile(m, tm_pref)
    nj = d // tn
    z0 = z_col // tn
    wspec = lambda w: pl.BlockSpec((None, w.shape[1], tn), lambda i, j: (layer, 0, j))
    return pl.pallas_call(
        _merge_kernel,
        out_shape=jax.ShapeDtypeStruct((m, d), BF16),
        grid=(m // tm, nj),
        in_specs=[
            pl.BlockSpec((tm, ya.shape[1]), lambda i, j: (i, 0)),
            pl.BlockSpec((tm, yb.shape[1]), lambda i, j: (i, 0)),
            pl.BlockSpec((tm, yc.shape[1]), lambda i, j: (i, 0)),
            wspec(wa), wspec(wb), wspec(wc),
            pl.BlockSpec((tm, tn), lambda i, j: (i, z0 + j)),
            pl.BlockSpec((tm, tn), lambda i, j: (i, z0 + nj + j)),
            pl.BlockSpec((tm, tn), lambda i, j: (i, z0 + 2 * nj + j)),
        ],
        out_specs=pl.BlockSpec((tm, tn), lambda i, j: (i, j)),
        compiler_params=_cparams("parallel", "arbitrary"),
        name="gated_merge",
    )(ya, yb, yc, wa, wb, wc, proj, proj, proj)


def _matmul_residual_kernel(a_ref, w_ref, x_ref, o_ref):
    o_ref[...] = x_ref[...] + _dot(a_ref[...], w_ref[...])


def matmul_residual(a, w, x, layer, *, tn, tm_pref=512):
    m, k = a.shape
    n = w.shape[2]
    tm = _row_tile(m, tm_pref)
    return pl.pallas_call(
        _matmul_residual_kernel,
        out_shape=jax.ShapeDtypeStruct((m, n), F32),
        grid=(m // tm, n // tn),
        in_specs=[
            pl.BlockSpec((tm, k), lambda i, j: (i, 0)),
            pl.BlockSpec((None, k, tn), lambda i, j: (layer, 0, j)),
            pl.BlockSpec((tm, tn), lambda i, j: (i, j)),
        ],
        out_specs=pl.BlockSpec((tm, tn), lambda i, j: (i, j)),
        compiler_params=_cparams("parallel", "arbitrary"),
        name="matmul_residual",
    )(a, w, x)


def _ffn_in_kernel(x_ref, g_ref, wg_ref, wv_ref, o_ref, h_ref):
    @pl.when(pl.program_id(1) == 0)
    def _():
        x = x_ref[...]
        ms = jnp.mean(x * x, axis=-1, keepdims=True)
        h_ref[...] = (x * lax.rsqrt(ms + NORM_EPS) * g_ref[...]).astype(BF16)

    h = h_ref[...]
    gate = _dot(h, wg_ref[...])
    val = _dot(h, wv_ref[...])
    o_ref[...] = (gate * _sigmoid(gate) * val).astype(o_ref.dtype)


def _ffn_in_key_means_kernel(pt_ref, x_ref, g_ref, wg_ref, wv_ref, *refs, n_pg, page_steps):
    del pt_ref
    page_refs, (o_ref, km_ref, h_ref) = refs[:n_pg], refs[n_pg:]
    _ffn_in_kernel(x_ref, g_ref, wg_ref, wv_ref, o_ref, h_ref)

    @pl.when(pl.program_id(0) * pl.num_programs(1) + pl.program_id(1) < page_steps)
    def _():
        for n in range(n_pg // PAGES_PER_BLOCK):
            tot = page_refs[PAGES_PER_BLOCK * n][0, 0].sum(axis=0)
            for e in range(1, PAGES_PER_BLOCK):
                tot = tot + page_refs[PAGES_PER_BLOCK * n + e][0, 0].sum(axis=0)
            tot = tot * (1.0 / MOBA_BLOCK)
            for hd in range(MOBA_HEADS):
                km_ref[0, hd, pl.ds(n, 1), :] = tot[hd:hd + 1, :]


def ffn_in(x, g, w, layer, *, tn=512, tm_pref=1024):
    m, d = x.shape
    hidden = w.shape[2] // 2
    tm = _row_tile(m, tm_pref)
    nj = hidden // tn
    return pl.pallas_call(
        _ffn_in_kernel,
        out_shape=jax.ShapeDtypeStruct((m, hidden), BF16),
        grid=(m // tm, nj),
        in_specs=[
            pl.BlockSpec((tm, d), lambda i, j: (i, 0)),
            pl.BlockSpec((1, d), lambda i, j: (0, 0)),
            pl.BlockSpec((None, d, tn), lambda i, j: (layer, 0, j)),
            pl.BlockSpec((None, d, tn), lambda i, j: (layer, 0, nj + j)),
        ],
        out_specs=pl.BlockSpec((tm, tn), lambda i, j: (i, j)),
        scratch_shapes=[pltpu.VMEM((tm, d), BF16)],
        compiler_params=_cparams("parallel", "arbitrary"),
        name="ffn_in_swiglu",
    )(x, g.reshape(1, d), w, w)


def ffn_in_key_means(x, g, w, layer, page_table, cache_k, *, tn=512, tm_pref=1024):
    m, d = x.shape
    hidden = w.shape[2] // 2
    tm = _row_tile(m, tm_pref)
    nj = hidden // tn
    bs, n_pages = page_table.shape
    pps = min(PAGES_PER_STEP, n_pages)
    per_seq = n_pages // pps
    page_steps = bs * per_seq
    assert n_pages % pps == 0 and page_steps <= (m // tm) * nj

    def page_pos(i, j):
        t = jnp.minimum(i * nj + j, page_steps - 1)
        return t // per_seq, t % per_seq

    def page_spec(e):
        def page_map(i, j, pt):
            b, s = page_pos(i, j)
            return (layer, pt[b * n_pages + s * pps + e], 0, 0, 0)
        return pl.BlockSpec((1, 1, PAGE_SIZE, MOBA_HEADS, MOBA_HEAD_DIM), page_map)

    def km_map(i, j, pt):
        b, s = page_pos(i, j)
        return (b, 0, s, 0)

    return pl.pallas_call(
        functools.partial(_ffn_in_key_means_kernel, n_pg=pps, page_steps=page_steps),
        out_shape=(
            jax.ShapeDtypeStruct((m, hidden), BF16),
            jax.ShapeDtypeStruct((bs, MOBA_HEADS, n_pages // PAGES_PER_BLOCK, MOBA_HEAD_DIM), F32),
        ),
        grid_spec=pltpu.PrefetchScalarGridSpec(
            num_scalar_prefetch=1,
            grid=(m // tm, nj),
            in_specs=[
                pl.BlockSpec((tm, d), lambda i, j, pt: (i, 0)),
                pl.BlockSpec((1, d), lambda i, j, pt: (0, 0)),
                pl.BlockSpec((None, d, tn), lambda i, j, pt: (layer, 0, j)),
                pl.BlockSpec((None, d, tn), lambda i, j, pt: (layer, 0, nj + j)),
            ] + [page_spec(e) for e in range(pps)],
            out_specs=(
                pl.BlockSpec((tm, tn), lambda i, j, pt: (i, j)),
                pl.BlockSpec((1, MOBA_HEADS, pps // PAGES_PER_BLOCK, MOBA_HEAD_DIM), km_map),
            ),
            scratch_shapes=[pltpu.VMEM((tm, d), BF16)],
        ),
        compiler_params=_cparams("arbitrary", "arbitrary"),
        name="ffn_in_swiglu_key_means",
    )(page_table.reshape(-1), x, g.reshape(1, d), w, w, *([cache_k] * pps))


def _rope_tables(pos):
    half = MOBA_HEAD_DIM // 2
    inv = ROPE_THETA ** (-jnp.arange(half, dtype=F32) / half)
    ang = pos.astype(F32)[:, None] * inv[None, :]
    cos, sin = jnp.cos(ang), jnp.sin(ang)
    return jnp.concatenate([cos, cos], axis=-1), jnp.concatenate([-sin, sin], axis=-1)


def _bf16_weights(w):
    o1 = RWKV_PROJ
    o2 = o1 + 3 * MOBA_WIDTH
    o3 = o2 + 2 * GLA_K_WIDTH + GLA_V_WIDTH
    o4 = o3 + GLA_GATE_RANK
    o5 = o4 + GLA_V_WIDTH
    w_in = w['w_in'].astype(BF16)
    total = COL_GATES + w_in.shape[-1] - o5

    def place(lo, hi, at):
        return jnp.pad(w_in[..., lo:hi], ((0, 0), (0, 0), (at, total - at - (hi - lo))))

    packed = (place(0, o1, COL_RWKV)
              + place(o2, o3, COL_GLA) + place(o4, o5, COL_GLA + o3 - o2)
              + place(o3, o4, COL_GLA + o3 - o2 + o5 - o4)
              + place(o1, o2, COL_MOBA) + place(o5, w_in.shape[-1], COL_GATES))
    out = {k: w[k].astype(BF16) for k in ('w_up_rwkv', 'w_up_moba', 'w_up_gla', 'w_out', 'w_ffn_in', 'w_ffn_out')}
    out['w_in'] = packed
    return out


def _mix_and_ffn(x, ya, yb, yc, proj, lw, wb, layer, paged_keys=None):
    merged = gated_merge(ya, yb, yc, wb['w_up_rwkv'], wb['w_up_moba'], wb['w_up_gla'], proj, layer,
                         z_col=COL_GATES, tn=512, tm_pref=1024)
    x = matmul_residual(merged, wb['w_out'], x, layer, tn=1024, tm_pref=1024)
    if paged_keys is None:
        act, kmean = ffn_in(x, lw['norm_ffn'], wb['w_ffn_in'], layer), None
    else:
        act, kmean = ffn_in_key_means(x, lw['norm_ffn'], wb['w_ffn_in'], layer, *paged_keys)
    return matmul_residual(act, wb['w_ffn_out'], x, layer, tn=512, tm_pref=1024), kmean


def _prompt_layer(x, lw, wb, rope, layer, depth, kv_all, paged_keys, *, batch, seq):
    proj = norm_matmul(x, lw['norm_mix'], wb['w_in'], layer)

    tt = min(seq, 256)
    ya, r_S, shift = rwkv_branch(
        proj.reshape(batch, seq, -1), jnp.zeros((batch, RWKV_PROJ), F32),
        jnp.zeros((batch, RWKV_HEADS, RWKV_HEAD_DIM, RWKV_HEAD_DIM), F32), lw,
        batch=batch, t_pad=seq, t_real=seq, tt=tt, chunk=min(tt, RWKV_CHUNK), bb=2 if batch % 2 == 0 else 1)
    pair = 2 if batch % 2 == 0 else 1
    yc, g_S = gla_branch(
        proj.reshape(batch, seq, -1), jnp.zeros((batch, GLA_HEADS, GLA_KEY_DIM, GLA_VALUE_DIM), F32), lw,
        batch=batch, t_pad=seq, t_real=seq, tt=tt, chunk=min(tt, 128), sub=16, col_block=COL_GLA // PROJ_SLOT,
        bb=pair)

    nq = seq // MOBA_BLOCK
    q_rot, k_all, v_all, kb, vt, kmean = moba_prep_prompt(
        proj, rope[0], rope[1], lw['moba_q_norm'], lw['moba_k_norm'], kv_all,
        layer=layer, depth=depth, batch=batch, seq=seq, col0=COL_MOBA // MOBA_WIDTH)
    yb = moba_attention_prompt(q_rot, kb, vt, kmean.reshape(batch, nq, MOBA_WIDTH), batch=batch, seq=seq)

    x, kmean_past = _mix_and_ffn(x, ya, yb, yc, proj, lw, wb, layer, paged_keys)
    return x, (r_S, shift, g_S), (k_all, v_all), kmean_past


def _sample_layer(x, lw, wb, rope, shift0, rwkv_S0, gla_S0, kmean_past, pt_flat, cache_k, cache_v,
                  *, layer, n_pages):
    bs = x.shape[0]
    proj = norm_matmul(x, lw['norm_mix'], wb['w_in'], layer)

    pad_rows = lambda a: jnp.pad(a[:, None, :], ((0, 0), (0, SUBLANES - 1), (0, 0))).reshape(bs * SUBLANES, -1)
    ya, r_S, shift = rwkv_branch(pad_rows(proj[:, COL_RWKV:COL_RWKV + RWKV_PROJ]).reshape(bs, SUBLANES, -1),
                                 shift0, rwkv_S0, lw,
                                 batch=bs, t_pad=SUBLANES, t_real=1, tt=SUBLANES, chunk=SUBLANES)
    yc, g_S = gla_branch(pad_rows(proj[:, COL_GLA:COL_GLA + PROJ_SLOT]).reshape(bs, SUBLANES, -1), gla_S0, lw,
                         batch=bs, t_pad=SUBLANES, t_real=1, tt=SUBLANES, chunk=SUBLANES, sub=SUBLANES)
    ya = ya[::SUBLANES]
    yc = yc[::SUBLANES]

    q_rot, k_rot = moba_prep_sample(proj, rope[0], rope[1], lw['moba_q_norm'], lw['moba_k_norm'],
                                    col0=COL_MOBA // MOBA_WIDTH)
    q3 = q_rot.reshape(bs, 1, MOBA_WIDTH)
    n_sel = min(MOBA_TOPK, kmean_past.shape[2])
    sel = moba_select_sample(q3, kmean_past)[:, :, :n_sel].reshape(-1)
    v_col = COL_MOBA + 2 * MOBA_WIDTH
    yb = moba_decode(sel, pt_flat, q3, k_rot.reshape(bs, 1, MOBA_WIDTH), proj.reshape(bs, 1, -1),
                     cache_k, cache_v, layer=layer, n_pages=n_pages, n_sel=n_sel, v_col=v_col).reshape(bs, MOBA_WIDTH)

    x, _ = _mix_and_ffn(x, ya, yb, yc, proj, lw, wb, layer)
    k_new = k_rot.reshape(bs, 1, MOBA_HEADS, MOBA_HEAD_DIM)
    v_new = proj[:, v_col:COL_GATES].reshape(bs, 1, MOBA_HEADS, MOBA_HEAD_DIM)
    return x, k_new, v_new, r_S, shift, g_S


_LAYER_KEYS = ('norm_mix', 'w_in', 'rwkv_mu', 'rwkv_w0', 'rwkv_w_up', 'rwkv_a0', 'rwkv_a_up', 'rwkv_g_up',
               'rwkv_k_k', 'rwkv_k_a', 'rwkv_r_k', 'rwkv_ln_w', 'rwkv_ln_b', 'moba_q_norm', 'moba_k_norm',
               'gla_a_up', 'gla_a_bias', 'gla_o_norm', 'w_up_rwkv', 'w_up_moba', 'w_up_gla', 'w_out',
               'norm_ffn', 'w_ffn_in', 'w_ffn_out')


def kernel(x_prompt, x_sample, cache_k, cache_v, page_table, state_rwkv, state_rwkv_shift, state_gla, norm_mix, w_in, rwkv_mu, rwkv_w0, rwkv_w_up, rwkv_a0, rwkv_a_up, rwkv_g_up, rwkv_k_k, rwkv_k_a, rwkv_r_k, rwkv_ln_w, rwkv_ln_b, moba_q_norm, moba_k_norm, gla_a_up, gla_a_bias, gla_o_norm, w_up_rwkv, w_up_moba, w_up_gla, w_out, norm_ffn, w_ffn_in, w_ffn_out):
    stacked = dict(zip(_LAYER_KEYS, (
        norm_mix, w_in, rwkv_mu, rwkv_w0, rwkv_w_up, rwkv_a0, rwkv_a_up, rwkv_g_up, rwkv_k_k, rwkv_k_a,
        rwkv_r_k, rwkv_ln_w, rwkv_ln_b, moba_q_norm, moba_k_norm, gla_a_up, gla_a_bias, gla_o_norm,
        w_up_rwkv, w_up_moba, w_up_gla, w_out, norm_ffn, w_ffn_in, w_ffn_out)))
    depth = w_in.shape[0]
    bp, seq, d = x_prompt.shape
    bs, dec_seq, _ = x_sample.shape
    n_pages = page_table.shape[1]
    past_len = n_pages * cache_k.shape[2]
    assert dec_seq == 1 and cache_k.shape[2:] == (PAGE_SIZE, MOBA_HEADS, MOBA_HEAD_DIM)
    assert seq % MOBA_BLOCK == 0 and past_len % MOBA_BLOCK == 0 and past_len // MOBA_BLOCK >= MOBA_TOPK

    pt_flat = page_table.reshape(-1)

    rope_p = _rope_tables(jnp.arange(seq))
    rope_s = _rope_tables(jnp.full((bs,), past_len))

    yp = x_prompt.reshape(bp * seq, d)
    ys = x_sample.reshape(bs, d)
    wb = _bf16_weights(stacked)
    small = [k for k in _LAYER_KEYS if k not in wb]
    outs_p, outs_s = [], []
    kv_all = None
    for l in range(depth):
        lw = {k: stacked[k][l] for k in small}
        yp, states_p, kv_all, kmean_past = _prompt_layer(yp, lw, wb, rope_p, l, depth, kv_all,
                                                         (page_table, cache_k), batch=bp, seq=seq)
        outs_p.append(states_p)
        ys, *rest_s = _sample_layer(ys, lw, wb, rope_s, state_rwkv_shift[l], state_rwkv[l], state_gla[l],
                                    kmean_past, pt_flat, cache_k, cache_v, layer=l, n_pages=n_pages)
        outs_s.append(rest_s)
    stack = lambda outs, i: jnp.stack([o[i] for o in outs])
    kv_shape = (depth, bp, seq, MOBA_HEADS, MOBA_HEAD_DIM)
    return (yp.reshape(bp, seq, d), ys.reshape(bs, 1, d),
            kv_all[0].reshape(kv_shape), kv_all[1].reshape(kv_shape),
            stack(outs_p, 0), stack(outs_p, 1), stack(outs_p, 2),
            stack(outs_s, 0), stack(outs_s, 1), stack(outs_s, 2), stack(outs_s, 3), stack(outs_s, 4))
```

```python
import functools

import jax
import jax.numpy as jnp
from jax import lax
from jax.experimental import pallas as pl
from jax.experimental.pallas import tpu as pltpu

F32 = jnp.float32
BF16 = jnp.bfloat16
HIGHEST = lax.Precision.HIGHEST

PAGE_SIZE = 128
RWKV_HEADS = 8
RWKV_HEAD_DIM = 64
RWKV_WIDTH = RWKV_HEADS * RWKV_HEAD_DIM
RWKV_DECAY_RANK = 64
RWKV_ICLR_RANK = 64
RWKV_GATE_RANK = 128
RWKV_PROJ = 3 * RWKV_WIDTH + RWKV_DECAY_RANK + RWKV_ICLR_RANK + RWKV_GATE_RANK
RWKV_GN_EPS = 64e-5
MOBA_HEADS = 8
MOBA_HEAD_DIM = 128
MOBA_WIDTH = MOBA_HEADS * MOBA_HEAD_DIM
MOBA_BLOCK = 256
MOBA_TOPK = 3
ROPE_THETA = 10000.0
GLA_HEADS = 4
GLA_KEY_DIM = 64
GLA_VALUE_DIM = 128
GLA_K_WIDTH = GLA_HEADS * GLA_KEY_DIM
GLA_V_WIDTH = GLA_HEADS * GLA_VALUE_DIM
GLA_GATE_RANK = 16
GLA_GATE_PAD = 256
GLA_SECT = 2 * GLA_K_WIDTH + 2 * GLA_V_WIDTH + GLA_GATE_PAD
GLA_TAU = 16.0
PROJ_SLOT = 2048
PROJ_TILE = 1024
COL_RWKV = 0
COL_GLA = PROJ_SLOT
COL_MOBA = 2 * PROJ_SLOT
COL_GATES = COL_MOBA + 3 * MOBA_WIDTH
N_BRANCHES = 3
NORM_EPS = 1e-6
NEG_INF = -1e30
LOG2_E = 1.4426950408889634
VT_ROWS = MOBA_HEAD_DIM + 16
MOBA_HEADS_PER_STEP = 4

VMEM_LIMIT_BYTES = 56 * 1024 * 1024
SUBLANES = 8

GATE_PASSES = 3
STATE_PASSES = 3
RWKV_CHUNK = 128
RWKV_CHUNK_PASSES = 1
GLA_CHUNK_PASSES = 1


def _cparams(*sem):
    return pltpu.CompilerParams(dimension_semantics=sem, vmem_limit_bytes=VMEM_LIMIT_BYTES)


def _dot(a, b, precision=None):
    return jnp.dot(a, b, preferred_element_type=F32, precision=precision)


def _dot_nt(a, b, precision=None):
    return lax.dot_general(a, b, (((1,), (1,)), ((), ())), preferred_element_type=F32, precision=precision)


def _dot_tn(a, b, precision=None):
    return lax.dot_general(a, b, (((0,), (0,)), ((), ())), preferred_element_type=F32, precision=precision)


def _mm(a, b, kind, passes):
    f = {'nn': _dot, 'nt': _dot_nt, 'tn': _dot_tn}[kind]
    if passes == 6:
        return f(a, b, HIGHEST)
    ah = a.astype(BF16)
    bh = b.astype(BF16)
    if passes == 1:
        return f(ah, bh)
    al = (a - ah.astype(F32)).astype(BF16)
    bl = (b - bh.astype(F32)).astype(BF16)
    return f(ah, bh) + f(ah, bl) + f(al, bh)


def _split3(x):
    hi = x.astype(BF16)
    r1 = x - hi.astype(F32)
    mid = r1.astype(BF16)
    lo = (r1 - mid.astype(F32)).astype(BF16)
    return hi, mid, lo


def _tril_ones(n, strict=False):
    r = lax.broadcasted_iota(jnp.int32, (n, n), 0)
    c = lax.broadcasted_iota(jnp.int32, (n, n), 1)
    return (r > c) if strict else (r >= c)


def _cumsum_rows(x, tri_bf16):
    hi, mid, lo = _split3(x)
    return _dot(tri_bf16, hi) + _dot(tri_bf16, mid) + _dot(tri_bf16, lo)


def _sigmoid(x):
    return 1.0 / (1.0 + jnp.exp(-x))


def _softplus(x):
    return jnp.maximum(x, 0.0) + jnp.log(1.0 + jnp.exp(-jnp.abs(x)))


def _row_tile(m, pref):
    return pref if m % pref == 0 else m


def _act_dtype(tile_rows):
    return BF16 if tile_rows % (2 * SUBLANES) == 0 else F32


def _norm_matmul_kernel(x_ref, g_ref, w_ref, o_ref, h_ref):
    @pl.when(pl.program_id(1) == 0)
    def _():
        x = x_ref[...]
        ms = jnp.mean(x * x, axis=-1, keepdims=True)
        h_ref[...] = (x * lax.rsqrt(ms + NORM_EPS) * g_ref[...]).astype(BF16)

    o_ref[...] = _dot(h_ref[...], w_ref[...]).astype(o_ref.dtype)


def norm_matmul(x, g, w, layer, *, tn=1024, tm_pref=1024):
    m, d = x.shape
    n = w.shape[2]
    tm = _row_tile(m, tm_pref)
    return pl.pallas_call(
        _norm_matmul_kernel,
        out_shape=jax.ShapeDtypeStruct((m, n), F32),
        grid=(m // tm, n // tn),
        in_specs=[
            pl.BlockSpec((tm, d), lambda i, j: (i, 0)),
            pl.BlockSpec((1, d), lambda i, j: (0, 0)),
            pl.BlockSpec((None, d, tn), lambda i, j: (layer, 0, j)),
        ],
        out_specs=pl.BlockSpec((tm, tn), lambda i, j: (i, j)),
        scratch_shapes=[pltpu.VMEM((tm, d), BF16)],
        compiler_params=_cparams("parallel", "arbitrary"),
        name="norm_matmul",
    )(x, g.reshape(1, d), w)


def _rwkv_kernel(p_ref, shift0_ref, s0_ref, mu_ref, w0_ref, wup_ref, a0_ref, aup_ref, gup_ref,
                 kk_ref, ka_ref, rk_ref, lnw_ref, lnb_ref,
                 y_ref, sout_ref, shout_ref,
                 carry_ref, s_ref, r_s, k_s, v_s, kk_s, b_s, lw_s, y_s,
                 *, t_real, tt, chunk, bb):
    t = pl.program_id(1)
    nt = pl.num_programs(1)
    W = RWKV_WIDTH
    N = RWKV_HEAD_DIM
    H = RWKV_HEADS
    C = chunk
    cp = RWKV_CHUNK_PASSES
    seqs = range(bb)

    @pl.when(t == 0)
    def _():
        carry_ref[...] = shift0_ref[...]
        s_ref[...] = s0_ref[...]

    rowid = lax.broadcasted_iota(jnp.int32, (tt, 1), 0)
    last_row = (t_real - 1) % tt
    seg_r = lax.broadcasted_iota(jnp.int32, (W, W), 0) // N
    seg_c = lax.broadcasted_iota(jnp.int32, (W, W), 1) // N
    seg = jnp.where(seg_r == seg_c, 1.0, 0.0).astype(BF16)
    gate = []
    for b in seqs:
        P = p_ref[b]
        prev = jnp.where(rowid == 0, carry_ref[b], pltpu.roll(P, 1, axis=0))
        carry_ref[b] = P[last_row:last_row + 1, :]

        @pl.when(t == nt - 1)
        def _():
            shout_ref[b] = P[last_row:last_row + 1, :]

        Pm = P + (prev - P) * mu_ref[...]
        r = Pm[:, 0:W]
        k = Pm[:, W:2 * W]
        v = Pm[:, 2 * W:3 * W]
        o = 3 * W
        wd = Pm[:, o:o + RWKV_DECAY_RANK]
        ad = Pm[:, o + RWKV_DECAY_RANK:o + RWKV_DECAY_RANK + RWKV_ICLR_RANK]
        gd = Pm[:, o + RWKV_DECAY_RANK + RWKV_ICLR_RANK:]
        w_val = -_softplus(-(w0_ref[...] + _mm(jnp.tanh(wd), wup_ref[...], 'nn', GATE_PASSES))) - 0.5
        lw = -jnp.exp(w_val)
        a = _sigmoid(a0_ref[...] + _mm(ad, aup_ref[...], 'nn', GATE_PASSES))
        gate.append(_mm(_sigmoid(gd), gup_ref[...], 'nn', GATE_PASSES))
        kk = k * kk_ref[...]
        kmod = k * (1.0 + (a - 1.0) * ka_ref[...])
        if t_real % tt != 0:
            valid = (t * tt + rowid) < t_real
            lw = jnp.where(valid, lw, 0.0)
            kk = jnp.where(valid, kk, 0.0)
            kmod = jnp.where(valid, kmod, 0.0)
            v = jnp.where(valid, v, 0.0)
        hi, md, lo = _split3(kk * kk)
        ss = _dot(hi, seg) + _dot(md, seg) + _dot(lo, seg)
        kkn = kk / jnp.maximum(jnp.sqrt(ss), 1e-12)
        r_s[b] = r
        k_s[b] = kmod
        v_s[b] = v
        kk_s[b] = kkn
        b_s[b] = kkn * a
        lw_s[b] = lw

    tri = jnp.where(_tril_ones(C), 1.0, 0.0).astype(BF16)
    low_incl = _tril_ones(C)
    low_strict = _tril_ones(C, strict=True)
    eye = lax.broadcasted_iota(jnp.int32, (N, N), 0) == lax.broadcasted_iota(jnp.int32, (N, N), 1)
    mid = max(C // 2 - 1, 0)
    pairs = [(b, slice(h * N, (h + 1) * N), h) for b in seqs for h in range(H)]
    idx = range(len(pairs))

    def chunk_body(c, carry):
        rows = pl.ds(pl.multiple_of(c * C, C), C)
        kk_hat, r_hat, k_til, b_til, kk_abs, r_abs, k_end, b_end, gam, r_c, k_c, v_c = ([] for _ in range(12))
        for b in seqs:
            lw_c = lw_s[b, rows, :]
            cum = _cumsum_rows(lw_c, tri)
            cume = cum - lw_c
            rho = cum[mid:mid + 1, :]
            cum_last = cum[C - 1:C, :]
            r_c.append(r_s[b, rows, :])
            k_c.append(k_s[b, rows, :])
            v_c.append(v_s[b, rows, :])
            kk_c = kk_s[b, rows, :]
            b_c = b_s[b, rows, :]
            e_out = jnp.exp(rho - cum)
            e_end = jnp.exp(cum_last - cum)
            kk_hat.append(kk_c * jnp.exp(cume - rho))
            r_hat.append(r_c[b] * jnp.exp(cum - rho))
            k_til.append(k_c[b] * e_out)
            b_til.append(b_c * e_out)
            kk_abs.append(kk_c * jnp.exp(cume))
            r_abs.append(r_c[b] * jnp.exp(cum))
            k_end.append(k_c[b] * e_end)
            b_end.append(b_c * e_end)
            gam.append(jnp.exp(cum_last))

        A = [_mm(jnp.concatenate([kk_hat[b][:, s], r_hat[b][:, s]], axis=0),
                 jnp.concatenate([k_til[b][:, s], b_til[b][:, s]], axis=0), 'nt', cp) for b, s, _ in pairs]
        Lk = [jnp.where(low_strict, x[0:C, 0:C], 0.0) for x in A]
        Lb = [jnp.where(low_strict, x[0:C, C:2 * C], 0.0) for x in A]
        Ark = [jnp.where(low_incl, x[C:2 * C, 0:C], 0.0) for x in A]
        Arb = [jnp.where(low_incl, x[C:2 * C, C:2 * C], 0.0) for x in A]
        X = [jnp.concatenate([kk_abs[b][:, s], _mm(Lk[i], v_c[b][:, s], 'nn', cp)], axis=1)
             for i, (b, s, _) in zip(idx, pairs)]
        X = [X[i] - _mm(Lb[i], X[i], 'nn', cp) for i in idx]
        Lp = Lb
        p = 2
        while p < C:
            Lp = [_mm(x, x, 'nn', cp) for x in Lp]
            X = [X[i] + _mm(Lp[i], X[i], 'nn', cp) for i in idx]
            p *= 2
        RY = [jnp.concatenate([r_abs[b][:, s], _mm(Ark[i], v_c[b][:, s], 'nn', cp)], axis=1)
              - _mm(Arb[i], X[i], 'nn', cp) for i, (b, s, _) in zip(idx, pairs)]
        MN = [_mm(X[i], b_end[b][:, s], 'tn', cp) for i, (b, s, _) in zip(idx, pairs)]
        VK = [_mm(v_c[b][:, s], k_end[b][:, s], 'tn', cp) for b, s, _ in pairs]
        for i, (b, s, h) in zip(idx, pairs):
            S = s_ref[b, h]
            Y = _mm(RY[i][:, 0:N], S, 'nt', STATE_PASSES) + RY[i][:, N:2 * N]
            trans = jnp.where(eye, jnp.broadcast_to(gam[b][:, s], (N, N)), 0.0) - MN[i][0:N]
            s_ref[b, h] = _mm(S, trans, 'nn', STATE_PASSES) + (VK[i] - MN[i][N:2 * N])
            mu = jnp.mean(Y, axis=-1, keepdims=True)
            var = jnp.mean(jnp.square(Y - mu), axis=-1, keepdims=True)
            yn = (Y - mu) * lax.rsqrt(var + RWKV_GN_EPS) * lnw_ref[:, s] + lnb_ref[:, s]
            bonus = jnp.sum(r_c[b][:, s] * k_c[b][:, s] * rk_ref[:, s], axis=-1, keepdims=True) * v_c[b][:, s]
            y_s[b, rows, s] = yn + bonus
        return carry

    lax.fori_loop(0, tt // C, chunk_body, 0)
    for b in seqs:
        y_ref[b] = (y_s[b] * gate[b]).astype(y_ref.dtype)

    @pl.when(t == nt - 1)
    def _():
        sout_ref[...] = s_ref[...]


def rwkv_branch(P, shift0, S0, lw, *, batch, t_pad, t_real, tt, chunk, bb=1):
    nt = t_pad // tt
    W = RWKV_WIDTH
    vec = lambda a: a.reshape(1, -1)
    full = lambda shape: pl.BlockSpec(shape, lambda b, t: (0,) * len(shape))
    state = (bb, RWKV_HEADS, RWKV_HEAD_DIM, RWKV_HEAD_DIM)
    kern = functools.partial(_rwkv_kernel, t_real=t_real, tt=tt, chunk=chunk, bb=bb)
    y, s_out, sh_out = pl.pallas_call(
        kern,
        out_shape=(
            jax.ShapeDtypeStruct((batch, t_pad, W), _act_dtype(tt)),
            jax.ShapeDtypeStruct((batch, RWKV_HEADS, RWKV_HEAD_DIM, RWKV_HEAD_DIM), F32),
            jax.ShapeDtypeStruct((batch, 1, RWKV_PROJ), F32),
        ),
        grid=(batch // bb, nt),
        in_specs=[
            pl.BlockSpec((bb, tt, RWKV_PROJ), lambda b, t: (b, t, 0)),
            pl.BlockSpec((bb, 1, RWKV_PROJ), lambda b, t: (b, 0, 0)),
            pl.BlockSpec(state, lambda b, t: (b, 0, 0, 0)),
            full((1, RWKV_PROJ)),
            full((1, W)),
            full((RWKV_DECAY_RANK, W)),
            full((1, W)),
            full((RWKV_ICLR_RANK, W)),
            full((RWKV_GATE_RANK, W)),
            full((1, W)), full((1, W)), full((1, W)), full((1, W)), full((1, W)),
        ],
        out_specs=(
            pl.BlockSpec((bb, tt, W), lambda b, t: (b, t, 0)),
            pl.BlockSpec(state, lambda b, t: (b, 0, 0, 0)),
            pl.BlockSpec((bb, 1, RWKV_PROJ), lambda b, t: (b, 0, 0)),
        ),
        scratch_shapes=[
            pltpu.VMEM((bb, 1, RWKV_PROJ), F32),
            pltpu.VMEM(state, F32),
        ] + [pltpu.VMEM((bb, tt, W), F32) for _ in range(7)],
        compiler_params=_cparams("parallel", "arbitrary"),
        name="rwkv7_branch",
    )(P, shift0.reshape(batch, 1, RWKV_PROJ), S0,
      vec(lw['rwkv_mu']), vec(lw['rwkv_w0']), lw['rwkv_w_up'], vec(lw['rwkv_a0']), lw['rwkv_a_up'],
      lw['rwkv_g_up'], vec(lw['rwkv_k_k']), vec(lw['rwkv_k_a']), vec(lw['rwkv_r_k']),
      vec(lw['rwkv_ln_w']), vec(lw['rwkv_ln_b']))
    return y.reshape(batch * t_pad, W), s_out, sh_out.reshape(batch, RWKV_PROJ)


def _gla_kernel(g_ref, s0_ref, aup_ref, abias_ref, onorm_ref,
                y_ref, sout_ref,
                s_ref, la_s,
                *, t_real, tt, chunk, sub, bb):
    t = pl.program_id(1)
    nt = pl.num_programs(1)
    KW, VW = GLA_K_WIDTH, GLA_V_WIDTH
    dk, dv = GLA_KEY_DIM, GLA_VALUE_DIM
    C = chunk
    cp = GLA_CHUNK_PASSES
    seqs = range(bb)

    @pl.when(t == 0)
    def _():
        s_ref[...] = s0_ref[...]

    padded = t_real % tt != 0
    for b in seqs:
        gl = g_ref[b, :, 2 * KW + 2 * VW:GLA_SECT]
        x = _mm(gl, aup_ref[...], 'nn', GATE_PASSES) + abias_ref[...]
        la = -_softplus(-x) * (1.0 / GLA_TAU)
        if padded:
            rowid = lax.broadcasted_iota(jnp.int32, (tt, 1), 0)
            valid = (t * tt + rowid) < t_real
            la = jnp.where(valid, la, 0.0)
        la_s[b] = la

    tri = jnp.where(_tril_ones(C), 1.0, 0.0).astype(BF16)
    ones_cv = jnp.ones((C, dv), BF16)
    low_incl = _tril_ones(C)
    crow = lax.broadcasted_iota(jnp.int32, (C, 1), 0)
    nsub = C // sub
    pairs = [(b, h) for b in seqs for h in range(GLA_HEADS)]
    idx = range(len(pairs))

    def chunk_body(c, carry):
        rows = pl.ds(pl.multiple_of(c * C, C), C)
        la_c = [la_s[b, rows, :] for b in seqs]
        cum_all = [_cumsum_rows(x, tri) for x in la_c]
        if padded:
            vmask = (t * tt + c * C + crow) < t_real
        q, k, v, cums = [], [], [], []
        for b, h in pairs:
            q.append(g_ref[b, rows, h * dk:(h + 1) * dk] * (dk ** -0.5))
            k_h = g_ref[b, rows, KW + h * dk:KW + (h + 1) * dk]
            v_h = g_ref[b, rows, 2 * KW + h * dv:2 * KW + (h + 1) * dv]
            if padded:
                k_h = jnp.where(vmask, k_h, 0.0)
                v_h = jnp.where(vmask, v_h, 0.0)
            k.append(k_h)
            v.append(v_h)
            cums.append(cum_all[b][:, h * dk:(h + 1) * dk])
        bounds = [[jnp.zeros((1, dk), F32)] + [cm[i * sub - 1:i * sub, :] for i in range(1, nsub)] for cm in cums]
        qt = [q[p] * jnp.exp(cums[p] - jnp.concatenate(
            [jnp.broadcast_to(x, (sub, dk)) for x in bounds[p]], axis=0)) for p in idx]
        att = []
        for p in idx:
            att_rows = []
            for i in range(nsub):
                kt = k[p] * jnp.exp(jnp.where(crow < (i + 1) * sub, bounds[p][i] - cums[p], NEG_INF))
                att_rows.append(_mm(qt[p][i * sub:(i + 1) * sub], kt, 'nt', cp))
            att.append(jnp.where(low_incl, jnp.concatenate(att_rows, axis=0), 0.0))
        intra = [_mm(att[p], v[p], 'nn', cp) for p in idx]
        kv = [_mm(k[p] * jnp.exp(cums[p][C - 1:C, :] - cums[p]), v[p], 'tn', STATE_PASSES) for p in idx]
        tot = []
        for b, h in pairs:
            hi, md, lo = _split3(la_c[b][:, h * dk:(h + 1) * dk])
            tot.append(_dot_tn(hi, ones_cv) + _dot_tn(md, ones_cv) + _dot_tn(lo, ones_cv))
        for p, (b, h) in zip(idx, pairs):
            S = s_ref[b, h]
            o = intra[p] + _mm(q[p] * jnp.exp(cums[p]), S, 'nn', STATE_PASSES)
            s_ref[b, h] = S * jnp.exp(tot[p]) + kv[p]
            r_h = g_ref[b, rows, 2 * KW + VW + h * dv:2 * KW + VW + (h + 1) * dv]
            ms = jnp.mean(o * o, axis=-1, keepdims=True)
            on = o * lax.rsqrt(ms + NORM_EPS) * onorm_ref[...]
            y_ref[b, rows, h * dv:(h + 1) * dv] = (on * (r_h * _sigmoid(r_h))).astype(y_ref.dtype)
        return carry

    lax.fori_loop(0, tt // C, chunk_body, 0)

    @pl.when(t == nt - 1)
    def _():
        sout_ref[...] = s_ref[...]


def gla_branch(G, S0, lw, *, batch, t_pad, t_real, tt, chunk, sub, col_block=0, bb=1):
    nt = t_pad // tt
    full = lambda shape: pl.BlockSpec(shape, lambda b, t: (0,) * len(shape))
    aup = jnp.zeros((GLA_GATE_PAD, GLA_K_WIDTH), F32).at[:GLA_GATE_RANK].set(lw['gla_a_up'])
    state = (bb, GLA_HEADS, GLA_KEY_DIM, GLA_VALUE_DIM)
    kern = functools.partial(_gla_kernel, t_real=t_real, tt=tt, chunk=chunk, sub=sub, bb=bb)
    y, s_out = pl.pallas_call(
        kern,
        out_shape=(
            jax.ShapeDtypeStruct((batch, t_pad, GLA_V_WIDTH), _act_dtype(tt)),
            jax.ShapeDtypeStruct((batch, GLA_HEADS, GLA_KEY_DIM, GLA_VALUE_DIM), F32),
        ),
        grid=(batch // bb, nt),
        in_specs=[
            pl.BlockSpec((bb, tt, PROJ_SLOT), lambda b, t: (b, t, col_block)),
            pl.BlockSpec(state, lambda b, t: (b, 0, 0, 0)),
            full((GLA_GATE_PAD, GLA_K_WIDTH)),
            full((1, GLA_K_WIDTH)),
            full((1, GLA_VALUE_DIM)),
        ],
        out_specs=(
            pl.BlockSpec((bb, tt, GLA_V_WIDTH), lambda b, t: (b, t, 0)),
            pl.BlockSpec(state, lambda b, t: (b, 0, 0, 0)),
        ),
        scratch_shapes=[
            pltpu.VMEM(state, F32),
            pltpu.VMEM((bb, tt, GLA_K_WIDTH), F32),
        ],
        compiler_params=_cparams("parallel", "arbitrary"),
        name="gla_branch",
    )(G, S0, aup, lw['gla_a_bias'].reshape(1, -1), lw['gla_o_norm'].reshape(1, -1))
    return y.reshape(batch * t_pad, GLA_V_WIDTH), s_out


def _norm_rope(x, g, cos, sin):
    ms = jnp.mean(x * x, axis=-1, keepdims=True)
    y = x * lax.rsqrt(ms + NORM_EPS) * g
    return y * cos + pltpu.roll(y, MOBA_HEAD_DIM // 2, axis=1) * sin


def _moba_prep_prompt_kernel(q_ref, k_ref, v_ref, cos_ref, sin_ref, gq_ref, gk_ref, *rest, first):
    qo_ref, ko_ref, vo_ref, kb_ref, vt_ref, km_ref = rest[-6:]
    Dh = MOBA_HEAD_DIM
    cos = cos_ref[...]
    sin = sin_ref[...]
    rows = q_ref.shape[0]
    if first:
        later = (ko_ref.shape[0] - 1,) + ko_ref.shape[1:]
        if later[0]:
            ko_ref[1:] = jnp.zeros(later, F32)
            vo_ref[1:] = jnp.zeros(later, F32)
        ko_ref, vo_ref = ko_ref.at[0], vo_ref.at[0]
    for h in range(MOBA_HEADS):
        sl = slice(h * Dh, (h + 1) * Dh)
        qo_ref[:, sl] = _norm_rope(q_ref[:, sl], gq_ref[...], cos, sin)
        kr = _norm_rope(k_ref[:, sl], gk_ref[...], cos, sin)
        ko_ref[:, h, :] = kr
        vo_ref[:, h, :] = v_ref[:, sl]
        kb_ref[:, sl] = kr.astype(BF16)
        km_ref[0, :, sl] = jnp.sum(kr, axis=0, keepdims=True) * (1.0 / rows)
        vt_ref[0, h, 0, 0:Dh, :] = v_ref[:, sl].T.astype(BF16)
        vt_ref[0, h, 0, Dh:VT_ROWS, :] = jnp.ones((VT_ROWS - Dh, rows), BF16)


def moba_prep_prompt(M, cos, sin, gq, gk, kv_all, *, layer, depth, batch, seq, col0=0):
    m = M.shape[0]
    W = MOBA_WIDTH
    blk = MOBA_BLOCK
    Dh = MOBA_HEAD_DIM
    H = MOBA_HEADS
    nq = seq // blk
    nblk = m // blk
    first = kv_all is None
    prev = () if first else tuple(kv_all)
    if first:
        assert layer == 0
        slab = pl.BlockSpec((depth, blk, H, Dh), lambda i: (0, i, 0, 0))
    else:
        slab = pl.BlockSpec((None, blk, H, Dh), lambda i: (layer, i, 0, 0))
    return pl.pallas_call(
        functools.partial(_moba_prep_prompt_kernel, first=first),
        out_shape=(
            jax.ShapeDtypeStruct((m, W), F32),
            jax.ShapeDtypeStruct((depth, m, H, Dh), F32),
            jax.ShapeDtypeStruct((depth, m, H, Dh), F32),
            jax.ShapeDtypeStruct((m, W), BF16),
            jax.ShapeDtypeStruct((batch, MOBA_HEADS, nq, VT_ROWS, blk), BF16),
            jax.ShapeDtypeStruct((nblk, 1, W), F32),
        ),
        grid=(nblk,),
        in_specs=[
            pl.BlockSpec((blk, W), lambda i: (i, col0)),
            pl.BlockSpec((blk, W), lambda i: (i, col0 + 1)),
            pl.BlockSpec((blk, W), lambda i: (i, col0 + 2)),
            pl.BlockSpec((blk, Dh), lambda i: (i % nq, 0)),
            pl.BlockSpec((blk, Dh), lambda i: (i % nq, 0)),
            pl.BlockSpec((1, Dh), lambda i: (0, 0)),
            pl.BlockSpec((1, Dh), lambda i: (0, 0)),
        ] + [pl.BlockSpec(memory_space=pl.ANY)] * len(prev),
        out_specs=(
            pl.BlockSpec((blk, W), lambda i: (i, 0)),
            slab,
            slab,
            pl.BlockSpec((blk, W), lambda i: (i, 0)),
            pl.BlockSpec((1, MOBA_HEADS, 1, VT_ROWS, blk), lambda i: (i // nq, 0, i % nq, 0, 0)),
            pl.BlockSpec((1, 1, W), lambda i: (i, 0, 0)),
        ),
        input_output_aliases={7 + j: 1 + j for j in range(len(prev))},
        compiler_params=_cparams("parallel"),
        name="moba_qkv_prep",
    )(M, M, M, cos, sin, gq.reshape(1, -1), gk.reshape(1, -1), *prev)


def _moba_prep_sample_kernel(q_ref, k_ref, cos_ref, sin_ref, gq_ref, gk_ref, qo_ref, ko_ref):
    Dh = MOBA_HEAD_DIM
    cos = cos_ref[...]
    sin = sin_ref[...]
    for h in range(MOBA_HEADS):
        sl = slice(h * Dh, (h + 1) * Dh)
        qo_ref[:, sl] = _norm_rope(q_ref[:, sl], gq_ref[...], cos, sin)
        ko_ref[:, sl] = _norm_rope(k_ref[:, sl], gk_ref[...], cos, sin)


def moba_prep_sample(M, cos, sin, gq, gk, *, col0=0):
    m = M.shape[0]
    W = MOBA_WIDTH
    Dh = MOBA_HEAD_DIM
    return pl.pallas_call(
        _moba_prep_sample_kernel,
        out_shape=(jax.ShapeDtypeStruct((m, W), F32), jax.ShapeDtypeStruct((m, W), F32)),
        grid=(1,),
        in_specs=[
            pl.BlockSpec((m, W), lambda i: (0, col0)),
            pl.BlockSpec((m, W), lambda i: (0, col0 + 1)),
            pl.BlockSpec((m, Dh), lambda i: (0, 0)),
            pl.BlockSpec((m, Dh), lambda i: (0, 0)),
            pl.BlockSpec((1, Dh), lambda i: (0, 0)),
            pl.BlockSpec((1, Dh), lambda i: (0, 0)),
        ],
        out_specs=(pl.BlockSpec((m, W), lambda i: (0, 0)), pl.BlockSpec((m, W), lambda i: (0, 0))),
        compiler_params=_cparams("arbitrary"),
        name="moba_qk_prep_sample",
    )(M, M, cos, sin, gq.reshape(1, -1), gk.reshape(1, -1))


def _block_rank(bs, n_valid, axis):
    nb = bs.shape[axis]
    idx = lax.broadcasted_iota(jnp.int32, bs.shape, axis)
    rank = jnp.zeros(bs.shape, jnp.int32)
    for mm in range(nb):
        one = bs[mm:mm + 1, :] if axis == 0 else bs[:, mm:mm + 1]
        beats = (one > bs) | ((one == bs) & (idx > mm))
        if n_valid is not None:
            beats = beats & (mm < n_valid)
        rank = rank + jnp.where(beats, 1, 0)
    return rank


def _moba_attn_kernel(q_ref, kb_ref, vt_ref, km_ref, o_ref, sel_ref, s_ref):
    i = pl.program_id(2)
    blk = MOBA_BLOCK
    Dh = MOBA_HEAD_DIM
    heads = range(MOBA_HEADS_PER_STEP)
    hs = [slice(h * Dh, (h + 1) * Dh) for h in heads]
    nb = km_ref.shape[1]
    grp = next(g for g in (4, 2, 1) if nb % g == 0)
    c2 = (Dh ** -0.5) * LOG2_E
    blk_id = lax.broadcasted_iota(jnp.int32, (nb, blk), 0)
    causal = (lax.broadcasted_iota(jnp.int32, (blk, blk), 0)
              <= lax.broadcasted_iota(jnp.int32, (blk, blk), 1))
    n_groups = (i + grp - 1) // grp

    qT = [q_ref[:, s].T for s in hs]
    bsT = [_dot(km_ref[0, :, s], qT[h], HIGHEST) for h, s in zip(heads, hs)]
    for h in heads:
        rank = _block_rank(bsT[h], i, 0)
        sel_ref[h] = jnp.where((blk_id < i) & (rank < MOBA_TOPK), 1.0, 0.0)
    qb = [x.astype(BF16) for x in qT]

    own = pl.ds(pl.multiple_of(i * blk, blk), blk)
    s_own = [jnp.where(causal, _dot(kb_ref[0, own, s], qb[h]) * c2, NEG_INF) for h, s in zip(heads, hs)]
    for h in heads:
        s_ref[h, nb * blk:(nb + 1) * blk, :] = s_own[h]

    def score_group(gi, m):
        m = list(m)
        for u in range(grp):
            j = gi * grp + u
            rows = pl.ds(pl.multiple_of(j * blk, blk), blk)
            raw = [_dot(kb_ref[0, rows, s], qb[h]) for h, s in zip(heads, hs)]
            for h in heads:
                picked = sel_ref[h, pl.ds(j, 1), :] > 0.0
                sc = jnp.where(picked, raw[h] * c2, NEG_INF)
                s_ref[h, rows, :] = sc
                m[h] = jnp.maximum(m[h], jnp.max(sc, axis=0, keepdims=True))
        return tuple(m)

    m = lax.fori_loop(0, n_groups, score_group, tuple(jnp.max(x, axis=0, keepdims=True) for x in s_own))

    def weighted_values(j, rows):
        p = [jnp.exp2(s_ref[h, rows, :] - m[h]).astype(BF16) for h in heads]
        return [_dot(vt_ref[0, h, j], p[h]) for h in heads]

    def value_group(gi, acc):
        acc = list(acc)
        for u in range(grp):
            j = gi * grp + u
            new = weighted_values(j, pl.ds(pl.multiple_of(j * blk, blk), blk))
            acc = [a + n for a, n in zip(acc, new)]
        return tuple(acc)

    acc = lax.fori_loop(0, n_groups, value_group, tuple(weighted_values(i, pl.ds(nb * blk, blk))))
    for h, s in zip(heads, hs):
        o_ref[:, s] = (acc[h][0:Dh] / acc[h][Dh:Dh + 1]).T.astype(o_ref.dtype)


def moba_attention_prompt(q_rot, kb, vt, kmean, *, batch, seq):
    W = MOBA_WIDTH
    blk = MOBA_BLOCK
    hp = MOBA_HEADS_PER_STEP
    wide = hp * MOBA_HEAD_DIM
    nq = seq // blk
    nb = kmean.shape[1]
    return pl.pallas_call(
        _moba_attn_kernel,
        out_shape=jax.ShapeDtypeStruct((batch * seq, W), BF16),
        grid=(batch, MOBA_HEADS // hp, nq),
        in_specs=[
            pl.BlockSpec((blk, wide), lambda b, h, i: (b * nq + i, h)),
            pl.BlockSpec((1, seq, wide), lambda b, h, i: (b, 0, h)),
            pl.BlockSpec((1, hp, nb, VT_ROWS, blk), lambda b, h, i: (b, h, 0, 0, 0)),
            pl.BlockSpec((1, nb, wide), lambda b, h, i: (b, 0, h)),
        ],
        out_specs=pl.BlockSpec((blk, wide), lambda b, h, i: (b * nq + i, h)),
        scratch_shapes=[pltpu.VMEM((hp, nb, blk), F32), pltpu.VMEM((hp, (nb + 1) * blk, blk), F32)],
        compiler_params=_cparams("parallel", "parallel", "arbitrary"),
        name="moba_attention",
    )(q_rot, kb.reshape(batch, seq, W), vt, kmean)


PAGES_PER_STEP = 16
PAGES_PER_BLOCK = MOBA_BLOCK // PAGE_SIZE


def _moba_select_kernel(q_ref, km_ref, o_ref):
    Dh = MOBA_HEAD_DIM
    rows = []
    for h in range(MOBA_HEADS):
        rows.append(_dot_nt(q_ref[0, :, h * Dh:(h + 1) * Dh], km_ref[0, h], HIGHEST))
    bs = jnp.concatenate(rows, axis=0)
    nb = bs.shape[1]
    rank = _block_rank(bs, None, 1)
    lane = lax.broadcasted_iota(jnp.int32, bs.shape, 1)
    olane = lax.broadcasted_iota(jnp.int32, (MOBA_HEADS, 128), 1)
    out = jnp.zeros((MOBA_HEADS, 128), jnp.int32)
    for s in range(min(MOBA_TOPK, nb)):
        idx = jnp.sum(jnp.where(rank == s, lane, 0), axis=-1, keepdims=True)
        out = jnp.where(olane == s, idx, out)
    o_ref[0] = out


def moba_select_sample(q3, kmean_past):
    bs, _, W = q3.shape
    nb = kmean_past.shape[2]
    return pl.pallas_call(
        _moba_select_kernel,
        out_shape=jax.ShapeDtypeStruct((bs, MOBA_HEADS, 128), jnp.int32),
        grid=(bs,),
        in_specs=[
            pl.BlockSpec((1, 1, W), lambda b: (b, 0, 0)),
            pl.BlockSpec((1, MOBA_HEADS, nb, MOBA_HEAD_DIM), lambda b: (b, 0, 0, 0)),
        ],
        out_specs=pl.BlockSpec((1, MOBA_HEADS, 128), lambda b: (b, 0, 0)),
        compiler_params=_cparams("parallel"),
        name="moba_select_sample",
    )(q3, kmean_past)


def _moba_decode_kernel(sel_ref, pt_ref, q_ref, kn_ref, vn_ref, ck_ref, cv_ref, o_ref, kbuf, vbuf, sem,
                        *, layer, n_pg, n_sel, n_pages):
    step = pl.program_id(0)
    scale = MOBA_HEAD_DIM ** -0.5

    def page_copies(st, slot):
        b = st // MOBA_HEADS
        h = st % MOBA_HEADS
        copies = []
        for e in range(n_pg):
            blk = sel_ref[st * n_sel + e // PAGES_PER_BLOCK]
            page = pt_ref[b * n_pages + blk * PAGES_PER_BLOCK + e % PAGES_PER_BLOCK]
            copies.append(pltpu.make_async_copy(ck_ref.at[layer, page, :, h, :], kbuf.at[slot, e], sem.at[slot, 0, e]))
            copies.append(pltpu.make_async_copy(cv_ref.at[layer, page, :, h, :], vbuf.at[slot, e], sem.at[slot, 1, e]))
        return copies

    @pl.when(step == 0)
    def _():
        for c in page_copies(0, 0):
            c.start()

    @pl.when(step + 1 < pl.num_programs(0))
    def _():
        for c in page_copies(step + 1, (step + 1) % 2):
            c.start()

    slot = step % 2
    for c in page_copies(step, slot):
        c.wait()

    rows = n_pg * PAGE_SIZE
    q = q_ref[0]
    s_own = jnp.sum(q * kn_ref[0], axis=-1, keepdims=True) * scale
    kp = kbuf[slot].reshape(rows, MOBA_HEAD_DIM).astype(BF16)
    vp = vbuf[slot].reshape(rows, MOBA_HEAD_DIM).astype(BF16)
    sc = _dot_nt(q.astype(BF16), kp) * scale
    m = jnp.maximum(s_own, jnp.max(sc, axis=-1, keepdims=True))
    p_own = jnp.exp(s_own - m)
    p = jnp.exp(sc - m)
    l = p_own + jnp.sum(p, axis=-1, keepdims=True)
    acc = p_own * vn_ref[0] + _dot(p.astype(BF16), vp)
    o_ref[0] = (acc / l).astype(o_ref.dtype)


def moba_decode(sel_flat, pt_flat, q3, k3, M3, cache_k, cache_v, *, layer, n_pages, n_sel, v_col):
    bs, _, W = q3.shape
    Dh = MOBA_HEAD_DIM
    H = MOBA_HEADS
    n_pg = n_sel * PAGES_PER_BLOCK
    vec = lambda col: pl.BlockSpec((1, 1, Dh), lambda st, sel, pt: (st // H, 0, col(st % H)))
    hbm = pl.BlockSpec(memory_space=pl.ANY)
    return pl.pallas_call(
        functools.partial(_moba_decode_kernel, layer=layer, n_pg=n_pg, n_sel=n_sel, n_pages=n_pages),
        out_shape=jax.ShapeDtypeStruct((bs, 1, W), F32),
        grid_spec=pltpu.PrefetchScalarGridSpec(
            num_scalar_prefetch=2,
            grid=(bs * H,),
            in_specs=[vec(lambda h: h), vec(lambda h: h), vec(lambda h: v_col // Dh + h), hbm, hbm],
            out_specs=vec(lambda h: h),
            scratch_shapes=[
                pltpu.VMEM((2, n_pg, PAGE_SIZE, Dh), F32),
                pltpu.VMEM((2, n_pg, PAGE_SIZE, Dh), F32),
                pltpu.SemaphoreType.DMA((2, 2, n_pg)),
            ],
        ),
        compiler_params=_cparams("arbitrary"),
        name="moba_decode",
    )(sel_flat, pt_flat, q3, k3, M3, cache_k, cache_v)


def _merge_kernel(ya_ref, yb_ref, yc_ref, wa_ref, wb_ref, wc_ref, z0_ref, z1_ref, z2_ref, o_ref):
    acc = _sigmoid(z0_ref[...]) * _dot(ya_ref[...].astype(BF16), wa_ref[...])
    acc = acc + _sigmoid(z1_ref[...]) * _dot(yb_ref[...].astype(BF16), wb_ref[...])
    acc = acc + _sigmoid(z2_ref[...]) * _dot(yc_ref[...].astype(BF16), wc_ref[...])
    o_ref[...] = acc.astype(o_ref.dtype)


def gated_merge(ya, yb, yc, wa, wb, wc, proj, layer, *, z_col, tn=PROJ_TILE, tm_pref=512):
    m = ya.shape[0]
    d = wa.shape[2]
    tm = _row_tile(m, tm_pref)
    nj = d // tn
    z0 = z_col // tn
    wspec = lambda w: pl.BlockSpec((None, w.shape[1], tn), lambda i, j: (layer, 0, j))
    return pl.pallas_call(
        _merge_kernel,
        out_shape=jax.ShapeDtypeStruct((m, d), BF16),
        grid=(m // tm, nj),
        in_specs=[
            pl.BlockSpec((tm, ya.shape[1]), lambda i, j: (i, 0)),
            pl.BlockSpec((tm, yb.shape[1]), lambda i, j: (i, 0)),
            pl.BlockSpec((tm, yc.shape[1]), lambda i, j: (i, 0)),
            wspec(wa), wspec(wb), wspec(wc),
            pl.BlockSpec((tm, tn), lambda i, j: (i, z0 + j)),
            pl.BlockSpec((tm, tn), lambda i, j: (i, z0 + nj + j)),
            pl.BlockSpec((tm, tn), lambda i, j: (i, z0 + 2 * nj + j)),
        ],
        out_specs=pl.BlockSpec((tm, tn), lambda i, j: (i, j)),
        compiler_params=_cparams("parallel", "arbitrary"),
        name="gated_merge",
    )(ya, yb, yc, wa, wb, wc, proj, proj, proj)


def _matmul_residual_kernel(a_ref, w_ref, x_ref, o_ref):
    o_ref[...] = x_ref[...] + _dot(a_ref[...], w_ref[...])


def matmul_residual(a, w, x, layer, *, tn, tm_pref=512):
    m, k = a.shape
    n = w.shape[2]
    tm = _row_tile(m, tm_pref)
    return pl.pallas_call(
        _matmul_residual_kernel,
        out_shape=jax.ShapeDtypeStruct((m, n), F32),
        grid=(m // tm, n // tn),
        in_specs=[
            pl.BlockSpec((tm, k), lambda i, j: (i, 0)),
            pl.BlockSpec((None, k, tn), lambda i, j: (layer, 0, j)),
            pl.BlockSpec((tm, tn), lambda i, j: (i, j)),
        ],
        out_specs=pl.BlockSpec((tm, tn), lambda i, j: (i, j)),
        compiler_params=_cparams("parallel", "arbitrary"),
        name="matmul_residual",
    )(a, w, x)


def _ffn_in_kernel(x_ref, g_ref, wg_ref, wv_ref, o_ref, h_ref):
    @pl.when(pl.program_id(1) == 0)
    def _():
        x = x_ref[...]
        ms = jnp.mean(x * x, axis=-1, keepdims=True)
        h_ref[...] = (x * lax.rsqrt(ms + NORM_EPS) * g_ref[...]).astype(BF16)

    h = h_ref[...]
    gate = _dot(h, wg_ref[...])
    val = _dot(h, wv_ref[...])
    o_ref[...] = (gate * _sigmoid(gate) * val).astype(o_ref.dtype)


def _ffn_in_key_means_kernel(pt_ref, x_ref, g_ref, wg_ref, wv_ref, *refs, n_pg, page_steps):
    del pt_ref
    page_refs, (o_ref, km_ref, h_ref) = refs[:n_pg], refs[n_pg:]
    _ffn_in_kernel(x_ref, g_ref, wg_ref, wv_ref, o_ref, h_ref)

    @pl.when(pl.program_id(0) * pl.num_programs(1) + pl.program_id(1) < page_steps)
    def _():
        for n in range(n_pg // PAGES_PER_BLOCK):
            tot = page_refs[PAGES_PER_BLOCK * n][0, 0].sum(axis=0)
            for e in range(1, PAGES_PER_BLOCK):
                tot = tot + page_refs[PAGES_PER_BLOCK * n + e][0, 0].sum(axis=0)
            tot = tot * (1.0 / MOBA_BLOCK)
            for hd in range(MOBA_HEADS):
                km_ref[0, hd, pl.ds(n, 1), :] = tot[hd:hd + 1, :]


def ffn_in(x, g, w, layer, *, tn=512, tm_pref=1024):
    m, d = x.shape
    hidden = w.shape[2] // 2
    tm = _row_tile(m, tm_pref)
    nj = hidden // tn
    return pl.pallas_call(
        _ffn_in_kernel,
        out_shape=jax.ShapeDtypeStruct((m, hidden), BF16),
        grid=(m // tm, nj),
        in_specs=[
            pl.BlockSpec((tm, d), lambda i, j: (i, 0)),
            pl.BlockSpec((1, d), lambda i, j: (0, 0)),
            pl.BlockSpec((None, d, tn), lambda i, j: (layer, 0, j)),
            pl.BlockSpec((None, d, tn), lambda i, j: (layer, 0, nj + j)),
        ],
        out_specs=pl.BlockSpec((tm, tn), lambda i, j: (i, j)),
        scratch_shapes=[pltpu.VMEM((tm, d), BF16)],
        compiler_params=_cparams("parallel", "arbitrary"),
        name="ffn_in_swiglu",
    )(x, g.reshape(1, d), w, w)


def ffn_in_key_means(x, g, w, layer, page_table, cache_k, *, tn=512, tm_pref=1024):
    m, d = x.shape
    hidden = w.shape[2] // 2
    tm = _row_tile(m, tm_pref)
    nj = hidden // tn
    bs, n_pages = page_table.shape
    pps = min(PAGES_PER_STEP, n_pages)
    per_seq = n_pages // pps
    page_steps = bs * per_seq
    assert n_pages % pps == 0 and page_steps <= (m // tm) * nj

    def page_pos(i, j):
        t = jnp.minimum(i * nj + j, page_steps - 1)
        return t // per_seq, t % per_seq

    def page_spec(e):
        def page_map(i, j, pt):
            b, s = page_pos(i, j)
            return (layer, pt[b * n_pages + s * pps + e], 0, 0, 0)
        return pl.BlockSpec((1, 1, PAGE_SIZE, MOBA_HEADS, MOBA_HEAD_DIM), page_map)

    def km_map(i, j, pt):
        b, s = page_pos(i, j)
        return (b, 0, s, 0)

    return pl.pallas_call(
        functools.partial(_ffn_in_key_means_kernel, n_pg=pps, page_steps=page_steps),
        out_shape=(
            jax.ShapeDtypeStruct((m, hidden), BF16),
            jax.ShapeDtypeStruct((bs, MOBA_HEADS, n_pages // PAGES_PER_BLOCK, MOBA_HEAD_DIM), F32),
        ),
        grid_spec=pltpu.PrefetchScalarGridSpec(
            num_scalar_prefetch=1,
            grid=(m // tm, nj),
            in_specs=[
                pl.BlockSpec((tm, d), lambda i, j, pt: (i, 0)),
                pl.BlockSpec((1, d), lambda i, j, pt: (0, 0)),
                pl.BlockSpec((None, d, tn), lambda i, j, pt: (layer, 0, j)),
                pl.BlockSpec((None, d, tn), lambda i, j, pt: (layer, 0, nj + j)),
            ] + [page_spec(e) for e in range(pps)],
            out_specs=(
                pl.BlockSpec((tm, tn), lambda i, j, pt: (i, j)),
                pl.BlockSpec((1, MOBA_HEADS, pps // PAGES_PER_BLOCK, MOBA_HEAD_DIM), km_map),
            ),
            scratch_shapes=[pltpu.VMEM((tm, d), BF16)],
        ),
        compiler_params=_cparams("arbitrary", "arbitrary"),
        name="ffn_in_swiglu_key_means",
    )(page_table.reshape(-1), x, g.reshape(1, d), w, w, *([cache_k] * pps))


def _rope_tables(pos):
    half = MOBA_HEAD_DIM // 2
    inv = ROPE_THETA ** (-jnp.arange(half, dtype=F32) / half)
    ang = pos.astype(F32)[:, None] * inv[None, :]
    cos, sin = jnp.cos(ang), jnp.sin(ang)
    return jnp.concatenate([cos, cos], axis=-1), jnp.concatenate([-sin, sin], axis=-1)


def _bf16_weights(w):
    o1 = RWKV_PROJ
    o2 = o1 + 3 * MOBA_WIDTH
    o3 = o2 + 2 * GLA_K_WIDTH + GLA_V_WIDTH
    o4 = o3 + GLA_GATE_RANK
    o5 = o4 + GLA_V_WIDTH
    w_in = w['w_in'].astype(BF16)
    total = COL_GATES + w_in.shape[-1] - o5

    def place(lo, hi, at):
        return jnp.pad(w_in[..., lo:hi], ((0, 0), (0, 0), (at, total - at - (hi - lo))))

    packed = (place(0, o1, COL_RWKV)
              + place(o2, o3, COL_GLA) + place(o4, o5, COL_GLA + o3 - o2)
              + place(o3, o4, COL_GLA + o3 - o2 + o5 - o4)
              + place(o1, o2, COL_MOBA) + place(o5, w_in.shape[-1], COL_GATES))
    out = {k: w[k].astype(BF16) for k in ('w_up_rwkv', 'w_up_moba', 'w_up_gla', 'w_out', 'w_ffn_in', 'w_ffn_out')}
    out['w_in'] = packed
    return out


def _mix_and_ffn(x, ya, yb, yc, proj, lw, wb, layer, paged_keys=None):
    merged = gated_merge(ya, yb, yc, wb['w_up_rwkv'], wb['w_up_moba'], wb['w_up_gla'], proj, layer,
                         z_col=COL_GATES, tn=512, tm_pref=1024)
    x = matmul_residual(merged, wb['w_out'], x, layer, tn=1024, tm_pref=1024)
    if paged_keys is None:
        act, kmean = ffn_in(x, lw['norm_ffn'], wb['w_ffn_in'], layer), None
    else:
        act, kmean = ffn_in_key_means(x, lw['norm_ffn'], wb['w_ffn_in'], layer, *paged_keys)
    return matmul_residual(act, wb['w_ffn_out'], x, layer, tn=512, tm_pref=1024), kmean


def _prompt_layer(x, lw, wb, rope, layer, depth, kv_all, paged_keys, *, batch, seq):
    proj = norm_matmul(x, lw['norm_mix'], wb['w_in'], layer)

    tt = min(seq, 256)
    ya, r_S, shift = rwkv_branch(
        proj.reshape(batch, seq, -1), jnp.zeros((batch, RWKV_PROJ), F32),
        jnp.zeros((batch, RWKV_HEADS, RWKV_HEAD_DIM, RWKV_HEAD_DIM), F32), lw,
        batch=batch, t_pad=seq, t_real=seq, tt=tt, chunk=min(tt, RWKV_CHUNK), bb=2 if batch % 2 == 0 else 1)
    pair = 2 if batch % 2 == 0 else 1
    yc, g_S = gla_branch(
        proj.reshape(batch, seq, -1), jnp.zeros((batch, GLA_HEADS, GLA_KEY_DIM, GLA_VALUE_DIM), F32), lw,
        batch=batch, t_pad=seq, t_real=seq, tt=tt, chunk=min(tt, 128), sub=16, col_block=COL_GLA // PROJ_SLOT,
        bb=pair)

    nq = seq // MOBA_BLOCK
    q_rot, k_all, v_all, kb, vt, kmean = moba_prep_prompt(
        proj, rope[0], rope[1], lw['moba_q_norm'], lw['moba_k_norm'], kv_all,
        layer=layer, depth=depth, batch=batch, seq=seq, col0=COL_MOBA // MOBA_WIDTH)
    yb = moba_attention_prompt(q_rot, kb, vt, kmean.reshape(batch, nq, MOBA_WIDTH), batch=batch, seq=seq)

    x, kmean_past = _mix_and_ffn(x, ya, yb, yc, proj, lw, wb, layer, paged_keys)
    return x, (r_S, shift, g_S), (k_all, v_all), kmean_past


def _sample_layer(x, lw, wb, rope, shift0, rwkv_S0, gla_S0, kmean_past, pt_flat, cache_k, cache_v,
                  *, layer, n_pages):
    bs = x.shape[0]
    proj = norm_matmul(x, lw['norm_mix'], wb['w_in'], layer)

    pad_rows = lambda a: jnp.pad(a[:, None, :], ((0, 0), (0, SUBLANES - 1), (0, 0))).reshape(bs * SUBLANES, -1)
    ya, r_S, shift = rwkv_branch(pad_rows(proj[:, COL_RWKV:COL_RWKV + RWKV_PROJ]).reshape(bs, SUBLANES, -1),
                                 shift0, rwkv_S0, lw,
                                 batch=bs, t_pad=SUBLANES, t_real=1, tt=SUBLANES, chunk=SUBLANES)
    yc, g_S = gla_branch(pad_rows(proj[:, COL_GLA:COL_GLA + PROJ_SLOT]).reshape(bs, SUBLANES, -1), gla_S0, lw,
                         batch=bs, t_pad=SUBLANES, t_real=1, tt=SUBLANES, chunk=SUBLANES, sub=SUBLANES)
    ya = ya[::SUBLANES]
    yc = yc[::SUBLANES]

    q_rot, k_rot = moba_prep_sample(proj, rope[0], rope[1], lw['moba_q_norm'], lw['moba_k_norm'],
                                    col0=COL_MOBA // MOBA_WIDTH)
    q3 = q_rot.reshape(bs, 1, MOBA_WIDTH)
    n_sel = min(MOBA_TOPK, kmean_past.shape[2])
    sel = moba_select_sample(q3, kmean_past)[:, :, :n_sel].reshape(-1)
    v_col = COL_MOBA + 2 * MOBA_WIDTH
    yb = moba_decode(sel, pt_flat, q3, k_rot.reshape(bs, 1, MOBA_WIDTH), proj.reshape(bs, 1, -1),
                     cache_k, cache_v, layer=layer, n_pages=n_pages, n_sel=n_sel, v_col=v_col).reshape(bs, MOBA_WIDTH)

    x, _ = _mix_and_ffn(x, ya, yb, yc, proj, lw, wb, layer)
    k_new = k_rot.reshape(bs, 1, MOBA_HEADS, MOBA_HEAD_DIM)
    v_new = proj[:, v_col:COL_GATES].reshape(bs, 1, MOBA_HEADS, MOBA_HEAD_DIM)
    return x, k_new, v_new, r_S, shift, g_S


_LAYER_KEYS = ('norm_mix', 'w_in', 'rwkv_mu', 'rwkv_w0', 'rwkv_w_up', 'rwkv_a0', 'rwkv_a_up', 'rwkv_g_up',
               'rwkv_k_k', 'rwkv_k_a', 'rwkv_r_k', 'rwkv_ln_w', 'rwkv_ln_b', 'moba_q_norm', 'moba_k_norm',
               'gla_a_up', 'gla_a_bias', 'gla_o_norm', 'w_up_rwkv', 'w_up_moba', 'w_up_gla', 'w_out',
               'norm_ffn', 'w_ffn_in', 'w_ffn_out')


def kernel(x_prompt, x_sample, cache_k, cache_v, page_table, state_rwkv, state_rwkv_shift, state_gla, norm_mix, w_in, rwkv_mu, rwkv_w0, rwkv_w_up, rwkv_a0, rwkv_a_up, rwkv_g_up, rwkv_k_k, rwkv_k_a, rwkv_r_k, rwkv_ln_w, rwkv_ln_b, moba_q_norm, moba_k_norm, gla_a_up, gla_a_bias, gla_o_norm, w_up_rwkv, w_up_moba, w_up_gla, w_out, norm_ffn, w_ffn_in, w_ffn_out):
    stacked = dict(zip(_LAYER_KEYS, (
        norm_mix, w_in, rwkv_mu, rwkv_w0, rwkv_w_up, rwkv_a0, rwkv_a_up, rwkv_g_up, rwkv_k_k, rwkv_k_a,
        rwkv_r_k, rwkv_ln_w, rwkv_ln_b, moba_q_norm, moba_k_norm, gla_a_up, gla_a_bias, gla_o_norm,
        w_up_rwkv, w_up_moba, w_up_gla, w_out, norm_ffn, w_ffn_in, w_ffn_out)))
    depth = w_in.shape[0]
    bp, seq, d = x_prompt.shape
    bs, dec_seq, _ = x_sample.shape
    n_pages = page_table.shape[1]
    past_len = n_pages * cache_k.shape[2]
    assert dec_seq == 1 and cache_k.shape[2:] == (PAGE_SIZE, MOBA_HEADS, MOBA_HEAD_DIM)
    assert seq % MOBA_BLOCK == 0 and past_len % MOBA_BLOCK == 0 and past_len // MOBA_BLOCK >= MOBA_TOPK

    pt_flat = page_table.reshape(-1)

    rope_p = _rope_tables(jnp.arange(seq))
    rope_s = _rope_tables(jnp.full((bs,), past_len))

    yp = x_prompt.reshape(bp * seq, d)
    ys = x_sample.reshape(bs, d)
    wb = _bf16_weights(stacked)
    small = [k for k in _LAYER_KEYS if k not in wb]
    outs_p, outs_s = [], []
    kv_all = None
    for l in range(depth):
        lw = {k: stacked[k][l] for k in small}
        yp, states_p, kv_all, kmean_past = _prompt_layer(yp, lw, wb, rope_p, l, depth, kv_all,
                                                         (page_table, cache_k), batch=bp, seq=seq)
        outs_p.append(states_p)
        ys, *rest_s = _sample_layer(ys, lw, wb, rope_s, state_rwkv_shift[l], state_rwkv[l], state_gla[l],
                                    kmean_past, pt_flat, cache_k, cache_v, layer=l, n_pages=n_pages)
        outs_s.append(rest_s)
    stack = lambda outs, i: jnp.stack([o[i] for o in outs])
    kv_shape = (depth, bp, seq, MOBA_HEADS, MOBA_HEAD_DIM)
    return (yp.reshape(bp, seq, d), ys.reshape(bs, 1, d),
            kv_all[0].reshape(kv_shape), kv_all[1].reshape(kv_shape),
            stack(outs_p, 0), stack(outs_p, 1), stack(outs_p, 2),
            stack(outs_s, 0), stack(outs_s, 1), stack(outs_s, 2), stack(outs_s, 3), stack(outs_s, 4))
```
